```python
import math
import jax, jax.numpy as jnp
from jax import lax
import numpy as np

D_MODEL = 2048
BATCH = 4
SEQ = 2048
DEPTH = 2
DEC_BATCH = 128
DEC_SEQ = 4
PAST_LEN = 2048
PAGE_SIZE = 128

DH = 128
H_A = D_MODEL // (2 * DH)
HKV = 2
GRP = H_A // HKV
L_CMP = 32
D_CMP = 16
CMP_R = L_CMP // D_CMP
CMP_HID = 2 * DH
L_SEL = 64
N_TOP = 16
WINDOW = 512
Q_BLOCK = 64
N_KV_SLOTS = 4
H_B = 4
DK_B = D_MODEL // (4 * H_B)
DV_B = D_MODEL // (2 * H_B)
GLA_LR = 16
GLA_GATE_NORM = 16.0
DK_C = 128
H_C = D_MODEL // DK_C
DV_C = D_MODEL // H_C
REC_CHUNK = 16
N_A_LAYERS = (DEPTH + 1) // 2
N_C_LAYERS = DEPTH // 2
EPS = 1e-6
FORCE_SCORE = 1e9
A_SIZES = (H_A * DH, 6 * HKV * DH, 3 * H_A, H_A * DH, H_B * DK_B, H_B * DK_B, H_B * DV_B, GLA_LR, H_B * DV_B)
C_SIZES = (H_C * DK_C, H_C * DK_C, H_C * DV_C, H_C * DV_C)

kernel_name = 'nsa_gla_hgrn2_hybrid_step'


def split_cols(a, sizes):
    return jnp.split(a, np.cumsum(sizes)[:-1].tolist(), axis=-1)


def rmsnorm(x, w):
    xf = x.astype(jnp.float32)
    y = xf * lax.rsqrt(jnp.mean(xf * xf, axis=-1, keepdims=True) + EPS)
    return (y * w.astype(jnp.float32)).astype(x.dtype)


def masked_softmax(s, mask):
    s = jnp.where(mask, s.astype(jnp.float32), -jnp.inf)
    m = jnp.max(s, axis=-1, keepdims=True)
    m = jnp.where(jnp.isfinite(m), m, 0.0)
    e = jnp.where(mask, jnp.exp(s - m), 0.0)
    return e / jnp.maximum(jnp.sum(e, axis=-1, keepdims=True), 1e-30)


def gated_recurrence(q, k, v, log_a, s0):
    B_, T, H, _ = q.shape
    V = v.shape[-1]
    c = math.gcd(T, REC_CHUNK)
    n = T // c
    f32 = jnp.float32

    def blocks(a):
        return a.astype(f32).reshape(B_, n, c, H, a.shape[-1]).transpose(1, 0, 3, 2, 4)

    causal = jnp.tril(jnp.ones((c, c), dtype=bool))[None, None, :, :, None]

    def step(s, xs):
        qc, kc, vc, gc = xs
        b = jnp.cumsum(gc, axis=2)
        decay = jnp.exp(jnp.where(causal, b[:, :, :, None, :] - b[:, :, None, :, :], -jnp.inf))
        att = jnp.einsum('bhik,bhijk,bhjk->bhij', qc, decay, kc)
        o = jnp.einsum('bhij,bhjv->bhiv', att, vc) + jnp.einsum('bhik,bhkv->bhiv', qc * jnp.exp(b), s)
        b_last = b[:, :, -1:, :]
        s = jnp.exp(b_last)[:, :, 0, :, None] * s + jnp.einsum('bhjk,bhjv->bhkv', kc * jnp.exp(b_last - b), vc)
        return s, o

    s_fin, o = lax.scan(step, s0.astype(f32), (blocks(q), blocks(k), blocks(v), blocks(log_a)))
    o = o.transpose(1, 0, 3, 2, 4).reshape(B_, T, H, V)
    return o.astype(v.dtype), s_fin.astype(s0.dtype)


def compress_blocks(x, pe, w1, b1, w2):
    B_, L = x.shape[:2]
    n_str = L // D_CMP
    nc = n_str - CMP_R + 1
    s = x[:, :n_str * D_CMP].reshape(B_, n_str, D_CMP, HKV, DH)
    w1r = w1.reshape(CMP_R, D_CMP, DH, CMP_HID)
    u = jnp.einsum('bnphd,rpde->rbnhe', s, w1r)
    h = b1 + jnp.einsum('rpd,rpde->e', pe.reshape(CMP_R, D_CMP, DH), w1r)
    for r in range(CMP_R):
        h = h + u[r, :, r:r + nc]
    return jax.nn.gelu(h) @ w2


def nsa_attend(q, gates, kv, win_kv, q_start, win_base, cmp_pe, cmp_w1, cmp_b1, cmp_w2):
    B_, Tq = q.shape[:2]
    L = kv.shape[1]
    scale = DH ** -0.5
    kc = compress_blocks(kv[:, :, 0], cmp_pe[0], cmp_w1[0], cmp_b1[0], cmp_w2[0])
    vc = compress_blocks(kv[:, :, 1], cmp_pe[1], cmp_w1[1], cmp_b1[1], cmp_w2[1])
    nc = kc.shape[1]
    ns = -(-L // L_SEL)

    def sel_blocks(a):
        a = jnp.pad(a, ((0, 0), (0, ns * L_SEL - L), (0, 0), (0, 0)))
        return a.reshape(B_, ns, L_SEL, HKV, DH).transpose(0, 3, 1, 2, 4)

    ks_all = sel_blocks(kv[:, :, 2])
    vs_all = sel_blocks(kv[:, :, 3])
    c_start = np.arange(nc)[:, None] * D_CMP
    s_start = np.arange(ns)[None, :] * L_SEL
    overlap = np.clip(np.minimum(c_start + L_CMP, s_start + L_SEL) - np.maximum(c_start, s_start), 0, None)
    cmp_to_sel = jnp.asarray(overlap / D_CMP, dtype=jnp.float32)
    cmp_end = jnp.asarray(np.arange(nc) * D_CMP + L_CMP - 1, dtype=jnp.int32)
    blk = jnp.arange(ns, dtype=jnp.int32)
    n_top = min(N_TOP, ns)
    qb = math.gcd(Tq, Q_BLOCK)
    nq = Tq // qb
    sel_off = jnp.arange(L_SEL, dtype=jnp.int32)
    win_off = jnp.arange(WINDOW + qb, dtype=jnp.int32) - WINDOW
    g_idx = jnp.arange(HKV, dtype=jnp.int32)[None, :, None]

    def chunk(args):
        qx, gx, bi, t0 = args
        tpos = t0 + jnp.arange(qb, dtype=jnp.int32)
        qg = qx.reshape(qb, HKV, GRP, DH)
        s = jnp.einsum('qgrd,cgd->qgrc', qg, kc[bi]) * scale
        p_c = masked_softmax(s, (cmp_end[None, :] <= tpos[:, None])[:, None, None, :])
        o_c = jnp.einsum('qgrc,cgd->qgrd', p_c, vc[bi])
        imp = jnp.einsum('qgrc,cs->qgs', p_c, cmp_to_sel)
        cur = (tpos // L_SEL)[:, None]
        valid = blk[None, :] <= cur
        forced = (blk[None, :] == 0) | (blk[None, :] == cur) | (blk[None, :] == cur - 1)
        score = jnp.where(valid[:, None, :], jnp.where(forced[:, None, :], FORCE_SCORE, imp), -jnp.inf)
        _, top = lax.top_k(score, n_top)
        ks = ks_all[bi][g_idx, top]
        vs = vs_all[bi][g_idx, top]
        kpos = (top[..., None] * L_SEL + sel_off).reshape(qb, HKV, 1, n_top * L_SEL)
        s = jnp.einsum('qgrd,qgnkd->qgrnk', qg, ks).reshape(qb, HKV, GRP, n_top * L_SEL) * scale
        p_s = masked_softmax(s, kpos <= tpos[:, None, None, None])
        o_s = jnp.einsum('qgrm,qgmd->qgrd', p_s, vs.reshape(qb, HKV, n_top * L_SEL, DH))
        wk = lax.dynamic_slice_in_dim(win_kv[bi], t0 - WINDOW - win_base, WINDOW + qb, axis=0)
        wpos = t0 + win_off
        wmask = (wpos[None, :] >= 0) & (wpos[None, :] <= tpos[:, None]) & (wpos[None, :] > tpos[:, None] - WINDOW)
        s = jnp.einsum('qgrd,kgd->qgrk', qg, wk[:, 0]) * scale
        p_w = masked_softmax(s, wmask[:, None, None, :])
        o_w = jnp.einsum('qgrk,kgd->qgrd', p_w, wk[:, 1])
        gx = gx.reshape(qb, HKV, GRP, 3)
        o = gx[..., 0:1] * o_c + gx[..., 1:2] * o_s + gx[..., 2:3] * o_w
        return o.reshape(qb, H_A, DH).astype(qx.dtype)

    bi = jnp.repeat(jnp.arange(B_, dtype=jnp.int32), nq)
    t0 = q_start + jnp.tile(jnp.arange(nq, dtype=jnp.int32) * qb, B_)
    out = lax.map(chunk, (q.reshape(B_ * nq, qb, H_A, DH), gates.reshape(B_ * nq, qb, H_A, 3), bi, t0))
    return out.reshape(B_, Tq, H_A * DH)


def layer_a(x, norm_w, w_in, gla_w2, gla_b, gla_gn, cmp_pe, cmp_w1, cmp_b1, cmp_w2, w_out,
            past_kv, win_buf, gla_s0, q_start):
    B_, T, _ = x.shape
    h = rmsnorm(x, norm_w)
    q, kv, g_br, z_a, q_b, k_b, v_b, lr_b, z_b = split_cols(h @ w_in, A_SIZES)
    q = q.reshape(B_, T, H_A, DH)
    kv = kv.reshape(B_, T, 6, HKV, DH)
    gates = jax.nn.sigmoid(g_br.astype(jnp.float32)).reshape(B_, T, H_A, 3)
    new_rows = kv[:, :, :N_KV_SLOTS]
    new_win = kv[:, :, N_KV_SLOTS:]
    if past_kv is None:
        full = new_rows
        keep = min(WINDOW, T)
        win_state = new_win[:, T - keep:]
        win_kv = jnp.pad(new_win, ((0, 0), (WINDOW, 0), (0, 0), (0, 0), (0, 0)))
        win_base = -WINDOW
    else:
        full = jnp.concatenate([past_kv, new_rows], axis=1)
        buf = jnp.concatenate([win_buf, new_win], axis=1)
        wb = win_buf.shape[1]
        win_state = buf[:, buf.shape[1] - wb:]
        win_kv = jnp.pad(buf, ((0, 0), (WINDOW - wb, 0), (0, 0), (0, 0), (0, 0)))
        win_base = q_start - WINDOW
    o_a = nsa_attend(q, gates, full, win_kv, q_start, win_base, cmp_pe, cmp_w1, cmp_b1, cmp_w2) * jax.nn.silu(z_a)
    log_a = jax.nn.log_sigmoid((lr_b @ gla_w2 + gla_b).astype(jnp.float32)) / GLA_GATE_NORM
    o_b, s_b = gated_recurrence(q_b.reshape(B_, T, H_B, DK_B) * (DK_B ** -0.5),
                                k_b.reshape(B_, T, H_B, DK_B),
                                v_b.reshape(B_, T, H_B, DV_B),
                                log_a.reshape(B_, T, H_B, DK_B), gla_s0)
    o_b = rmsnorm(o_b, gla_gn).reshape(B_, T, H_B * DV_B) * jax.nn.silu(z_b)
    y = x + jnp.concatenate([o_a, o_b], axis=-1) @ w_out
    return y, new_rows, win_state, s_b


def layer_c(x, layer_idx, norm_w, w_in, lb_logits, gn, w_out, s0):
    B_, T, _ = x.shape
    h = rmsnorm(x, norm_w)
    q, f, i, z = split_cols(h @ w_in, C_SIZES)
    prob = jax.nn.softmax(lb_logits.astype(jnp.float32), axis=0)
    lb = jnp.cumsum(prob, axis=0)[layer_idx] - prob[0]
    log_f = jnp.logaddexp(jnp.log(lb), jnp.log1p(-lb) + jax.nn.log_sigmoid(f.astype(jnp.float32)))
    k = -jnp.expm1(log_f)
    o, s = gated_recurrence(jax.nn.silu(q).reshape(B_, T, H_C, DK_C), k.reshape(B_, T, H_C, DK_C),
                            i.reshape(B_, T, H_C, DV_C), log_f.reshape(B_, T, H_C, DK_C), s0)
    o = rmsnorm(o, gn).reshape(B_, T, H_C * DV_C) * jax.nn.silu(z)
    return x + o @ w_out, s


def setup_inputs(seed: int = 0) -> dict:
    key = jax.random.key(seed)
    ks = jax.random.split(key, 24)
    n_pages = PAST_LEN // PAGE_SIZE
    n_used = DEC_BATCH * n_pages
    n_pool = n_used + n_used // 4
    wb = min(WINDOW, PAST_LEN)

    def nrm(k, shape, s):
        return s * jax.random.normal(k, shape, jnp.float32)

    a_out_in = H_A * DH + H_B * DV_B
    return {
        'x_prompt': nrm(ks[0], (BATCH, SEQ, D_MODEL), 1.0),
        'x_sample': nrm(ks[1], (DEC_BATCH, DEC_SEQ, D_MODEL), 1.0),
        'cache_kv': nrm(ks[2], (N_A_LAYERS, n_pool, PAGE_SIZE, N_KV_SLOTS, HKV, DH), 1.0),
        'cache_win': nrm(ks[3], (N_A_LAYERS, DEC_BATCH, wb, 2, HKV, DH), 1.0),
        'state_gla': nrm(ks[4], (N_A_LAYERS, DEC_BATCH, H_B, DK_B, DV_B), 1.0),
        'state_hgrn': nrm(ks[5], (N_C_LAYERS, DEC_BATCH, H_C, DK_C, DV_C), 0.5),
        'page_table': jax.random.permutation(ks[6], n_pool)[:n_used].reshape(DEC_BATCH, n_pages).astype(jnp.int32),
        'a_norm': 1.0 + nrm(ks[7], (N_A_LAYERS, D_MODEL), 0.02),
        'a_w_in': nrm(ks[8], (N_A_LAYERS, D_MODEL, sum(A_SIZES)), D_MODEL ** -0.5),
        'a_gla_w2': nrm(ks[9], (N_A_LAYERS, GLA_LR, H_B * DK_B), GLA_LR ** -0.5),
        'a_gla_b': nrm(ks[10], (N_A_LAYERS, H_B * DK_B), 0.1),
        'a_gla_gn': 1.0 + nrm(ks[11], (N_A_LAYERS, DV_B), 0.02),
        'a_cmp_pe': nrm(ks[12], (N_A_LAYERS, 2, L_CMP, DH), 0.1),
        'a_cmp_w1': nrm(ks[13], (N_A_LAYERS, 2, L_CMP * DH, CMP_HID), (L_CMP * DH) ** -0.5),
        'a_cmp_b1': nrm(ks[14], (N_A_LAYERS, 2, CMP_HID), 0.02),
        'a_cmp_w2': nrm(ks[15], (N_A_LAYERS, 2, CMP_HID, DH), CMP_HID ** -0.5),
        'a_w_out': nrm(ks[16], (N_A_LAYERS, a_out_in, D_MODEL), a_out_in ** -0.5),
        'c_norm': 1.0 + nrm(ks[17], (N_C_LAYERS, D_MODEL), 0.02),
        'c_w_in': nrm(ks[18], (N_C_LAYERS, D_MODEL, sum(C_SIZES)), D_MODEL ** -0.5),
        'c_lb_logits': nrm(ks[19], (DEPTH, H_C * DK_C), 0.5),
        'c_gn': 1.0 + nrm(ks[20], (N_C_LAYERS, DV_C), 0.02),
        'c_w_out': nrm(ks[21], (N_C_LAYERS, H_C * DV_C, D_MODEL), (H_C * DV_C) ** -0.5),
        'final_norm': 1.0 + nrm(ks[22], (D_MODEL,), 0.02),
    }


def reference(x_prompt, x_sample, cache_kv, cache_win, state_gla, state_hgrn, page_table,
              a_norm, a_w_in, a_gla_w2, a_gla_b, a_gla_gn, a_cmp_pe, a_cmp_w1, a_cmp_b1, a_cmp_w2, a_w_out,
              c_norm, c_w_in, c_lb_logits, c_gn, c_w_out, final_norm):
    n_seq, n_pages = page_table.shape
    past_len = n_pages * cache_kv.shape[2]

    def run(x, sample):
        rows_l, win_l, gla_l, hg_l = [], [], [], []
        for l in range(DEPTH):
            if l % 2 == 0:
                a = l // 2
                if sample:
                    past = cache_kv[a][page_table].reshape(n_seq, past_len, N_KV_SLOTS, HKV, DH)
                    buf, s0, q_start = cache_win[a], state_gla[a], past_len
                else:
                    past, buf, q_start = None, None, 0
                    s0 = jnp.zeros((x.shape[0], H_B, DK_B, DV_B), x.dtype)
                x, rows, win, sg = layer_a(x, a_norm[a], a_w_in[a], a_gla_w2[a], a_gla_b[a], a_gla_gn[a],
                                           a_cmp_pe[a], a_cmp_w1[a], a_cmp_b1[a], a_cmp_w2[a], a_w_out[a],
                                           past, buf, s0, q_start)
                rows_l.append(rows)
                win_l.append(win)
                gla_l.append(sg)
            else:
                c = l // 2
                s0 = state_hgrn[c] if sample else jnp.zeros((x.shape[0], H_C, DK_C, DV_C), x.dtype)
                x, sh = layer_c(x, l, c_norm[c], c_w_in[c], c_lb_logits, c_gn[c], c_w_out[c], s0)
                hg_l.append(sh)
        return rmsnorm(x, final_norm), jnp.stack(rows_l), jnp.stack(win_l), jnp.stack(gla_l), jnp.stack(hg_l)

    y_prompt, kv_p, win_p, gla_p, hg_p = run(x_prompt, False)
    y_sample, kv_s, win_s, gla_s, hg_s = run(x_sample, True)
    return (y_prompt, y_sample, kv_p, kv_s, win_p, win_s, gla_p, gla_s, hg_p, hg_s)
```

```python
import functools

import numpy as np
import jax
import jax.numpy as jnp
from jax import lax
from jax.experimental import pallas as pl
from jax.experimental.pallas import tpu as pltpu

F32 = jnp.float32
BF16 = jnp.bfloat16

D_MODEL = 2048
DH = 128
H_A = 8
HKV = 2
GRP = H_A // HKV
L_CMP = 32
D_CMP = 16
CMP_R = L_CMP // D_CMP
CMP_HID = 2 * DH
L_SEL = 64
N_TOP = 16
WINDOW = 512
N_KV_SLOTS = 4
H_B = 4
DK_B = 128
DV_B = 256
GLA_LR = 16
GLA_GATE_NORM = 16.0
DK_C = 128
H_C = 16
DV_C = 128
EPS = 1e-6
FORCE_SCORE = 1e9
A_SIZES = (H_A * DH, 6 * HKV * DH, 3 * H_A, H_A * DH, H_B * DK_B, H_B * DK_B, H_B * DV_B, GLA_LR, H_B * DV_B)

LANES = 128
SUBLANES = 8
VMEM_LIMIT = 56 * 1024 * 1024

A_Q = 0
A_VB = 1024
A_ZB = 2048
A_ZA = 3072
A_QB = 4096
A_KB = 4608
A_KV = 5120
A_MISC = 6656
NA = 7168
MISC_LR = GRP * 3

NCP = 128
NSP = 128
TP = 8
NEG = -1e30


def _dot(a, b):
    return jnp.dot(a, b, preferred_element_type=F32)


def _dot_nt(a, b):
    return lax.dot_general(a, b, (((1,), (1,)), ((), ())), preferred_element_type=F32)


def _split3(a):
    hi = a.astype(BF16)
    r = a - hi.astype(F32)
    mid = r.astype(BF16)
    lo = (r - mid.astype(F32)).astype(BF16)
    return hi, mid, lo


def _dot_exact_rhs(a, b_exact):
    hi, mid, lo = _split3(a)
    return _dot(hi, b_exact) + _dot(mid, b_exact) + _dot(lo, b_exact)


def _dot_exact_lhs(a_exact, b):
    hi, mid, lo = _split3(b)
    return _dot(a_exact, hi) + _dot(a_exact, mid) + _dot(a_exact, lo)


def _sigmoid(x):
    return 1.0 / (1.0 + jnp.exp(-x))


def _silu(x):
    return x * _sigmoid(x)


def _log_sigmoid(x):
    return jnp.minimum(x, 0.0) - jnp.log(1.0 + jnp.exp(-jnp.abs(x)))


def _gelu_tanh(x):
    return 0.5 * x * (1.0 + jnp.tanh(np.sqrt(2.0 / np.pi) * (x + 0.044715 * (x * x * x))))


def _masked_softmax_pre(sm):
    m = jnp.max(sm, axis=-1, keepdims=True)
    e = jnp.where(sm > 0.5 * NEG, jnp.exp(sm - m), 0.0)
    l = jnp.sum(e, axis=-1, keepdims=True)
    return e / jnp.maximum(l, 1e-30)


def _params(sem):
    return pltpu.CompilerParams(dimension_semantics=sem, vmem_limit_bytes=VMEM_LIMIT)


def _norm_matmul_body(x_ref, nw_ref, w_ref, o_ref, h_ref):
    @pl.when(pl.program_id(1) == 0)
    def _():
        x = x_ref[...]
        ms = jnp.mean(x * x, axis=-1, keepdims=True)
        h_ref[...] = (x * lax.rsqrt(ms + EPS) * nw_ref[...]).astype(BF16)

    o_ref[...] = _dot(h_ref[...], w_ref[...])


def _norm_matmul(x, nw, w, tm, tn):
    M, D = x.shape
    N = w.shape[1]
    assert M % tm == 0 and N % tn == 0
    return pl.pallas_call(
        _norm_matmul_body,
        grid=(M // tm, N // tn),
        in_specs=[pl.BlockSpec((tm, D), lambda i, j: (i, 0)),
                  pl.BlockSpec((1, D), lambda i, j: (0, 0)),
                  pl.BlockSpec((D, tn), lambda i, j: (0, j))],
        out_specs=pl.BlockSpec((tm, tn), lambda i, j: (i, j)),
        out_shape=jax.ShapeDtypeStruct((M, N), F32),
        scratch_shapes=[pltpu.VMEM((tm, D), BF16)],
        compiler_params=_params(("arbitrary", "arbitrary")),
    )(x, nw, w)


def _out_proj_body(a1_ref, a2_ref, w1_ref, w2_ref, x_ref, nw_ref, y_ref, *, final_norm):
    y = x_ref[...] + _dot(a1_ref[...].astype(BF16), w1_ref[...]) + _dot(a2_ref[...].astype(BF16), w2_ref[...])
    if final_norm:
        ms = jnp.mean(y * y, axis=-1, keepdims=True)
        y = y * lax.rsqrt(ms + EPS) * nw_ref[...]
    y_ref[...] = y


def _out_proj(a1, c1, a2, c2, w, x, nw, tm, final_norm):
    M, D = x.shape
    KH = w.shape[0] // 2
    assert M % tm == 0
    return pl.pallas_call(
        functools.partial(_out_proj_body, final_norm=final_norm),
        grid=(M // tm,),
        in_specs=[pl.BlockSpec((tm, KH), lambda i: (i, c1)),
                  pl.BlockSpec((tm, KH), lambda i: (i, c2)),
                  pl.BlockSpec((KH, D), lambda i: (0, 0)),
                  pl.BlockSpec((KH, D), lambda i: (1, 0)),
                  pl.BlockSpec((tm, D), lambda i: (i, 0)),
                  pl.BlockSpec((1, D), lambda i: (0, 0))],
        out_specs=pl.BlockSpec((tm, D), lambda i: (i, 0)),
        out_shape=jax.ShapeDtypeStruct((M, D), F32),
        compiler_params=_params(("arbitrary",)),
    )(a1, a2, w, w, x, nw)


def _cmp_bias_body(pe_ref, w1_ref, b1_ref, o_ref):
    pe = jnp.broadcast_to(pe_ref[...], (SUBLANES, pe_ref.shape[-1])).astype(BF16)
    o_ref[...] = b1_ref[...] + _dot(pe, w1_ref[...])[0:1, :]


def _cmp_bias(pe, w1, b1):
    S, KF, HID = w1.shape
    return pl.pallas_call(
        _cmp_bias_body,
        grid=(S,),
        in_specs=[pl.BlockSpec((None, 1, KF), lambda s: (s, 0, 0)),
                  pl.BlockSpec((None, KF, HID), lambda s: (s, 0, 0)),
                  pl.BlockSpec((None, 1, HID), lambda s: (s, 0, 0))],
        out_specs=pl.BlockSpec((None, 1, HID), lambda s: (s, 0, 0)),
        out_shape=jax.ShapeDtypeStruct((S, 1, HID), F32),
        compiler_params=_params(("arbitrary",)),
    )(pe, w1, b1)


def _compress_hidden(u0, u1, bias):
    return u0 + pltpu.roll(u1, NCP - 1, 0) + bias


def _compress_prompt_body(x_ref, w1_ref, cb_ref, w2_ref, o_ref):
    u0 = jnp.zeros((NCP, CMP_HID), F32)
    u1 = jnp.zeros((NCP, CMP_HID), F32)
    for pp in range(D_CMP // 2):
        xa = x_ref[pl.ds(2 * pp, NCP, stride=D_CMP), :]
        xb = x_ref[pl.ds(2 * pp + 1, NCP, stride=D_CMP), :]
        xp = jnp.concatenate([xa, xb], axis=1).astype(BF16)
        u0 = u0 + _dot(xp, w1_ref[pl.ds(pp * 2 * DH, 2 * DH), :])
        u1 = u1 + _dot(xp, w1_ref[pl.ds(D_CMP * DH + pp * 2 * DH, 2 * DH), :])
    h = _compress_hidden(u0, u1, cb_ref[...])
    o_ref[...] = _dot(_gelu_tanh(h).astype(BF16), w2_ref[...])


def _compress_prompt(P, w1, cb, w2, B, T):
    assert T // D_CMP == NCP
    kv0 = A_KV // DH
    return pl.pallas_call(
        _compress_prompt_body,
        grid=(B, 2, HKV),
        in_specs=[pl.BlockSpec((T, DH), lambda b, s, g: (b, kv0 + HKV * s + g)),
                  pl.BlockSpec((None, L_CMP * DH, CMP_HID), lambda b, s, g: (s, 0, 0)),
                  pl.BlockSpec((None, 1, CMP_HID), lambda b, s, g: (s, 0, 0)),
                  pl.BlockSpec((None, CMP_HID, DH), lambda b, s, g: (s, 0, 0))],
        out_specs=pl.BlockSpec((None, None, None, NCP, DH), lambda b, s, g: (b, s, g, 0, 0)),
        out_shape=jax.ShapeDtypeStruct((B, 2, HKV, NCP, DH), F32),
        compiler_params=_params(("arbitrary", "arbitrary", "arbitrary")),
    )(P, w1, cb, w2)


def _select_blocks(imp, tpos_tok, ns):
    blk = lax.broadcasted_iota(jnp.int32, (1, NSP), 1)
    cur = tpos_tok // L_SEL
    valid = blk <= cur
    forced = (blk == 0) | (blk == cur) | (blk == cur - 1)
    score = jnp.where(valid, jnp.where(forced, FORCE_SCORE, imp), -jnp.inf)
    rank = jnp.zeros(score.shape, F32)
    for j in range(ns):
        sj = score[:, j:j + 1]
        beats = (sj > score) | ((sj == score) & (blk > j))
        rank = rank + jnp.where(beats, 1.0, 0.0)
    return jnp.where(rank < float(min(N_TOP, ns)), 1.0, 0.0)


def _block_onehot(kpos, width):
    rows = lax.broadcasted_iota(jnp.int32, (NSP, width), 0)
    return jnp.where(rows == kpos // L_SEL, 1.0, 0.0).astype(BF16)


def _compressed_branch(qb, kc, vc, tpos, nc, scale):
    s = _dot_nt(qb, kc.astype(BF16)) * scale
    cidx = lax.broadcasted_iota(jnp.int32, (1, NCP), 1)
    cmask = (cidx < nc) & (cidx * D_CMP + (L_CMP - 1) <= tpos)
    p = _masked_softmax_pre(jnp.where(cmask, s, NEG))
    return p, _dot(p.astype(BF16), vc.astype(BF16))


def _flash(qb, k_ref, v_ref, lo, hi, tk, mask_fn, scale, acc_ref, m_ref, l_ref):
    acc_ref[...] = jnp.zeros(acc_ref.shape, F32)
    m_ref[...] = jnp.full(m_ref.shape, NEG, F32)
    l_ref[...] = jnp.zeros(l_ref.shape, F32)

    def body(kt, carry):
        k0 = pl.multiple_of(kt * tk, tk)
        kk = k_ref[pl.ds(k0, tk), :].astype(BF16)
        vv = v_ref[pl.ds(k0, tk), :].astype(BF16)
        s = _dot_nt(qb, kk) * scale
        kpos = k0 + lax.broadcasted_iota(jnp.int32, (1, tk), 1)
        mask = mask_fn(kpos)
        sm = jnp.where(mask, s, NEG)
        m_old = m_ref[...]
        m_new = jnp.maximum(m_old, jnp.max(sm, axis=-1, keepdims=True))
        alpha = jnp.exp(m_old - m_new)
        e = jnp.where(mask, jnp.exp(sm - m_new), 0.0)
        l_ref[...] = alpha * l_ref[...] + jnp.sum(e, axis=-1, keepdims=True)
        acc_ref[...] = alpha * acc_ref[...] + _dot(e.astype(BF16), vv)
        m_ref[...] = m_new
        return carry

    lax.fori_loop(lo, hi, body, 0)
    return acc_ref[...] / jnp.maximum(l_ref[...], 1e-30)


def _nsa_prompt_body(q_ref, kc_ref, vc_ref, ks_ref, vs_ref, kw_ref, vw_ref, misc_ref, z_ref, c2s_ref,
                     o_ref, acc_ref, m_ref, l_ref, *, tq, tk, T):
    qi = pl.program_id(2)
    t0 = qi * tq
    R = GRP * tq
    scale = DH ** -0.5
    ns = -(-T // L_SEL)
    nc = T // D_CMP - CMP_R + 1
    qb = jnp.concatenate([q_ref[:, r * DH:(r + 1) * DH] for r in range(GRP)], axis=0).astype(BF16)
    tpos = t0 + (lax.broadcasted_iota(jnp.int32, (R, 1), 0) & (tq - 1))
    tpos_tok = t0 + lax.broadcasted_iota(jnp.int32, (tq, 1), 0)

    p_c, o_c = _compressed_branch(qb, kc_ref[...], vc_ref[...], tpos, nc, scale)
    psum = p_c[0:tq]
    for r in range(1, GRP):
        psum = psum + p_c[r * tq:(r + 1) * tq]
    imp = _dot_exact_rhs(psum, c2s_ref[...])
    sel = _select_blocks(imp, tpos_tok, ns)
    sel4 = jnp.concatenate([sel] * GRP, axis=0).astype(BF16)

    def sel_mask(kpos):
        chosen = _dot(sel4, _block_onehot(kpos, tk))
        return (chosen > 0.5) & (kpos <= tpos)

    o_s = _flash(qb, ks_ref, vs_ref, 0, lax.div(t0, tk) + 1, tk, sel_mask, scale, acc_ref, m_ref, l_ref)

    def win_mask(kpos):
        return (kpos <= tpos) & (kpos > tpos - WINDOW)

    o_w = _flash(qb, kw_ref, vw_ref, jnp.maximum(qi - WINDOW // tq, 0), qi + 1, tq, win_mask, scale,
                 acc_ref, m_ref, l_ref)

    gm = _sigmoid(misc_ref[:, 0:GRP * 3])

    def gate(j):
        return jnp.concatenate([gm[:, r * 3 + j:r * 3 + j + 1] for r in range(GRP)], axis=0)

    o = gate(0) * o_c + gate(1) * o_s + gate(2) * o_w
    for r in range(GRP):
        zr = z_ref[:, r * DH:(r + 1) * DH]
        o_ref[:, r * DH:(r + 1) * DH] = (o[r * tq:(r + 1) * tq] * _silu(zr)).astype(BF16)


def _nsa_prompt(P, kcv, c2s, B, T, tq, tk):
    assert T % tq == 0 and tq & (tq - 1) == 0 and tk % tq == 0 and T % tk == 0 and WINDOW % tq == 0
    nq = T // tq
    kv0 = A_KV // DH
    gw = GRP * DH
    R = GRP * tq

    def kvspec(slot):
        return pl.BlockSpec((T, DH), lambda b, g, i: (b, kv0 + HKV * slot + g))

    return pl.pallas_call(
        functools.partial(_nsa_prompt_body, tq=tq, tk=tk, T=T),
        grid=(B, HKV, nq),
        in_specs=[pl.BlockSpec((tq, gw), lambda b, g, i: (b * nq + i, A_Q // gw + g)),
                  pl.BlockSpec((None, None, None, NCP, DH), lambda b, g, i: (b, 0, g, 0, 0)),
                  pl.BlockSpec((None, None, None, NCP, DH), lambda b, g, i: (b, 1, g, 0, 0)),
                  kvspec(2), kvspec(3), kvspec(4), kvspec(5),
                  pl.BlockSpec((tq, LANES), lambda b, g, i: (b * nq + i, A_MISC // LANES + g)),
                  pl.BlockSpec((tq, gw), lambda b, g, i: (b * nq + i, A_ZA // gw + g)),
                  pl.BlockSpec((NCP, NSP), lambda b, g, i: (0, 0))],
        out_specs=pl.BlockSpec((tq, gw), lambda b, g, i: (b * nq + i, g)),
        out_shape=jax.ShapeDtypeStruct((B * T, H_A * DH), BF16),
        scratch_shapes=[pltpu.VMEM((R, DH), F32), pltpu.VMEM((R, 1), F32), pltpu.VMEM((R, 1), F32)],
        compiler_params=_params(("arbitrary", "arbitrary", "arbitrary")),
    )(P, kcv, kcv, P, P, P, P, P, P, c2s)


def _pad_rows(x, rows):
    return jnp.concatenate([x, jnp.zeros((rows - x.shape[0], x.shape[1]), x.dtype)], axis=0)


def _nsa_sample_body(pt_ref, *refs, n_pages, page, past_len, dec_seq, wb):
    pages = refs[:n_pages]
    (win_ref, q_ref, kvr_ref, kvw_ref, misc_ref, z_ref, w1_ref, cb_ref, w2_ref, c2s_ref, o_ref) = refs[n_pages:]
    del pt_ref
    scale = DH ** -0.5
    L = past_len + dec_seq
    ns = -(-L // L_SEL)
    nc = L // D_CMP - CMP_R + 1
    R = GRP * TP
    per_page = page // D_CMP
    nch = N_KV_SLOTS * HKV

    def page_rows(pg, ch):
        return pg[pl.ds(ch, page, stride=nch), :]

    kcv = {}
    for s in range(2):
        u0 = jnp.zeros((HKV * NCP, CMP_HID), F32)
        u1 = jnp.zeros((HKV * NCP, CMP_HID), F32)
        for pp in range(D_CMP // 2):
            xs = []
            for g in range(HKV):
                ch = s * HKV + g
                xa = jnp.concatenate([pg[pl.ds(2 * pp * nch + ch, per_page, stride=D_CMP * nch), :] for pg in pages],
                                     axis=0)
                xb = jnp.concatenate([pg[pl.ds((2 * pp + 1) * nch + ch, per_page, stride=D_CMP * nch), :]
                                      for pg in pages], axis=0)
                xs.append(jnp.concatenate([xa, xb], axis=1))
            xp = jnp.concatenate(xs, axis=0).astype(BF16)
            u0 = u0 + _dot(xp, w1_ref[s, pl.ds(pp * 2 * DH, 2 * DH), :])
            u1 = u1 + _dot(xp, w1_ref[s, pl.ds(D_CMP * DH + pp * 2 * DH, 2 * DH), :])
        for g in range(HKV):
            h = _compress_hidden(u0[g * NCP:(g + 1) * NCP], u1[g * NCP:(g + 1) * NCP], cb_ref[s])
            kcv[s, g] = _dot(_gelu_tanh(h).astype(BF16), w2_ref[s])

    trow = lax.broadcasted_iota(jnp.int32, (R, 1), 0) & (TP - 1)
    tpos = past_len + trow
    tpos_tok = past_len + lax.broadcasted_iota(jnp.int32, (TP, 1), 0)
    lane = lax.broadcasted_iota(jnp.int32, (1, LANES), 1)
    new_pos = past_len + lane

    for g in range(HKV):
        qb = jnp.concatenate([q_ref[:, (g * GRP + r) * DH:(g * GRP + r + 1) * DH] for r in range(GRP)],
                             axis=0).astype(BF16)
        p_c, o_c = _compressed_branch(qb, kcv[0, g], kcv[1, g], tpos, nc, scale)
        psum = p_c[0:TP]
        for r in range(1, GRP):
            psum = psum + p_c[r * TP:(r + 1) * TP]
        imp = _dot_exact_rhs(psum, c2s_ref[...])
        sel = _select_blocks(imp, tpos_tok, ns)
        sel4 = jnp.concatenate([sel] * GRP, axis=0).astype(BF16)

        def sel_scores(kk, kpos):
            s = _dot_nt(qb, kk.astype(BF16)) * scale
            chosen = _dot(sel4, _block_onehot(kpos, LANES))
            return jnp.where((chosen > 0.5) & (kpos <= tpos) & (kpos < L), s, NEG)

        ksel_col = (2 * HKV + g) * DH
        vsel_col = (3 * HKV + g) * DH
        pieces = [sel_scores(page_rows(pages[j], 2 * HKV + g), j * page + lane) for j in range(n_pages)]
        pieces.append(sel_scores(_pad_rows(kvr_ref[:, ksel_col:ksel_col + DH], LANES), new_pos))
        p = _masked_softmax_pre(jnp.concatenate(pieces, axis=1))
        o_s = _dot(p[:, n_pages * page:].astype(BF16),
                   _pad_rows(kvr_ref[:, vsel_col:vsel_col + DH], LANES).astype(BF16))
        for j in range(n_pages):
            o_s = o_s + _dot(p[:, j * page:(j + 1) * page].astype(BF16),
                             page_rows(pages[j], 3 * HKV + g).astype(BF16))

        def win_scores(kk, kpos):
            s = _dot_nt(qb, kk.astype(BF16)) * scale
            return jnp.where((kpos <= tpos) & (kpos > tpos - WINDOW) & (kpos < L), s, NEG)

        wpos = (past_len - wb) + lax.broadcasted_iota(jnp.int32, (1, wb), 1)
        kw_col = g * DH
        vw_col = (HKV + g) * DH
        pw = _masked_softmax_pre(jnp.concatenate(
            [win_scores(win_ref[pl.ds(g, wb, stride=2 * HKV), :], wpos),
             win_scores(_pad_rows(kvw_ref[:, kw_col:kw_col + DH], LANES), new_pos)], axis=1))
        o_w = (_dot(pw[:, :wb].astype(BF16), win_ref[pl.ds(HKV + g, wb, stride=2 * HKV), :].astype(BF16))
               + _dot(pw[:, wb:].astype(BF16), _pad_rows(kvw_ref[:, vw_col:vw_col + DH], LANES).astype(BF16)))

        gm = _sigmoid(misc_ref[:, g * LANES:g * LANES + GRP * 3])

        def gate(j):
            return jnp.concatenate([gm[:, r * 3 + j:r * 3 + j + 1] for r in range(GRP)], axis=0)

        o = gate(0) * o_c + gate(1) * o_s + gate(2) * o_w
        for r in range(GRP):
            c = (g * GRP + r) * DH
            o_ref[:, c:c + DH] = o[r * TP:(r + 1) * TP] * _silu(z_ref[:, c:c + DH])


def _nsa_sample(Ps, cache, cache_win, page_table, w1, cb, w2, c2s, dec_seq):
    NB, n_pages = page_table.shape
    n_pool, prow, _ = cache.shape
    page = prow // (N_KV_SLOTS * HKV)
    wb = cache_win.shape[1] // (2 * HKV)
    past_len = n_pages * page
    L = past_len + dec_seq
    assert wb == WINDOW and (L // D_CMP) == NCP and (L // D_CMP) * D_CMP <= past_len and dec_seq <= TP
    assert -(-L // L_SEL) <= NSP and page % L_SEL == 0
    qw = H_A * DH

    def page_spec(j):
        return pl.BlockSpec((None, prow, DH), lambda b, pt: (pt[b * n_pages + j], 0, 0))

    grid_spec = pltpu.PrefetchScalarGridSpec(
        num_scalar_prefetch=1,
        grid=(NB,),
        in_specs=[page_spec(j) for j in range(n_pages)] + [
            pl.BlockSpec((None, wb * 2 * HKV, DH), lambda b, pt: (b, 0, 0)),
            pl.BlockSpec((TP, qw), lambda b, pt: (b, A_Q // qw)),
            pl.BlockSpec((TP, 4 * HKV * DH), lambda b, pt: (b, A_KV // (4 * HKV * DH))),
            pl.BlockSpec((TP, 2 * HKV * DH), lambda b, pt: (b, (A_KV + 4 * HKV * DH) // (2 * HKV * DH))),
            pl.BlockSpec((TP, HKV * LANES), lambda b, pt: (b, A_MISC // (HKV * LANES))),
            pl.BlockSpec((TP, qw), lambda b, pt: (b, A_ZA // qw)),
            pl.BlockSpec((2, L_CMP * DH, CMP_HID), lambda b, pt: (0, 0, 0)),
            pl.BlockSpec((2, 1, CMP_HID), lambda b, pt: (0, 0, 0)),
            pl.BlockSpec((2, CMP_HID, DH), lambda b, pt: (0, 0, 0)),
            pl.BlockSpec((NCP, NSP), lambda b, pt: (0, 0))],
        out_specs=pl.BlockSpec((TP, qw), lambda b, pt: (b, 0)),
    )
    return pl.pallas_call(
        functools.partial(_nsa_sample_body, n_pages=n_pages, page=page, past_len=past_len, dec_seq=dec_seq, wb=wb),
        grid_spec=grid_spec,
        out_shape=jax.ShapeDtypeStruct((NB * TP, qw), F32),
        compiler_params=_params(("arbitrary",)),
    )(page_table.reshape(-1), *([cache] * n_pages), cache_win, Ps, Ps, Ps, Ps, Ps, w1, cb, w2, c2s)


def _rec_chunk(q, k, v, la, S, tri, C, sb, last_row):
    K = q.shape[1]
    V = v.shape[1]
    b = _dot_exact_lhs(tri, la)
    o = _dot((q * jnp.exp(b)).astype(BF16), S.astype(BF16))
    rows = []
    for i in range(C // sb):
        r0 = i * sb
        ci = b[r0 + sb // 2:r0 + sb // 2 + 1, :]
        qi = (q[r0:r0 + sb] * jnp.exp(b[r0:r0 + sb] - ci)).astype(BF16)
        ki = (k * jnp.exp(jnp.minimum(ci - b, 80.0))).astype(BF16)
        rows.append(_dot_nt(qi, ki))
    att = rows[0] if len(rows) == 1 else jnp.concatenate(rows, axis=0)
    ri = lax.broadcasted_iota(jnp.int32, (C, C), 0)
    cj = lax.broadcasted_iota(jnp.int32, (C, C), 1)
    att = jnp.where(cj <= ri, att, 0.0)
    o = o + _dot(att.astype(BF16), v.astype(BF16))

    b_last = b[last_row:last_row + 1, :]
    rowi = lax.broadcasted_iota(jnp.int32, (C, 1), 0)
    kd = jnp.where(rowi <= last_row, k * jnp.exp(jnp.minimum(b_last - b, 0.0)), 0.0)
    stack = jnp.concatenate([kd, jnp.broadcast_to(b_last, (SUBLANES, K)),
                             jnp.zeros((LANES - C - SUBLANES, K), F32)], axis=0)
    stack_t = stack.T
    a_col = jnp.exp(stack_t[:, C:C + 1])
    v_pad = jnp.concatenate([v, jnp.zeros((LANES - C, V), F32)], axis=0)
    s_new = a_col * S + _dot(stack_t.astype(BF16), v_pad.astype(BF16))
    return o, s_new


def _rec_body(*refs, variant, nh, K, V, T, C, last_row, has_s0, layer_idx):
    refs = list(refs)
    q_ref, k_ref, v_ref, z_ref = refs[:4]
    pos = 4
    if variant == "gla":
        misc_ref, w2_ref, gb_ref = refs[pos:pos + 3]
        pos += 3
    else:
        lb_ref = refs[pos]
        pos += 1
    gn_ref = refs[pos]
    pos += 1
    if has_s0:
        s0_ref = refs[pos]
        pos += 1
    o_ref, s_ref = refs[pos:pos + 2]
    pos += 2
    n_chunks = T // C
    sb = min(16, C)
    ri = lax.broadcasted_iota(jnp.int32, (C, C), 0)
    cj = lax.broadcasted_iota(jnp.int32, (C, C), 1)
    tri = jnp.where(cj <= ri, 1.0, 0.0).astype(BF16)

    for h in range(nh):
        if variant == "hgrn":
            lg = lb_ref[:, h * K:(h + 1) * K]
            e = jnp.exp(lg - jnp.max(lg, axis=0, keepdims=True))
            prob = e / jnp.sum(e, axis=0, keepdims=True)
            lb = prob[1:2]
            for i in range(2, layer_idx + 1):
                lb = lb + prob[i:i + 1]

        def load(c0, h=h):
            rows = pl.ds(c0, C)
            qr = q_ref[rows, h * K:(h + 1) * K]
            kr = k_ref[rows, h * K:(h + 1) * K]
            vr = v_ref[rows, h * V:(h + 1) * V]
            if variant == "gla":
                zg = _dot(misc_ref[rows, :].astype(BF16), w2_ref[h].astype(BF16)) + gb_ref[h]
                return qr * (K ** -0.5), kr, vr, _log_sigmoid(zg) / GLA_GATE_NORM
            sig = _sigmoid(kr)
            return _silu(qr), (1.0 - lb) * _sigmoid(-kr), vr, jnp.log(lb + (1.0 - lb) * sig)

        def finish(c0, o, h=h):
            rows = pl.ds(c0, C)
            ms = jnp.mean(o * o, axis=-1, keepdims=True)
            on = o * lax.rsqrt(ms + EPS) * gn_ref[...]
            o_ref[rows, h * V:(h + 1) * V] = (on * _silu(z_ref[rows, h * V:(h + 1) * V])).astype(o_ref.dtype)

        if n_chunks == 1:
            S = s0_ref[h] if has_s0 else jnp.zeros((K, V), F32)
            q, k, v, la = load(0)
            o, S = _rec_chunk(q, k, v, la, S, tri, C, sb, last_row)
            finish(0, o)
            s_ref[h] = S
        else:
            s_ref[h] = s0_ref[h] if has_s0 else jnp.zeros((K, V), F32)

            def body(ci, carry, h=h):
                c0 = pl.multiple_of(ci * C, C)
                q, k, v, la = load(c0)
                o, S = _rec_chunk(q, k, v, la, s_ref[h], tri, C, sb, last_row)
                finish(c0, o)
                s_ref[h] = S
                return carry

            lax.fori_loop(0, n_chunks, body, 0)


def _recurrence(P, cols, extra, gn, s0, *, variant, B, T, H, K, V, C, nh, last_row, out_dtype, layer_idx=1):
    cq, ck, cv, cz = cols
    hb = H // nh
    assert H % nh == 0 and T % C == 0

    def colspec(c0, w):
        assert c0 % (nh * w) == 0
        return pl.BlockSpec((T, nh * w), lambda b, j: (b, c0 // (nh * w) + j))

    in_specs = [colspec(cq, K), colspec(ck, K), colspec(cv, V), colspec(cz, V)]
    args = [P, P, P, P]
    if variant == "gla":
        w2p, gb = extra
        in_specs += [pl.BlockSpec((T, LANES), lambda b, j: (b, A_MISC // LANES)),
                     pl.BlockSpec((nh, LANES, K), lambda b, j: (j, 0, 0)),
                     pl.BlockSpec((nh, 1, K), lambda b, j: (j, 0, 0))]
        args += [P, w2p, gb]
    else:
        (lb_logits,) = extra
        in_specs += [pl.BlockSpec((lb_logits.shape[0], nh * K), lambda b, j: (0, j))]
        args += [lb_logits]
    in_specs += [pl.BlockSpec((1, V), lambda b, j: (0, 0))]
    args += [gn]
    st_spec = pl.BlockSpec((None, None, nh, K, V), lambda b, j: (0, b, j, 0, 0))
    if s0 is not None:
        in_specs += [st_spec]
        args += [s0]
    return pl.pallas_call(
        functools.partial(_rec_body, variant=variant, nh=nh, K=K, V=V, T=T, C=C, last_row=last_row,
                          has_s0=s0 is not None, layer_idx=layer_idx),
        grid=(B, hb),
        in_specs=in_specs,
        out_specs=[pl.BlockSpec((T, nh * V), lambda b, j: (b, j)), st_spec],
        out_shape=[jax.ShapeDtypeStruct((B * T, H * V), out_dtype),
                   jax.ShapeDtypeStruct((1, B, H, K, V), F32)],
        compiler_params=_params(("arbitrary", "arbitrary")),
    )(*args)


def _layout_a_w_in(w):
    off = np.concatenate([[0], np.cumsum(A_SIZES)])
    q, kv, gbr, za, qb, kb, vb, lr, zb = [w[:, off[i]:off[i + 1]] for i in range(len(A_SIZES))]
    D = w.shape[0]
    z = lambda n: jnp.zeros((D, n), w.dtype)
    misc0 = jnp.concatenate([gbr[:, :GRP * 3], lr, z(LANES - GRP * 3 - GLA_LR)], axis=1)
    misc1 = jnp.concatenate([gbr[:, GRP * 3:], z(LANES - GRP * 3)], axis=1)
    out = jnp.concatenate([q, vb, zb, za, qb, kb, kv, misc0, misc1, z(NA - A_MISC - HKV * LANES)], axis=1)
    assert out.shape[1] == NA
    return out.astype(BF16)


def _cmp_to_sel():
    c_start = np.arange(NCP)[:, None] * D_CMP
    s_start = np.arange(NSP)[None, :] * L_SEL
    overlap = np.clip(np.minimum(c_start + L_CMP, s_start + L_SEL) - np.maximum(c_start, s_start), 0, None)
    return jnp.asarray(overlap / D_CMP, dtype=BF16)


def kernel(x_prompt, x_sample, cache_kv, cache_win, state_gla, state_hgrn, page_table, a_norm, a_w_in, a_gla_w2,
           a_gla_b, a_gla_gn, a_cmp_pe, a_cmp_w1, a_cmp_b1, a_cmp_w2, a_w_out, c_norm, c_w_in, c_lb_logits, c_gn,
           c_w_out, final_norm):
    B, T, D = x_prompt.shape
    NB, TS, _ = x_sample.shape
    n_pool, page = cache_kv.shape[1], cache_kv.shape[2]
    wb = cache_win.shape[2]
    assert a_norm.shape[0] == 1 and c_norm.shape[0] == 1 and c_lb_logits.shape[0] == 2

    wa = _layout_a_w_in(a_w_in[0])
    wc = c_w_in[0].astype(BF16)
    wao = a_w_out[0].astype(BF16)
    wco = c_w_out[0].astype(BF16)
    w1 = a_cmp_w1[0].astype(BF16)
    w2 = a_cmp_w2[0].astype(BF16)
    pe = a_cmp_pe[0].reshape(2, 1, L_CMP * DH)
    b1 = a_cmp_b1[0].reshape(2, 1, CMP_HID)
    w2p = jnp.zeros((H_B, LANES, DK_B), F32).at[:, MISC_LR:MISC_LR + GLA_LR, :].set(
        a_gla_w2[0].reshape(GLA_LR, H_B, DK_B).transpose(1, 0, 2))
    gb = a_gla_b[0].reshape(H_B, 1, DK_B)
    c2s = _cmp_to_sel()
    a_nw = a_norm[0].reshape(1, D)
    c_nw = c_norm[0].reshape(1, D)
    f_nw = final_norm.reshape(1, D)
    gla_gn = a_gla_gn[0].reshape(1, DV_B)
    hg_gn = c_gn[0].reshape(1, DV_C)

    xp = x_prompt.reshape(B * T, D)
    xs = jnp.pad(x_sample, ((0, 0), (0, TP - TS), (0, 0))).reshape(NB * TP, D)

    cb = _cmp_bias(pe, w1, b1)

    Pp = _norm_matmul(xp, a_nw, wa, tm=1024, tn=512)
    kcv = _compress_prompt(Pp, w1, cb, w2, B, T)
    oa_p = _nsa_prompt(Pp, kcv, c2s, B, T, tq=128, tk=512)
    ob_p, gla_p = _recurrence(Pp, (A_QB, A_KB, A_VB, A_ZB), (w2p, gb), gla_gn, None, variant="gla", B=B, T=T,
                              H=H_B, K=DK_B, V=DV_B, C=64, nh=1, last_row=63, out_dtype=BF16)
    y1p = _out_proj(oa_p, 0, ob_p, 0, wao, xp, a_nw, tm=512, final_norm=False)

    Ps = _norm_matmul(xs, a_nw, wa, tm=NB * TP, tn=512)
    cache2 = cache_kv[0].reshape(n_pool, page * N_KV_SLOTS * HKV, DH)
    win2 = cache_win[0].reshape(NB, wb * 2 * HKV, DH)
    oa_s = _nsa_sample(Ps, cache2, win2, page_table, w1, cb, w2, c2s, TS)
    ob_s, gla_s = _recurrence(Ps, (A_QB, A_KB, A_VB, A_ZB), (w2p, gb), gla_gn, state_gla, variant="gla", B=NB, T=TP,
                              H=H_B, K=DK_B, V=DV_B, C=TP, nh=H_B, last_row=TS - 1, out_dtype=F32)
    y1s = _out_proj(oa_s, 0, ob_s, 0, wao, xs, a_nw, tm=min(512, NB * TP), final_norm=False)

    Pc = _norm_matmul(y1p, c_nw, wc, tm=1024, tn=512)
    hc = H_C * DK_C
    oc_p, hg_p = _recurrence(Pc, (0, hc, 2 * hc, 3 * hc), (c_lb_logits,), hg_gn, None, variant="hgrn", B=B, T=T,
                             H=H_C, K=DK_C, V=DV_C, C=64, nh=1, last_row=63, out_dtype=BF16)
    y_prompt = _out_proj(oc_p, 0, oc_p, 1, wco, y1p, f_nw, tm=512, final_norm=True)

    Pcs = _norm_matmul(y1s, c_nw, wc, tm=NB * TP, tn=512)
    oc_s, hg_s = _recurrence(Pcs, (0, hc, 2 * hc, 3 * hc), (c_lb_logits,), hg_gn, state_hgrn, variant="hgrn", B=NB,
                             T=TP, H=H_C, K=DK_C, V=DV_C, C=TP, nh=H_C, last_row=TS - 1, out_dtype=F32)
    y_sample = _out_proj(oc_s, 0, oc_s, 1, wco, y1s, f_nw, tm=min(512, NB * TP), final_norm=True)

    kvw = 6 * HKV * DH
    nrw = N_KV_SLOTS * HKV * DH
    Pp3 = Pp.reshape(B, T, NA)
    Ps3 = Ps.reshape(NB, TP, NA)
    kv_p = Pp3[:, :, A_KV:A_KV + nrw].reshape(1, B, T, N_KV_SLOTS, HKV, DH)
    kv_s = Ps3[:, :TS, A_KV:A_KV + nrw].reshape(1, NB, TS, N_KV_SLOTS, HKV, DH)
    keep = min(WINDOW, T)
    win_p = Pp3[:, T - keep:, A_KV + nrw:A_KV + kvw].reshape(1, B, keep, 2, HKV, DH)
    buf = jnp.concatenate([cache_win[0], Ps3[:, :TS, A_KV + nrw:A_KV + kvw].reshape(NB, TS, 2, HKV, DH)], axis=1)
    win_s = buf[:, buf.shape[1] - wb:][None]
    return (y_prompt.reshape(B, T, D), y_sample.reshape(NB, TP, D)[:, :TS], kv_p, kv_s, win_p, win_s,
            gla_p, gla_s, hg_p, hg_s)
```

```python
import functools

import numpy as np
import jax
import jax.numpy as jnp
from jax import lax
from jax.experimental import pallas as pl
from jax.experimental.pallas import tpu as pltpu

F32 = jnp.float32
BF16 = jnp.bfloat16

D_MODEL = 2048
DH = 128
H_A = 8
HKV = 2
GRP = H_A // HKV
L_CMP = 32
D_CMP = 16
CMP_R = L_CMP // D_CMP
CMP_HID = 2 * DH
L_SEL = 64
N_TOP = 16
WINDOW = 512
N_KV_SLOTS = 4
H_B = 4
DK_B = 128
DV_B = 256
GLA_LR = 16
GLA_GATE_NORM = 16.0
DK_C = 128
H_C = 16
DV_C = 128
EPS = 1e-6
FORCE_SCORE = 1e9
A_SIZES = (H_A * DH, 6 * HKV * DH, 3 * H_A, H_A * DH, H_B * DK_B, H_B * DK_B, H_B * DV_B, GLA_LR, H_B * DV_B)

LANES = 128
SUBLANES = 8
VMEM_LIMIT = 56 * 1024 * 1024

A_Q = 0
A_VB = 1024
A_ZB = 2048
A_ZA = 3072
A_QB = 4096
A_KB = 4608
A_KV = 5120
A_MISC = 6656
NA = 7168
MISC_LR = GRP * 3

NCP = 128
NSP = 128
TP = 8
NEG = -1e30


def _dot(a, b):
    return jnp.dot(a, b, preferred_element_type=F32)


def _dot_nt(a, b):
    return lax.dot_general(a, b, (((1,), (1,)), ((), ())), preferred_element_type=F32)


def _split3(a):
    hi = a.astype(BF16)
    r = a - hi.astype(F32)
    mid = r.astype(BF16)
    lo = (r - mid.astype(F32)).astype(BF16)
    return hi, mid, lo


def _dot_exact_rhs(a, b_exact):
    hi, mid, lo = _split3(a)
    return _dot(hi, b_exact) + _dot(mid, b_exact) + _dot(lo, b_exact)


def _dot_exact_lhs(a_exact, b):
    hi, mid, lo = _split3(b)
    return _dot(a_exact, hi) + _dot(a_exact, mid) + _dot(a_exact, lo)


def _sigmoid(x):
    return 1.0 / (1.0 + jnp.exp(-x))


def _silu(x):
    return x * _sigmoid(x)


def _log_sigmoid(x):
    return jnp.minimum(x, 0.0) - jnp.log(1.0 + jnp.exp(-jnp.abs(x)))


def _gelu_tanh(x):
    return 0.5 * x * (1.0 + jnp.tanh(np.sqrt(2.0 / np.pi) * (x + 0.044715 * (x * x * x))))


def _masked_softmax_pre(sm):
    m = jnp.max(sm, axis=-1, keepdims=True)
    e = jnp.where(sm > 0.5 * NEG, jnp.exp(sm - m), 0.0)
    l = jnp.sum(e, axis=-1, keepdims=True)
    return e / jnp.maximum(l, 1e-30)


def _params(sem):
    return pltpu.CompilerParams(dimension_semantics=sem, vmem_limit_bytes=VMEM_LIMIT)


def _norm_matmul_body(x_ref, nw_ref, w_ref, o_ref, h_ref):
    @pl.when(pl.program_id(1) == 0)
    def _():
        x = x_ref[...]
        ms = jnp.mean(x * x, axis=-1, keepdims=True)
        h_ref[...] = (x * lax.rsqrt(ms + EPS) * nw_ref[...]).astype(BF16)

    o_ref[...] = _dot(h_ref[...], w_ref[...])


def _norm_matmul(x, nw, w, tm, tn):
    M, D = x.shape
    N = w.shape[1]
    assert M % tm == 0 and N % tn == 0
    return pl.pallas_call(
        _norm_matmul_body,
        grid=(M // tm, N // tn),
        in_specs=[pl.BlockSpec((tm, D), lambda i, j: (i, 0)),
                  pl.BlockSpec((1, D), lambda i, j: (0, 0)),
                  pl.BlockSpec((D, tn), lambda i, j: (0, j))],
        out_specs=pl.BlockSpec((tm, tn), lambda i, j: (i, j)),
        out_shape=jax.ShapeDtypeStruct((M, N), F32),
        scratch_shapes=[pltpu.VMEM((tm, D), BF16)],
        compiler_params=_params(("arbitrary", "arbitrary")),
    )(x, nw, w)


def _out_proj_body(a1_ref, a2_ref, w1_ref, w2_ref, x_ref, nw_ref, y_ref, *, final_norm):
    y = x_ref[...] + _dot(a1_ref[...].astype(BF16), w1_ref[...]) + _dot(a2_ref[...].astype(BF16), w2_ref[...])
    if final_norm:
        ms = jnp.mean(y * y, axis=-1, keepdims=True)
        y = y * lax.rsqrt(ms + EPS) * nw_ref[...]
    y_ref[...] = y


def _out_proj(a1, c1, a2, c2, w, x, nw, tm, final_norm):
    M, D = x.shape
    KH = w.shape[0] // 2
    assert M % tm == 0
    return pl.pallas_call(
        functools.partial(_out_proj_body, final_norm=final_norm),
        grid=(M // tm,),
        in_specs=[pl.BlockSpec((tm, KH), lambda i: (i, c1)),
                  pl.BlockSpec((tm, KH), lambda i: (i, c2)),
                  pl.BlockSpec((KH, D), lambda i: (0, 0)),
                  pl.BlockSpec((KH, D), lambda i: (1, 0)),
                  pl.BlockSpec((tm, D), lambda i: (i, 0)),
                  pl.BlockSpec((1, D), lambda i: (0, 0))],
        out_specs=pl.BlockSpec((tm, D), lambda i: (i, 0)),
        out_shape=jax.ShapeDtypeStruct((M, D), F32),
        compiler_params=_params(("arbitrary",)),
    )(a1, a2, w, w, x, nw)


def _cmp_bias_body(pe_ref, w1_ref, b1_ref, o_ref):
    pe = jnp.broadcast_to(pe_ref[...], (SUBLANES, pe_ref.shape[-1])).astype(BF16)
    o_ref[...] = b1_ref[...] + _dot(pe, w1_ref[...])[0:1, :]


def _cmp_bias(pe, w1, b1):
    S, KF, HID = w1.shape
    return pl.pallas_call(
        _cmp_bias_body,
        grid=(S,),
        in_specs=[pl.BlockSpec((None, 1, KF), lambda s: (s, 0, 0)),
                  pl.BlockSpec((None, KF, HID), lambda s: (s, 0, 0)),
                  pl.BlockSpec((None, 1, HID), lambda s: (s, 0, 0))],
        out_specs=pl.BlockSpec((None, 1, HID), lambda s: (s, 0, 0)),
        out_shape=jax.ShapeDtypeStruct((S, 1, HID), F32),
        compiler_params=_params(("arbitrary",)),
    )(pe, w1, b1)


def _compress_hidden(u0, u1, bias):
    return u0 + pltpu.roll(u1, NCP - 1, 0) + bias


def _compress_prompt_body(x_ref, w1_ref, cb_ref, w2_ref, o_ref):
    u0 = jnp.zeros((NCP, CMP_HID), F32)
    u1 = jnp.zeros((NCP, CMP_HID), F32)
    for pp in range(D_CMP // 2):
        xa = x_ref[pl.ds(2 * pp, NCP, stride=D_CMP), :]
        xb = x_ref[pl.ds(2 * pp + 1, NCP, stride=D_CMP), :]
        xp = jnp.concatenate([xa, xb], axis=1).astype(BF16)
        u0 = u0 + _dot(xp, w1_ref[pl.ds(pp * 2 * DH, 2 * DH), :])
        u1 = u1 + _dot(xp, w1_ref[pl.ds(D_CMP * DH + pp * 2 * DH, 2 * DH), :])
    h = _compress_hidden(u0, u1, cb_ref[...])
    o_ref[...] = _dot(_gelu_tanh(h).astype(BF16), w2_ref[...])


def _compress_prompt(P, w1, cb, w2, B, T):
    assert T // D_CMP == NCP
    kv0 = A_KV // DH
    return pl.pallas_call(
        _compress_prompt_body,
        grid=(B, 2, HKV),
        in_specs=[pl.BlockSpec((T, DH), lambda b, s, g: (b, kv0 + HKV * s + g)),
                  pl.BlockSpec((None, L_CMP * DH, CMP_HID), lambda b, s, g: (s, 0, 0)),
                  pl.BlockSpec((None, 1, CMP_HID), lambda b, s, g: (s, 0, 0)),
                  pl.BlockSpec((None, CMP_HID, DH), lambda b, s, g: (s, 0, 0))],
        out_specs=pl.BlockSpec((None, None, None, NCP, DH), lambda b, s, g: (b, s, g, 0, 0)),
        out_shape=jax.ShapeDtypeStruct((B, 2, HKV, NCP, DH), F32),
        compiler_params=_params(("arbitrary", "arbitrary", "arbitrary")),
    )(P, w1, cb, w2)


def _select_blocks(imp, tpos_tok, ns):
    blk = lax.broadcasted_iota(jnp.int32, (1, NSP), 1)
    cur = tpos_tok // L_SEL
    valid = blk <= cur
    forced = (blk == 0) | (blk == cur) | (blk == cur - 1)
    score = jnp.where(valid, jnp.where(forced, FORCE_SCORE, imp), -jnp.inf)
    rank = jnp.zeros(score.shape, F32)
    for j in range(ns):
        sj = score[:, j:j + 1]
        beats = (sj > score) | ((sj == score) & (blk > j))
        rank = rank + jnp.where(beats, 1.0, 0.0)
    return jnp.where(rank < float(min(N_TOP, ns)), 1.0, 0.0)


def _block_onehot(kpos, width):
    rows = lax.broadcasted_iota(jnp.int32, (NSP, width), 0)
    return jnp.where(rows == kpos // L_SEL, 1.0, 0.0).astype(BF16)


def _compressed_branch(qb, kc, vc, tpos, nc, scale):
    s = _dot_nt(qb, kc.astype(BF16)) * scale
    cidx = lax.broadcasted_iota(jnp.int32, (1, NCP), 1)
    cmask = (cidx < nc) & (cidx * D_CMP + (L_CMP - 1) <= tpos)
    p = _masked_softmax_pre(jnp.where(cmask, s, NEG))
    return p, _dot(p.astype(BF16), vc.astype(BF16))


def _flash(qb, k_ref, v_ref, lo, hi, tk, mask_fn, scale, acc_ref, m_ref, l_ref):
    acc_ref[...] = jnp.zeros(acc_ref.shape, F32)
    m_ref[...] = jnp.full(m_ref.shape, NEG, F32)
    l_ref[...] = jnp.zeros(l_ref.shape, F32)

    def body(kt, carry):
        k0 = pl.multiple_of(kt * tk, tk)
        kk = k_ref[pl.ds(k0, tk), :].astype(BF16)
        vv = v_ref[pl.ds(k0, tk), :].astype(BF16)
        s = _dot_nt(qb, kk) * scale
        kpos = k0 + lax.broadcasted_iota(jnp.int32, (1, tk), 1)
        mask = mask_fn(kpos)
        sm = jnp.where(mask, s, NEG)
        m_old = m_ref[...]
        m_new = jnp.maximum(m_old, jnp.max(sm, axis=-1, keepdims=True))
        alpha = jnp.exp(m_old - m_new)
        e = jnp.where(mask, jnp.exp(sm - m_new), 0.0)
        l_ref[...] = alpha * l_ref[...] + jnp.sum(e, axis=-1, keepdims=True)
        acc_ref[...] = alpha * acc_ref[...] + _dot(e.astype(BF16), vv)
        m_ref[...] = m_new
        return carry

    lax.fori_loop(lo, hi, body, 0)
    return acc_ref[...] / jnp.maximum(l_ref[...], 1e-30)


def _nsa_prompt_body(q_ref, kc_ref, vc_ref, ks_ref, vs_ref, kw_ref, vw_ref, misc_ref, z_ref, c2s_ref,
                     o_ref, acc_ref, m_ref, l_ref, *, tq, tk, T):
    qi = pl.program_id(2)
    t0 = qi * tq
    R = GRP * tq
    scale = DH ** -0.5
    ns = -(-T // L_SEL)
    nc = T // D_CMP - CMP_R + 1
    qb = jnp.concatenate([q_ref[:, r * DH:(r + 1) * DH] for r in range(GRP)], axis=0).astype(BF16)
    tpos = t0 + (lax.broadcasted_iota(jnp.int32, (R, 1), 0) & (tq - 1))
    tpos_tok = t0 + lax.broadcasted_iota(jnp.int32, (tq, 1), 0)

    p_c, o_c = _compressed_branch(qb, kc_ref[...], vc_ref[...], tpos, nc, scale)
    psum = p_c[0:tq]
    for r in range(1, GRP):
        psum = psum + p_c[r * tq:(r + 1) * tq]
    imp = _dot_exact_rhs(psum, c2s_ref[...])
    sel = _select_blocks(imp, tpos_tok, ns)
    sel4 = jnp.concatenate([sel] * GRP, axis=0).astype(BF16)

    def sel_mask(kpos):
        chosen = _dot(sel4, _block_onehot(kpos, tk))
        return (chosen > 0.5) & (kpos <= tpos)

    o_s = _flash(qb, ks_ref, vs_ref, 0, lax.div(t0, tk) + 1, tk, sel_mask, scale, acc_ref, m_ref, l_ref)

    def win_mask(kpos):
        return (kpos <= tpos) & (kpos > tpos - WINDOW)

    o_w = _flash(qb, kw_ref, vw_ref, jnp.maximum(qi - WINDOW // tq, 0), qi + 1, tq, win_mask, scale,
                 acc_ref, m_ref, l_ref)

    gm = _sigmoid(misc_ref[:, 0:GRP * 3])

    def gate(j):
        return jnp.concatenate([gm[:, r * 3 + j:r * 3 + j + 1] for r in range(GRP)], axis=0)

    o = gate(0) * o_c + gate(1) * o_s + gate(2) * o_w
    for r in range(GRP):
        zr = z_ref[:, r * DH:(r + 1) * DH]
        o_ref[:, r * DH:(r + 1) * DH] = (o[r * tq:(r + 1) * tq] * _silu(zr)).astype(BF16)


def _nsa_prompt(P, kcv, c2s, B, T, tq, tk):
    assert T % tq == 0 and tq & (tq - 1) == 0 and tk % tq == 0 and T % tk == 0 and WINDOW % tq == 0
    nq = T // tq
    kv0 = A_KV // DH
    gw = GRP * DH
    R = GRP * tq

    def kvspec(slot):
        return pl.BlockSpec((T, DH), lambda b, g, i: (b, kv0 + HKV * slot + g))

    return pl.pallas_call(
        functools.partial(_nsa_prompt_body, tq=tq, tk=tk, T=T),
        grid=(B, HKV, nq),
        in_specs=[pl.BlockSpec((tq, gw), lambda b, g, i: (b * nq + i, A_Q // gw + g)),
                  pl.BlockSpec((None, None, None, NCP, DH), lambda b, g, i: (b, 0, g, 0, 0)),
                  pl.BlockSpec((None, None, None, NCP, DH), lambda b, g, i: (b, 1, g, 0, 0)),
                  kvspec(2), kvspec(3), kvspec(4), kvspec(5),
                  pl.BlockSpec((tq, LANES), lambda b, g, i: (b * nq + i, A_MISC // LANES + g)),
                  pl.BlockSpec((tq, gw), lambda b, g, i: (b * nq + i, A_ZA // gw + g)),
                  pl.BlockSpec((NCP, NSP), lambda b, g, i: (0, 0))],
        out_specs=pl.BlockSpec((tq, gw), lambda b, g, i: (b * nq + i, g)),
        out_shape=jax.ShapeDtypeStruct((B * T, H_A * DH), BF16),
        scratch_shapes=[pltpu.VMEM((R, DH), F32), pltpu.VMEM((R, 1), F32), pltpu.VMEM((R, 1), F32)],
        compiler_params=_params(("arbitrary", "arbitrary", "arbitrary")),
    )(P, kcv, kcv, P, P, P, P, P, P, c2s)


def _pad_rows(x, rows):
    return jnp.concatenate([x, jnp.zeros((rows - x.shape[0], x.shape[1]), x.dtype)], axis=0)


def _nsa_sample_body(pt_ref, *refs, n_pages, page, past_len, dec_seq, wb):
    pages = refs[:n_pages]
    (win_ref, q_ref, kvr_ref, kvw_ref, misc_ref, z_ref, w1_ref, cb_ref, w2_ref, c2s_ref, o_ref) = refs[n_pages:]
    del pt_ref
    scale = DH ** -0.5
    L = past_len + dec_seq
    ns = -(-L // L_SEL)
    nc = L // D_CMP - CMP_R + 1
    R = GRP * TP
    per_page = page // D_CMP
    nch = N_KV_SLOTS * HKV

    def page_rows(pg, ch):
        return pg[pl.ds(ch, page, stride=nch), :]

    kcv = {}
    for s in range(2):
        u0 = jnp.zeros((HKV * NCP, CMP_HID), F32)
        u1 = jnp.zeros((HKV * NCP, CMP_HID), F32)
        for pp in range(D_CMP // 2):
            xs = []
            for g in range(HKV):
                ch = s * HKV + g
                xa = jnp.concatenate([pg[pl.ds(2 * pp * nch + ch, per_page, stride=D_CMP * nch), :] for pg in pages],
                                     axis=0)
                xb = jnp.concatenate([pg[pl.ds((2 * pp + 1) * nch + ch, per_page, stride=D_CMP * nch), :]
                                      for pg in pages], axis=0)
                xs.append(jnp.concatenate([xa, xb], axis=1))
            xp = jnp.concatenate(xs, axis=0).astype(BF16)
            u0 = u0 + _dot(xp, w1_ref[s, pl.ds(pp * 2 * DH, 2 * DH), :])
            u1 = u1 + _dot(xp, w1_ref[s, pl.ds(D_CMP * DH + pp * 2 * DH, 2 * DH), :])
        for g in range(HKV):
            h = _compress_hidden(u0[g * NCP:(g + 1) * NCP], u1[g * NCP:(g + 1) * NCP], cb_ref[s])
            kcv[s, g] = _dot(_gelu_tanh(h).astype(BF16), w2_ref[s])

    trow = lax.broadcasted_iota(jnp.int32, (R, 1), 0) & (TP - 1)
    tpos = past_len + trow
    tpos_tok = past_len + lax.broadcasted_iota(jnp.int32, (TP, 1), 0)
    lane = lax.broadcasted_iota(jnp.int32, (1, LANES), 1)
    new_pos = past_len + lane

    for g in range(HKV):
        qb = jnp.concatenate([q_ref[:, (g * GRP + r) * DH:(g * GRP + r + 1) * DH] for r in range(GRP)],
                             axis=0).astype(BF16)
        p_c, o_c = _compressed_branch(qb, kcv[0, g], kcv[1, g], tpos, nc, scale)
        psum = p_c[0:TP]
        for r in range(1, GRP):
            psum = psum + p_c[r * TP:(r + 1) * TP]
        imp = _dot_exact_rhs(psum, c2s_ref[...])
        sel = _select_blocks(imp, tpos_tok, ns)
        sel4 = jnp.concatenate([sel] * GRP, axis=0).astype(BF16)

        def sel_scores(kk, kpos):
            s = _dot_nt(qb, kk.astype(BF16)) * scale
            chosen = _dot(sel4, _block_onehot(kpos, LANES))
            return jnp.where((chosen > 0.5) & (kpos <= tpos) & (kpos < L), s, NEG)

        ksel_col = (2 * HKV + g) * DH
        vsel_col = (3 * HKV + g) * DH
        pieces = [sel_scores(page_rows(pages[j], 2 * HKV + g), j * page + lane) for j in range(n_pages)]
        pieces.append(sel_scores(_pad_rows(kvr_ref[:, ksel_col:ksel_col + DH], LANES), new_pos))
        p = _masked_softmax_pre(jnp.concatenate(pieces, axis=1))
        o_s = _dot(p[:, n_pages * page:].astype(BF16),
                   _pad_rows(kvr_ref[:, vsel_col:vsel_col + DH], LANES).astype(BF16))
        for j in range(n_pages):
            o_s = o_s + _dot(p[:, j * page:(j + 1) * page].astype(BF16),
                             page_rows(pages[j], 3 * HKV + g).astype(BF16))

        def win_scores(kk, kpos):
            s = _dot_nt(qb, kk.astype(BF16)) * scale
            return jnp.where((kpos <= tpos) & (kpos > tpos - WINDOW) & (kpos < L), s, NEG)

        wpos = (past_len - wb) + lax.broadcasted_iota(jnp.int32, (1, wb), 1)
        kw_col = g * DH
        vw_col = (HKV + g) * DH
        pw = _masked_softmax_pre(jnp.concatenate(
            [win_scores(win_ref[pl.ds(g, wb, stride=2 * HKV), :], wpos),
             win_scores(_pad_rows(kvw_ref[:, kw_col:kw_col + DH], LANES), new_pos)], axis=1))
        o_w = (_dot(pw[:, :wb].astype(BF16), win_ref[pl.ds(HKV + g, wb, stride=2 * HKV), :].astype(BF16))
               + _dot(pw[:, wb:].astype(BF16), _pad_rows(kvw_ref[:, vw_col:vw_col + DH], LANES).astype(BF16)))

        gm = _sigmoid(misc_ref[:, g * LANES:g * LANES + GRP * 3])

        def gate(j):
            return jnp.concatenate([gm[:, r * 3 + j:r * 3 + j + 1] for r in range(GRP)], axis=0)

        o = gate(0) * o_c + gate(1) * o_s + gate(2) * o_w
        for r in range(GRP):
            c = (g * GRP + r) * DH
            o_ref[:, c:c + DH] = o[r * TP:(r + 1) * TP] * _silu(z_ref[:, c:c + DH])


def _nsa_sample(Ps, cache, cache_win, page_table, w1, cb, w2, c2s, dec_seq):
    NB, n_pages = page_table.shape
    n_pool, prow, _ = cache.shape
    page = prow // (N_KV_SLOTS * HKV)
    wb = cache_win.shape[1] // (2 * HKV)
    past_len = n_pages * page
    L = past_len + dec_seq
    assert wb == WINDOW and (L // D_CMP) == NCP and (L // D_CMP) * D_CMP <= past_len and dec_seq <= TP
    assert -(-L // L_SEL) <= NSP and page % L_SEL == 0
    qw = H_A * DH

    def page_spec(j):
        return pl.BlockSpec((None, prow, DH), lambda b, pt: (pt[b * n_pages + j], 0, 0))

    grid_spec = pltpu.PrefetchScalarGridSpec(
        num_scalar_prefetch=1,
        grid=(NB,),
        in_specs=[page_spec(j) for j in range(n_pages)] + [
            pl.BlockSpec((None, wb * 2 * HKV, DH), lambda b, pt: (b, 0, 0)),
            pl.BlockSpec((TP, qw), lambda b, pt: (b, A_Q // qw)),
            pl.BlockSpec((TP, 4 * HKV * DH), lambda b, pt: (b, A_KV // (4 * HKV * DH))),
            pl.BlockSpec((TP, 2 * HKV * DH), lambda b, pt: (b, (A_KV + 4 * HKV * DH) // (2 * HKV * DH))),
            pl.BlockSpec((TP, HKV * LANES), lambda b, pt: (b, A_MISC // (HKV * LANES))),
            pl.BlockSpec((TP, qw), lambda b, pt: (b, A_ZA // qw)),
            pl.BlockSpec((2, L_CMP * DH, CMP_HID), lambda b, pt: (0, 0, 0)),
            pl.BlockSpec((2, 1, CMP_HID), lambda b, pt: (0, 0, 0)),
            pl.BlockSpec((2, CMP_HID, DH), lambda b, pt: (0, 0, 0)),
            pl.BlockSpec((NCP, NSP), lambda b, pt: (0, 0))],
        out_specs=pl.BlockSpec((TP, qw), lambda b, pt: (b, 0)),
    )
    return pl.pallas_call(
        functools.partial(_nsa_sample_body, n_pages=n_pages, page=page, past_len=past_len, dec_seq=dec_seq, wb=wb),
        grid_spec=grid_spec,
        out_shape=jax.ShapeDtypeStruct((NB * TP, qw), F32),
        compiler_params=_params(("arbitrary",)),
    )(page_table.reshape(-1), *([cache] * n_pages), cache_win, Ps, Ps, Ps, Ps, Ps, w1, cb, w2, c2s)


def _rec_chunk(q, k, v, la, gate, S, C, last_row, nh, K, V):
    sb = min(16, C)
    ri = lax.broadcasted_iota(jnp.int32, (C, C), 0)
    cj = lax.broadcasted_iota(jnp.int32, (C, C), 1)
    causal = cj <= ri
    tri = jnp.where(causal, 1.0, 0.0).astype(BF16)
    b = _dot_exact_lhs(tri, la)
    qe = (q * jnp.exp(b)).astype(BF16)
    qis, kis = [], []
    for i in range(C // sb):
        r0 = i * sb
        ci = b[r0 + sb // 2:r0 + sb // 2 + 1, :]
        qis.append((q[r0:r0 + sb] * jnp.exp(b[r0:r0 + sb] - ci)).astype(BF16))
        kis.append((k * jnp.exp(jnp.minimum(ci - b, 80.0))).astype(BF16))
    b_last = b[last_row:last_row + 1, :]
    rowi = lax.broadcasted_iota(jnp.int32, (C, 1), 0)
    kd = jnp.where(rowi <= last_row, k * jnp.exp(jnp.minimum(b_last - b, 0.0)), 0.0)
    stack = jnp.concatenate([kd, jnp.broadcast_to(b_last, (SUBLANES, nh * K)),
                             jnp.zeros((LANES - C - SUBLANES, nh * K), F32)], axis=0)
    vb = v.astype(BF16)
    v_pad = jnp.concatenate([vb, jnp.zeros((LANES - C, nh * V), BF16)], axis=0)
    outs, s_new = [], []
    for h in range(nh):
        ks = slice(h * K, (h + 1) * K)
        vs = slice(h * V, (h + 1) * V)
        o = _dot(qe[:, ks], S[h].astype(BF16))
        rows = [_dot_nt(qi[:, ks], ki[:, ks]) for qi, ki in zip(qis, kis)]
        att = rows[0] if len(rows) == 1 else jnp.concatenate(rows, axis=0)
        att = jnp.where(causal, att, 0.0)
        o = o + _dot(att.astype(BF16), vb[:, vs])
        stack_t = stack[:, ks].T
        a_col = jnp.exp(stack_t[:, C:C + 1])
        s_new.append(a_col * S[h] + _dot(stack_t.astype(BF16), v_pad[:, vs]))
        ms = jnp.mean(o * o, axis=-1, keepdims=True)
        outs.append(o * lax.rsqrt(ms + EPS) * gate[:, vs])
    return outs, s_new


def _rec_body(*refs, variant, nh, K, V, C, TB, last_row, has_s0, layer_idx):
    refs = list(refs)
    q_ref, k_ref, v_ref, z_ref = refs[:4]
    pos = 4
    if variant == "gla":
        misc_ref, w2_ref, gb_ref = refs[pos:pos + 3]
        pos += 3
    else:
        lb_ref = refs[pos]
        pos += 1
    gn_ref = refs[pos]
    pos += 1
    if has_s0:
        s0_ref = refs[pos]
        pos += 1
    o_ref, s_ref = refs[pos:pos + 2]
    n_chunks = TB // C
    single = n_chunks == 1 and has_s0

    if not single:
        @pl.when(pl.program_id(2) == 0)
        def _():
            if has_s0:
                s_ref[...] = s0_ref[...]
            else:
                s_ref[...] = jnp.zeros(s_ref.shape, F32)

    if variant == "hgrn":
        lg = lb_ref[...]
        e = jnp.exp(lg - jnp.max(lg, axis=0, keepdims=True))
        prob = e / jnp.sum(e, axis=0, keepdims=True)
        lb = prob[1:2]
        for i in range(2, layer_idx + 1):
            lb = lb + prob[i:i + 1]
    gn_all = jnp.concatenate([gn_ref[...]] * nh, axis=1)

    def chunk(c0):
        rows = pl.ds(c0, C)
        qr = q_ref[rows, :]
        kr = k_ref[rows, :]
        v = v_ref[rows, :]
        gate = _silu(z_ref[rows, :]) * gn_all
        if variant == "gla":
            zg = _dot(misc_ref[rows, :].astype(BF16), w2_ref[...].astype(BF16)) + gb_ref[...]
            q, k, la = qr * (K ** -0.5), kr, _log_sigmoid(zg) / GLA_GATE_NORM
        else:
            t = jnp.exp(-jnp.abs(kr))
            r = 1.0 / (1.0 + t)
            tr = t * r
            nonneg = kr >= 0.0
            sig = jnp.where(nonneg, r, tr)
            nsig = jnp.where(nonneg, tr, r)
            q, k, la = _silu(qr), (1.0 - lb) * nsig, jnp.log(lb + (1.0 - lb) * sig)
        src = s0_ref if single else s_ref
        outs, s_new = _rec_chunk(q, k, v, la, gate, [src[h] for h in range(nh)], C, last_row, nh, K, V)
        for h in range(nh):
            o_ref[rows, h * V:(h + 1) * V] = outs[h].astype(o_ref.dtype)
            s_ref[h] = s_new[h]

    if n_chunks == 1:
        chunk(0)
    else:
        def body(ci, carry):
            chunk(pl.multiple_of(ci * C, C))
            return carry

        lax.fori_loop(0, n_chunks, body, 0, unroll=2)


def _recurrence(P, cols, extra, gn, s0, *, variant, B, T, H, K, V, C, TB, nh, last_row, out_dtype, layer_idx=1):
    cq, ck, cv, cz = cols
    assert H % nh == 0 and T % TB == 0 and TB % C == 0
    nt = T // TB

    def colspec(c0, w):
        assert c0 % (nh * w) == 0
        return pl.BlockSpec((TB, nh * w), lambda b, j, t: (b * nt + t, c0 // (nh * w) + j))

    in_specs = [colspec(cq, K), colspec(ck, K), colspec(cv, V), colspec(cz, V)]
    args = [P, P, P, P]
    if variant == "gla":
        w2p, gb = extra
        in_specs += [pl.BlockSpec((TB, LANES), lambda b, j, t: (b * nt + t, A_MISC // LANES)),
                     pl.BlockSpec((LANES, nh * K), lambda b, j, t: (0, j)),
                     pl.BlockSpec((1, nh * K), lambda b, j, t: (0, j))]
        args += [P, w2p, gb]
    else:
        (lb_logits,) = extra
        in_specs += [pl.BlockSpec((lb_logits.shape[0], nh * K), lambda b, j, t: (0, j))]
        args += [lb_logits]
    in_specs += [pl.BlockSpec((1, V), lambda b, j, t: (0, 0))]
    args += [gn]
    st_spec = pl.BlockSpec((None, None, nh, K, V), lambda b, j, t: (0, b, j, 0, 0))
    if s0 is not None:
        in_specs += [st_spec]
        args += [s0]
    return pl.pallas_call(
        functools.partial(_rec_body, variant=variant, nh=nh, K=K, V=V, C=C, TB=TB, last_row=last_row,
                          has_s0=s0 is not None, layer_idx=layer_idx),
        grid=(B, H // nh, nt),
        in_specs=in_specs,
        out_specs=[pl.BlockSpec((TB, nh * V), lambda b, j, t: (b * nt + t, j)), st_spec],
        out_shape=[jax.ShapeDtypeStruct((B * T, H * V), out_dtype),
                   jax.ShapeDtypeStruct((1, B, H, K, V), F32)],
        compiler_params=_params(("arbitrary", "arbitrary", "arbitrary")),
    )(*args)


def _layout_a_w_in(w):
    off = np.concatenate([[0], np.cumsum(A_SIZES)])
    q, kv, gbr, za, qb, kb, vb, lr, zb = [w[:, off[i]:off[i + 1]] for i in range(len(A_SIZES))]
    D = w.shape[0]
    z = lambda n: jnp.zeros((D, n), w.dtype)
    misc0 = jnp.concatenate([gbr[:, :GRP * 3], lr, z(LANES - GRP * 3 - GLA_LR)], axis=1)
    misc1 = jnp.concatenate([gbr[:, GRP * 3:], z(LANES - GRP * 3)], axis=1)
    out = jnp.concatenate([q, vb, zb, za, qb, kb, kv, misc0, misc1, z(NA - A_MISC - HKV * LANES)], axis=1)
    assert out.shape[1] == NA
    return out.astype(BF16)


def _cmp_to_sel():
    c_start = np.arange(NCP)[:, None] * D_CMP
    s_start = np.arange(NSP)[None, :] * L_SEL
    overlap = np.clip(np.minimum(c_start + L_CMP, s_start + L_SEL) - np.maximum(c_start, s_start), 0, None)
    return jnp.asarray(overlap / D_CMP, dtype=BF16)


def kernel(x_prompt, x_sample, cache_kv, cache_win, state_gla, state_hgrn, page_table, a_norm, a_w_in, a_gla_w2,
           a_gla_b, a_gla_gn, a_cmp_pe, a_cmp_w1, a_cmp_b1, a_cmp_w2, a_w_out, c_norm, c_w_in, c_lb_logits, c_gn,
           c_w_out, final_norm):
    B, T, D = x_prompt.shape
    NB, TS, _ = x_sample.shape
    n_pool, page = cache_kv.shape[1], cache_kv.shape[2]
    wb = cache_win.shape[2]
    assert a_norm.shape[0] == 1 and c_norm.shape[0] == 1 and c_lb_logits.shape[0] == 2

    wa = _layout_a_w_in(a_w_in[0])
    wc = c_w_in[0].astype(BF16)
    wao = a_w_out[0].astype(BF16)
    wco = c_w_out[0].astype(BF16)
    w1 = a_cmp_w1[0].astype(BF16)
    w2 = a_cmp_w2[0].astype(BF16)
    pe = a_cmp_pe[0].reshape(2, 1, L_CMP * DH)
    b1 = a_cmp_b1[0].reshape(2, 1, CMP_HID)
    w2p = jnp.zeros((LANES, H_B * DK_B), F32).at[MISC_LR:MISC_LR + GLA_LR, :].set(a_gla_w2[0])
    gb = a_gla_b[0].reshape(1, H_B * DK_B)
    c2s = _cmp_to_sel()
    a_nw = a_norm[0].reshape(1, D)
    c_nw = c_norm[0].reshape(1, D)
    f_nw = final_norm.reshape(1, D)
    gla_gn = a_gla_gn[0].reshape(1, DV_B)
    hg_gn = c_gn[0].reshape(1, DV_C)

    xp = x_prompt.reshape(B * T, D)
    xs = jnp.pad(x_sample, ((0, 0), (0, TP - TS), (0, 0))).reshape(NB * TP, D)

    cb = _cmp_bias(pe, w1, b1)

    Pp = _norm_matmul(xp, a_nw, wa, tm=1024, tn=512)
    kcv = _compress_prompt(Pp, w1, cb, w2, B, T)
    oa_p = _nsa_prompt(Pp, kcv, c2s, B, T, tq=128, tk=512)
    ob_p, gla_p = _recurrence(Pp, (A_QB, A_KB, A_VB, A_ZB), (w2p, gb), gla_gn, None, variant="gla", B=B, T=T,
                              H=H_B, K=DK_B, V=DV_B, C=64, TB=512, nh=4, last_row=63, out_dtype=BF16)
    y1p = _out_proj(oa_p, 0, ob_p, 0, wao, xp, a_nw, tm=512, final_norm=False)

    Ps = _norm_matmul(xs, a_nw, wa, tm=NB * TP, tn=512)
    cache2 = cache_kv[0].reshape(n_pool, page * N_KV_SLOTS * HKV, DH)
    win2 = cache_win[0].reshape(NB, wb * 2 * HKV, DH)
    oa_s = _nsa_sample(Ps, cache2, win2, page_table, w1, cb, w2, c2s, TS)
    ob_s, gla_s = _recurrence(Ps, (A_QB, A_KB, A_VB, A_ZB), (w2p, gb), gla_gn, state_gla, variant="gla", B=NB, T=TP,
                              H=H_B, K=DK_B, V=DV_B, C=TP, TB=TP, nh=H_B, last_row=TS - 1, out_dtype=F32)
    y1s = _out_proj(oa_s, 0, ob_s, 0, wao, xs, a_nw, tm=min(512, NB * TP), final_norm=False)

    Pc = _norm_matmul(y1p, c_nw, wc, tm=1024, tn=512)
    hc = H_C * DK_C
    oc_p, hg_p = _recurrence(Pc, (0, hc, 2 * hc, 3 * hc), (c_lb_logits,), hg_gn, None, variant="hgrn", B=B, T=T,
                             H=H_C, K=DK_C, V=DV_C, C=64, TB=512, nh=4, last_row=63, out_dtype=BF16)
    y_prompt = _out_proj(oc_p, 0, oc_p, 1, wco, y1p, f_nw, tm=512, final_norm=True)

    Pcs = _norm_matmul(y1s, c_nw, wc, tm=NB * TP, tn=512)
    oc_s, hg_s = _recurrence(Pcs, (0, hc, 2 * hc, 3 * hc), (c_lb_logits,), hg_gn, state_hgrn, variant="hgrn", B=NB,
                             T=TP, H=H_C, K=DK_C, V=DV_C, C=TP, TB=TP, nh=H_C, last_row=TS - 1, out_dtype=F32)
    y_sample = _out_proj(oc_s, 0, oc_s, 1, wco, y1s, f_nw, tm=min(512, NB * TP), final_norm=True)

    kvw = 6 * HKV * DH
    nrw = N_KV_SLOTS * HKV * DH
    Pp3 = Pp.reshape(B, T, NA)
    Ps3 = Ps.reshape(NB, TP, NA)
    kv_p = Pp3[:, :, A_KV:A_KV + nrw].reshape(1, B, T, N_KV_SLOTS, HKV, DH)
    kv_s = Ps3[:, :TS, A_KV:A_KV + nrw].reshape(1, NB, TS, N_KV_SLOTS, HKV, DH)
    keep = min(WINDOW, T)
    win_p = Pp3[:, T - keep:, A_KV + nrw:A_KV + kvw].reshape(1, B, keep, 2, HKV, DH)
    buf = jnp.concatenate([cache_win[0], Ps3[:, :TS, A_KV + nrw:A_KV + kvw].reshape(NB, TS, 2, HKV, DH)], axis=1)
    win_s = buf[:, buf.shape[1] - wb:][None]
    return (y_prompt.reshape(B, T, D), y_sample.reshape(NB, TP, D)[:, :TS], kv_p, kv_s, win_p, win_s,
            gla_p, gla_s, hg_p, hg_s)
```

```python
import functools

import numpy as np
import jax
import jax.numpy as jnp
from jax import lax
from jax.experimental import pallas as pl
from jax.experimental.pallas import tpu as pltpu

F32 = jnp.float32
BF16 = jnp.bfloat16

D_MODEL = 2048
DH = 128
H_A = 8
HKV = 2
GRP = H_A // HKV
L_CMP = 32
D_CMP = 16
CMP_R = L_CMP // D_CMP
CMP_HID = 2 * DH
L_SEL = 64
N_TOP = 16
WINDOW = 512
N_KV_SLOTS = 4
H_B = 4
DK_B = 128
DV_B = 256
GLA_LR = 16
GLA_GATE_NORM = 16.0
DK_C = 128
H_C = 16
DV_C = 128
EPS = 1e-6
FORCE_SCORE = 1e9
A_SIZES = (H_A * DH, 6 * HKV * DH, 3 * H_A, H_A * DH, H_B * DK_B, H_B * DK_B, H_B * DV_B, GLA_LR, H_B * DV_B)

LANES = 128
SUBLANES = 8
VMEM_LIMIT = 56 * 1024 * 1024

A_Q = 0
A_VB = 1024
A_ZB = 2048
A_ZA = 3072
A_QB = 4096
A_KB = 4608
A_KV = 5120
A_MISC = 6656
NA = 7168
MISC_LR = GRP * 3

NCP = 128
NSP = 128
TP = 8
NEG = -1e30
LOG2E = 1.4426950408889634


def _dot(a, b):
    return jnp.dot(a, b, preferred_element_type=F32)


def _dot_nt(a, b):
    return lax.dot_general(a, b, (((1,), (1,)), ((), ())), preferred_element_type=F32)


def _split3(a):
    hi = a.astype(BF16)
    r = a - hi.astype(F32)
    mid = r.astype(BF16)
    lo = (r - mid.astype(F32)).astype(BF16)
    return hi, mid, lo


def _dot_exact_rhs(a, b_exact):
    hi, mid, lo = _split3(a)
    return _dot(hi, b_exact) + _dot(mid, b_exact) + _dot(lo, b_exact)


def _dot_exact_lhs(a_exact, b):
    hi, mid, lo = _split3(b)
    return _dot(a_exact, hi) + _dot(a_exact, mid) + _dot(a_exact, lo)


def _sigmoid(x):
    return 1.0 / (1.0 + jnp.exp(-x))


def _silu(x):
    return x * _sigmoid(x)


def _log_sigmoid(x):
    return jnp.minimum(x, 0.0) - jnp.log(1.0 + jnp.exp(-jnp.abs(x)))


def _gelu_tanh(x):
    return 0.5 * x * (1.0 + jnp.tanh(np.sqrt(2.0 / np.pi) * (x + 0.044715 * (x * x * x))))


def _masked_softmax_pre(sm):
    m = jnp.max(sm, axis=-1, keepdims=True)
    e = jnp.where(sm > 0.5 * NEG, jnp.exp(sm - m), 0.0)
    l = jnp.sum(e, axis=-1, keepdims=True)
    return e / jnp.maximum(l, 1e-30)


def _params(sem):
    return pltpu.CompilerParams(dimension_semantics=sem, vmem_limit_bytes=VMEM_LIMIT)


def _norm_matmul_body(x_ref, nw_ref, w_ref, o_ref, h_ref):
    @pl.when(pl.program_id(1) == 0)
    def _():
        x = x_ref[...]
        ms = jnp.mean(x * x, axis=-1, keepdims=True)
        h_ref[...] = (x * lax.rsqrt(ms + EPS) * nw_ref[...]).astype(BF16)

    o_ref[...] = _dot(h_ref[...], w_ref[...])


def _norm_matmul(x, nw, w, tm, tn):
    M, D = x.shape
    N = w.shape[1]
    assert M % tm == 0 and N % tn == 0
    return pl.pallas_call(
        _norm_matmul_body,
        grid=(M // tm, N // tn),
        in_specs=[pl.BlockSpec((tm, D), lambda i, j: (i, 0)),
                  pl.BlockSpec((1, D), lambda i, j: (0, 0)),
                  pl.BlockSpec((D, tn), lambda i, j: (0, j))],
        out_specs=pl.BlockSpec((tm, tn), lambda i, j: (i, j)),
        out_shape=jax.ShapeDtypeStruct((M, N), F32),
        scratch_shapes=[pltpu.VMEM((tm, D), BF16)],
        compiler_params=_params(("arbitrary", "arbitrary")),
    )(x, nw, w)


def _out_proj_body(a1_ref, a2_ref, w1_ref, w2_ref, x_ref, nw_ref, y_ref, *, final_norm):
    y = x_ref[...] + _dot(a1_ref[...].astype(BF16), w1_ref[...]) + _dot(a2_ref[...].astype(BF16), w2_ref[...])
    if final_norm:
        ms = jnp.mean(y * y, axis=-1, keepdims=True)
        y = y * lax.rsqrt(ms + EPS) * nw_ref[...]
    y_ref[...] = y


def _out_proj(a1, c1, a2, c2, w, x, nw, tm, final_norm):
    M, D = x.shape
    KH = w.shape[0] // 2
    assert M % tm == 0
    return pl.pallas_call(
        functools.partial(_out_proj_body, final_norm=final_norm),
        grid=(M // tm,),
        in_specs=[pl.BlockSpec((tm, KH), lambda i: (i, c1)),
                  pl.BlockSpec((tm, KH), lambda i: (i, c2)),
                  pl.BlockSpec((KH, D), lambda i: (0, 0)),
                  pl.BlockSpec((KH, D), lambda i: (1, 0)),
                  pl.BlockSpec((tm, D), lambda i: (i, 0)),
                  pl.BlockSpec((1, D), lambda i: (0, 0))],
        out_specs=pl.BlockSpec((tm, D), lambda i: (i, 0)),
        out_shape=jax.ShapeDtypeStruct((M, D), F32),
        compiler_params=_params(("arbitrary",)),
    )(a1, a2, w, w, x, nw)


def _cmp_bias_body(pe_ref, w1_ref, b1_ref, o_ref):
    pe = jnp.broadcast_to(pe_ref[...], (SUBLANES, pe_ref.shape[-1])).astype(BF16)
    o_ref[...] = b1_ref[...] + _dot(pe, w1_ref[...])[0:1, :]


def _cmp_bias(pe, w1, b1):
    S, KF, HID = w1.shape
    return pl.pallas_call(
        _cmp_bias_body,
        grid=(S,),
        in_specs=[pl.BlockSpec((None, 1, KF), lambda s: (s, 0, 0)),
                  pl.BlockSpec((None, KF, HID), lambda s: (s, 0, 0)),
                  pl.BlockSpec((None, 1, HID), lambda s: (s, 0, 0))],
        out_specs=pl.BlockSpec((None, 1, HID), lambda s: (s, 0, 0)),
        out_shape=jax.ShapeDtypeStruct((S, 1, HID), F32),
        compiler_params=_params(("arbitrary",)),
    )(pe, w1, b1)


def _compress_hidden(u0, u1, bias):
    return u0 + pltpu.roll(u1, NCP - 1, 0) + bias


def _compress_prompt_body(x_ref, w1_ref, cb_ref, w2_ref, o_ref):
    u0 = jnp.zeros((NCP, CMP_HID), F32)
    u1 = jnp.zeros((NCP, CMP_HID), F32)
    for pp in range(D_CMP // 2):
        xa = x_ref[pl.ds(2 * pp, NCP, stride=D_CMP), :]
        xb = x_ref[pl.ds(2 * pp + 1, NCP, stride=D_CMP), :]
        xp = jnp.concatenate([xa, xb], axis=1).astype(BF16)
        u0 = u0 + _dot(xp, w1_ref[pl.ds(pp * 2 * DH, 2 * DH), :])
        u1 = u1 + _dot(xp, w1_ref[pl.ds(D_CMP * DH + pp * 2 * DH, 2 * DH), :])
    h = _compress_hidden(u0, u1, cb_ref[...])
    o_ref[...] = _dot(_gelu_tanh(h).astype(BF16), w2_ref[...])


def _compress_prompt(P, w1, cb, w2, B, T):
    assert T // D_CMP == NCP
    kv0 = A_KV // DH
    return pl.pallas_call(
        _compress_prompt_body,
        grid=(B, 2, HKV),
        in_specs=[pl.BlockSpec((T, DH), lambda b, s, g: (b, kv0 + HKV * s + g)),
                  pl.BlockSpec((None, L_CMP * DH, CMP_HID), lambda b, s, g: (s, 0, 0)),
                  pl.BlockSpec((None, 1, CMP_HID), lambda b, s, g: (s, 0, 0)),
                  pl.BlockSpec((None, CMP_HID, DH), lambda b, s, g: (s, 0, 0))],
        out_specs=pl.BlockSpec((None, None, None, NCP, DH), lambda b, s, g: (b, s, g, 0, 0)),
        out_shape=jax.ShapeDtypeStruct((B, 2, HKV, NCP, DH), F32),
        compiler_params=_params(("arbitrary", "arbitrary", "arbitrary")),
    )(P, w1, cb, w2)


def _select_blocks(imp, tpos_tok, ns):
    blk = lax.broadcasted_iota(jnp.int32, (1, NSP), 1)
    cur = tpos_tok // L_SEL
    valid = blk <= cur
    forced = (blk == 0) | (blk == cur) | (blk == cur - 1)
    score = jnp.where(valid, jnp.where(forced, FORCE_SCORE, imp), -jnp.inf)
    k_top = float(min(N_TOP, ns))
    if score.shape[0] == LANES:
        nb = -(-ns // SUBLANES) * SUBLANES
        st = score.T[0:nb]
        rblk = lax.broadcasted_iota(jnp.int32, (nb, 1), 0)
        rank = jnp.zeros(st.shape, F32)
        for j in range(ns):
            sj = st[j:j + 1, :]
            rank = rank + jnp.where((sj > st) | ((sj == st) & (rblk > j)), 1.0, 0.0)
        top = jnp.where(rank < k_top, 1.0, 0.0)
        top = jnp.concatenate([top, jnp.zeros((NSP - nb, LANES), F32)], axis=0).T
        return jnp.where(valid, top, 0.0)
    rank = jnp.zeros(score.shape, F32)
    for j in range(ns):
        sj = score[:, j:j + 1]
        beats = (sj > score) | ((sj == score) & (blk > j))
        rank = rank + jnp.where(beats, 1.0, 0.0)
    return jnp.where(valid & (rank < float(min(N_TOP, ns))), 1.0, 0.0)


def _compressed_branch(qb, kc, vc, tpos, nc, scale):
    s = _dot_nt(qb, kc.astype(BF16)) * scale
    cidx = lax.broadcasted_iota(jnp.int32, (1, NCP), 1)
    cmask = (cidx < nc) & (cidx * D_CMP + (L_CMP - 1) <= tpos)
    p = _masked_softmax_pre(jnp.where(cmask, s, NEG))
    return p, _dot(p.astype(BF16), vc.astype(BF16))


def _keys_with_block_onehot(k_ref, koh_ref, k0, width):
    return jnp.concatenate([k_ref[pl.ds(k0, width), :].astype(BF16), koh_ref[pl.ds(k0, width), :]], axis=1)


def _nsa_prompt_body(q_ref, kc_ref, vc_ref, ks_ref, vs_ref, kw_ref, vw_ref, misc_ref, z_ref, c2s_ref, koh_ref,
                     o_ref, sc_ref, mx_ref, l_ref, acc_ref, *, tq, tk, T):
    qi = pl.program_id(2)
    t0 = pl.multiple_of(qi * tq, tq)
    R = GRP * tq
    scale = DH ** -0.5
    ns = -(-T // L_SEL)
    nc = T // D_CMP - CMP_R + 1
    n_top = min(N_TOP, ns)
    qb = jnp.concatenate([q_ref[:, r * DH:(r + 1) * DH] for r in range(GRP)], axis=0).astype(BF16)
    tpos = t0 + (lax.broadcasted_iota(jnp.int32, (R, 1), 0) & (tq - 1))
    tpos_tok = t0 + lax.broadcasted_iota(jnp.int32, (tq, 1), 0)

    c1 = scale * LOG2E
    nwt = WINDOW // tq + 1
    tiles, starts = [], []
    for i in range(nwt):
        k0 = t0 - WINDOW + i * tq
        k0c = pl.multiple_of(jnp.maximum(k0, 0), tq)
        s = _dot_nt(qb, kw_ref[pl.ds(k0c, tq), :].astype(BF16))
        kpos = k0 + lax.broadcasted_iota(jnp.int32, (1, tq), 1)
        if i == 0:
            s = jnp.where((kpos > tpos - WINDOW) & (kpos >= 0), s, NEG)
        elif i == nwt - 1:
            s = jnp.where(kpos <= tpos, s, NEG)
        else:
            s = jnp.where(k0 >= 0, s, NEG)
        tiles.append(s)
        starts.append(k0c)
    s = jnp.concatenate(tiles, axis=1)
    e = jnp.exp2((s - jnp.max(s, axis=-1, keepdims=True)) * c1)
    o_w = _dot(e[:, 0:tq].astype(BF16), vw_ref[pl.ds(starts[0], tq), :].astype(BF16))
    for i in range(1, nwt):
        o_w = o_w + _dot(e[:, i * tq:(i + 1) * tq].astype(BF16), vw_ref[pl.ds(starts[i], tq), :].astype(BF16))
    o_w = o_w / jnp.sum(e, axis=-1, keepdims=True)

    p_c, o_c = _compressed_branch(qb, kc_ref[...], vc_ref[...], tpos, nc, scale)
    psum = p_c[0:tq]
    for r in range(1, GRP):
        psum = psum + p_c[r * tq:(r + 1) * tq]
    imp = _dot_exact_rhs(psum, c2s_ref[...])

    def all_valid():
        blk = lax.broadcasted_iota(jnp.int32, (1, NSP), 1)
        return jnp.where(blk <= tpos_tok // L_SEL, 1.0, 0.0)

    sel = lax.cond(t0 + tq <= n_top * L_SEL, all_valid, lambda: _select_blocks(imp, tpos_tok, ns))

    sel_bias = (sel - 1.0) * (-NEG)
    blk = lax.broadcasted_iota(jnp.int32, (1, NSP), 1)
    early_bias = jnp.where(blk < lax.div(t0, L_SEL), sel_bias, NEG)
    q_aug = jnp.concatenate([qb, jnp.concatenate([sel_bias.astype(BF16)] * GRP, axis=0)], axis=1)
    q_early = jnp.concatenate([qb, jnp.concatenate([early_bias.astype(BF16)] * GRP, axis=0)], axis=1)
    kpos_d = t0 + lax.broadcasted_iota(jnp.int32, (1, tq), 1)
    s_diag = jnp.where(kpos_d <= tpos, _dot_nt(q_aug, _keys_with_block_onehot(ks_ref, koh_ref, t0, tq)), NEG)
    n_early = lax.div(t0 + tk - 1, tk)
    mx_ref[...] = s_diag

    def pass1(kt, carry):
        k0 = pl.multiple_of(kt * tk, tk)
        s = _dot_nt(q_early, _keys_with_block_onehot(ks_ref, koh_ref, k0, tk))
        sc_ref[kt] = s
        mx = s[:, 0:LANES]
        for c in range(1, tk // LANES):
            mx = jnp.maximum(mx, s[:, c * LANES:(c + 1) * LANES])
        mx_ref[...] = jnp.maximum(mx_ref[...], mx)
        return carry

    lax.fori_loop(0, n_early, pass1, 0)
    c1 = scale * LOG2E
    mx_ref[...] = jnp.broadcast_to(jnp.max(mx_ref[...], axis=-1, keepdims=True) * c1, (R, LANES))
    e = jnp.exp2(s_diag * c1 - mx_ref[...])
    l_ref[...] = e
    acc_ref[...] = _dot(e.astype(BF16), vs_ref[pl.ds(t0, tq), :].astype(BF16))

    def pass2(kt, carry):
        k0 = pl.multiple_of(kt * tk, tk)
        s = sc_ref[kt]
        m2 = mx_ref[...]
        es = [jnp.exp2(s[:, c * LANES:(c + 1) * LANES] * c1 - m2) for c in range(tk // LANES)]
        lsum = es[0]
        for ec in es[1:]:
            lsum = lsum + ec
        l_ref[...] = l_ref[...] + lsum
        e = jnp.concatenate(es, axis=1).astype(BF16)
        acc_ref[...] = acc_ref[...] + _dot(e, vs_ref[pl.ds(k0, tk), :].astype(BF16))
        return carry

    lax.fori_loop(0, n_early, pass2, 0)
    o_s = acc_ref[...] / jnp.sum(l_ref[...], axis=-1, keepdims=True)

    gm = _sigmoid(misc_ref[:, 0:GRP * 3])

    def gate(j):
        return jnp.concatenate([gm[:, r * 3 + j:r * 3 + j + 1] for r in range(GRP)], axis=0)

    o = gate(0) * o_c + gate(1) * o_s + gate(2) * o_w
    for r in range(GRP):
        zr = z_ref[:, r * DH:(r + 1) * DH]
        o_ref[:, r * DH:(r + 1) * DH] = (o[r * tq:(r + 1) * tq] * _silu(zr)).astype(BF16)


def _nsa_prompt(P, kcv, c2s, koh, B, T, tq, tk):
    assert T % tq == 0 and tq & (tq - 1) == 0 and tk % tq == 0 and T % tk == 0 and WINDOW % tq == 0
    nq = T // tq
    kv0 = A_KV // DH
    gw = GRP * DH
    R = GRP * tq

    def kvspec(slot):
        return pl.BlockSpec((T, DH), lambda b, g, i: (b, kv0 + HKV * slot + g))

    return pl.pallas_call(
        functools.partial(_nsa_prompt_body, tq=tq, tk=tk, T=T),
        grid=(B, HKV, nq),
        in_specs=[pl.BlockSpec((tq, gw), lambda b, g, i: (b * nq + i, A_Q // gw + g)),
                  pl.BlockSpec((None, None, None, NCP, DH), lambda b, g, i: (b, 0, g, 0, 0)),
                  pl.BlockSpec((None, None, None, NCP, DH), lambda b, g, i: (b, 1, g, 0, 0)),
                  kvspec(2), kvspec(3), kvspec(4), kvspec(5),
                  pl.BlockSpec((tq, LANES), lambda b, g, i: (b * nq + i, A_MISC // LANES + g)),
                  pl.BlockSpec((tq, gw), lambda b, g, i: (b * nq + i, A_ZA // gw + g)),
                  pl.BlockSpec((NCP, NSP), lambda b, g, i: (0, 0)),
                  pl.BlockSpec((T, NSP), lambda b, g, i: (0, 0))],
        out_specs=pl.BlockSpec((tq, gw), lambda b, g, i: (b * nq + i, g)),
        out_shape=jax.ShapeDtypeStruct((B * T, H_A * DH), BF16),
        scratch_shapes=[pltpu.VMEM((T // tk, R, tk), F32), pltpu.VMEM((R, LANES), F32),
                        pltpu.VMEM((R, LANES), F32), pltpu.VMEM((R, DH), F32)],
        compiler_params=_params(("arbitrary", "arbitrary", "arbitrary")),
    )(P, kcv, kcv, P, P, P, P, P, P, c2s, koh)


def _pad_rows(x, rows):
    return jnp.concatenate([x, jnp.zeros((rows - x.shape[0], x.shape[1]), x.dtype)], axis=0)


def _nsa_sample_body(pt_ref, cache_ref, win_ref, q_ref, kvr_ref, kvw_ref, misc_ref, z_ref, w1_ref, cb_ref, w2_ref,
                     c2s_ref, koh_ref, o_ref, wout_ref, pbuf, sem, *, n_pages, page, past_len, dec_seq, wb):
    b = pl.program_id(0)
    scale = DH ** -0.5
    c1 = scale * LOG2E
    L = past_len + dec_seq
    ns = -(-L // L_SEL)
    nc = L // D_CMP - CMP_R + 1
    R = GRP * TP
    per_page = page // D_CMP
    nch = N_KV_SLOTS * HKV
    grp_rows = D_CMP * nch
    pitch = grp_rows + 1

    def page_copies(seq, slot):
        cps = []
        for j in range(n_pages):
            pid = pt_ref[seq * n_pages + j]
            for n in range(per_page):
                cps.append(pltpu.make_async_copy(
                    cache_ref.at[pid, pl.ds(n * grp_rows, grp_rows), :],
                    pbuf.at[slot, pl.ds((j * per_page + n) * pitch, grp_rows), :], sem.at[slot]))
        return cps

    slot = lax.rem(b, 2)

    @pl.when(b == 0)
    def _():
        for cp in page_copies(0, 0):
            cp.start()

    @pl.when(b + 1 < pl.num_programs(0))
    def _():
        for cp in page_copies(b + 1, 1 - slot):
            cp.start()

    for cp in page_copies(b, slot):
        cp.wait()

    def gather(p, ch):
        return pbuf[slot, pl.ds(p * nch + ch, NCP, stride=pitch), :]

    trow = lax.broadcasted_iota(jnp.int32, (R, 1), 0) & (TP - 1)
    tpos = past_len + trow
    tpos_tok = past_len + lax.broadcasted_iota(jnp.int32, (TP, 1), 0)
    lane = lax.broadcasted_iota(jnp.int32, (1, LANES), 1)
    new_pos = past_len + lane
    new_ok = (new_pos <= tpos) & (new_pos < L)
    new_blk = (past_len + lax.broadcasted_iota(jnp.int32, (LANES, 1), 0)) // L_SEL
    new_oh = jnp.where(lax.broadcasted_iota(jnp.int32, (LANES, NSP), 1) == new_blk, 1.0, 0.0).astype(BF16)
    wpos = (past_len - wb) + lax.broadcasted_iota(jnp.int32, (1, wb), 1)
    wch = 2 * HKV

    qbs, o_ws, raw = [], [], []
    for g in range(HKV):
        qb = jnp.concatenate([q_ref[:, (g * GRP + r) * DH:(g * GRP + r + 1) * DH] for r in range(GRP)],
                             axis=0).astype(BF16)
        kw_col = g * DH
        vw_col = (HKV + g) * DH
        s_w = _dot_nt(qb, win_ref[pl.ds(g, wb, stride=wch), :].astype(BF16))
        s_w = jnp.where((wpos <= tpos) & (wpos > tpos - WINDOW), s_w, NEG)
        s_n = _dot_nt(qb, _pad_rows(kvw_ref[:, kw_col:kw_col + DH], LANES).astype(BF16))
        s_n = jnp.where(new_ok & (new_pos > tpos - WINDOW), s_n, NEG)
        s = jnp.concatenate([s_w, s_n], axis=1)
        e = jnp.exp2((s - jnp.max(s, axis=-1, keepdims=True)) * c1)
        o_w = (_dot(e[:, :wb].astype(BF16), win_ref[pl.ds(HKV + g, wb, stride=wch), :].astype(BF16))
               + _dot(e[:, wb:].astype(BF16), _pad_rows(kvw_ref[:, vw_col:vw_col + DH], LANES).astype(BF16)))
        o_ws.append(o_w / jnp.sum(e, axis=-1, keepdims=True))
        ksel_col = (2 * HKV + g) * DH
        pieces = [_dot_nt(qb, gather(p, 2 * HKV + g).astype(BF16)) for p in range(D_CMP)]
        pieces.append(_dot_nt(qb, _pad_rows(kvr_ref[:, ksel_col:ksel_col + DH], LANES).astype(BF16)))
        raw.append(pieces)
        qbs.append(qb)

    kcv = {}
    for s in range(2):
        u0 = jnp.zeros((HKV * NCP, CMP_HID), F32)
        u1 = jnp.zeros((HKV * NCP, CMP_HID), F32)
        for pp in range(D_CMP // 2):
            xs = []
            for g in range(HKV):
                ch = s * HKV + g
                xs.append(jnp.concatenate([gather(2 * pp, ch), gather(2 * pp + 1, ch)], axis=1))
            xp = jnp.concatenate(xs, axis=0).astype(BF16)
            u0 = u0 + _dot(xp, w1_ref[s, pl.ds(pp * 2 * DH, 2 * DH), :])
            u1 = u1 + _dot(xp, w1_ref[s, pl.ds(D_CMP * DH + pp * 2 * DH, 2 * DH), :])
        for g in range(HKV):
            h = _compress_hidden(u0[g * NCP:(g + 1) * NCP], u1[g * NCP:(g + 1) * NCP], cb_ref[s])
            kcv[s, g] = _dot(_gelu_tanh(h).astype(BF16), w2_ref[s])

    heads = range(HKV)
    comp = [_compressed_branch(qbs[g], kcv[0, g], kcv[1, g], tpos, nc, scale) for g in heads]
    imps = []
    for g in heads:
        p_c = comp[g][0]
        psum = p_c[0:TP]
        for r in range(1, GRP):
            psum = psum + p_c[r * TP:(r + 1) * TP]
        imps.append(_dot_exact_rhs(psum, c2s_ref[...]))
    sels = [_select_blocks(imps[g], tpos_tok, ns) for g in heads]
    biases = [jnp.concatenate([((sels[g] - 1.0) * (-NEG)).astype(BF16)] * GRP, axis=0) for g in heads]
    bias_past = [_dot_nt(biases[g], koh_ref[...]) for g in heads]
    bias_new = [jnp.where(new_ok, _dot_nt(biases[g], new_oh), NEG) for g in heads]
    es = []
    for g in heads:
        s = jnp.concatenate([pc + bias_past[g] for pc in raw[g][:D_CMP]] + [raw[g][D_CMP] + bias_new[g]], axis=1)
        es.append(jnp.exp2((s - jnp.max(s, axis=-1, keepdims=True)) * c1))
    o_ss = []
    for g in heads:
        vsel_col = (3 * HKV + g) * DH
        o_ss.append(_dot(es[g][:, D_CMP * NCP:].astype(BF16),
                         _pad_rows(kvr_ref[:, vsel_col:vsel_col + DH], LANES).astype(BF16)))
    for p in range(D_CMP):
        for g in heads:
            o_ss[g] = o_ss[g] + _dot(es[g][:, p * NCP:(p + 1) * NCP].astype(BF16),
                                     gather(p, 3 * HKV + g).astype(BF16))
    outs = []
    for g in heads:
        o_s = o_ss[g] / jnp.sum(es[g], axis=-1, keepdims=True)
        gm = _sigmoid(misc_ref[:, g * LANES:g * LANES + GRP * 3])

        def gate(j):
            return jnp.concatenate([gm[:, r * 3 + j:r * 3 + j + 1] for r in range(GRP)], axis=0)

        outs.append(gate(0) * comp[g][1] + gate(1) * o_s + gate(2) * o_ws[g])

    for g in range(HKV):
        for r in range(GRP):
            c = (g * GRP + r) * DH
            o_ref[:, c:c + DH] = outs[g][r * TP:(r + 1) * TP] * _silu(z_ref[:, c:c + DH])

    wout_ref[pl.ds(0, (wb - dec_seq) * wch), :] = win_ref[pl.ds(dec_seq * wch, (wb - dec_seq) * wch), :]
    for t in range(dec_seq):
        for ch in range(wch):
            wout_ref[pl.ds((wb - dec_seq + t) * wch + ch, 1), :] = kvw_ref[t:t + 1, ch * DH:(ch + 1) * DH]


def _nsa_sample(Ps, cache, cache_win, page_table, w1, cb, w2, c2s, dec_seq):
    NB, n_pages = page_table.shape
    n_pool, prow, _ = cache.shape
    page = prow // (N_KV_SLOTS * HKV)
    wb = cache_win.shape[1] // (2 * HKV)
    past_len = n_pages * page
    L = past_len + dec_seq
    assert wb == WINDOW and (L // D_CMP) == NCP and NCP * D_CMP == past_len and dec_seq <= TP
    assert -(-L // L_SEL) < NSP and page % D_CMP == 0 and (dec_seq * 2 * HKV) % SUBLANES == 0
    qw = H_A * DH
    nch = N_KV_SLOTS * HKV
    koh = jnp.asarray(np.arange(NCP)[:, None] * D_CMP // L_SEL == np.arange(NSP)[None, :], dtype=BF16)

    grid_spec = pltpu.PrefetchScalarGridSpec(
        num_scalar_prefetch=1,
        grid=(NB,),
        in_specs=[
            pl.BlockSpec(memory_space=pl.ANY),
            pl.BlockSpec((None, wb * 2 * HKV, DH), lambda b, pt: (b, 0, 0)),
            pl.BlockSpec((TP, qw), lambda b, pt: (b, A_Q // qw)),
            pl.BlockSpec((TP, 4 * HKV * DH), lambda b, pt: (b, A_KV // (4 * HKV * DH))),
            pl.BlockSpec((TP, 2 * HKV * DH), lambda b, pt: (b, (A_KV + 4 * HKV * DH) // (2 * HKV * DH))),
            pl.BlockSpec((TP, HKV * LANES), lambda b, pt: (b, A_MISC // (HKV * LANES))),
            pl.BlockSpec((TP, qw), lambda b, pt: (b, A_ZA // qw)),
            pl.BlockSpec((2, L_CMP * DH, CMP_HID), lambda b, pt: (0, 0, 0)),
            pl.BlockSpec((2, 1, CMP_HID), lambda b, pt: (0, 0, 0)),
            pl.BlockSpec((2, CMP_HID, DH), lambda b, pt: (0, 0, 0)),
            pl.BlockSpec((NCP, NSP), lambda b, pt: (0, 0)),
            pl.BlockSpec((NCP, NSP), lambda b, pt: (0, 0))],
        out_specs=[pl.BlockSpec((TP, qw), lambda b, pt: (b, 0)),
                   pl.BlockSpec((None, wb * 2 * HKV, DH), lambda b, pt: (b, 0, 0))],
        scratch_shapes=[pltpu.VMEM((2, NCP * (D_CMP * nch + 1), DH), F32), pltpu.SemaphoreType.DMA((2,))],
    )
    return pl.pallas_call(
        functools.partial(_nsa_sample_body, n_pages=n_pages, page=page, past_len=past_len, dec_seq=dec_seq, wb=wb),
        grid_spec=grid_spec,
        out_shape=[jax.ShapeDtypeStruct((NB * TP, qw), F32),
                   jax.ShapeDtypeStruct((NB, wb * 2 * HKV, DH), F32)],
        compiler_params=_params(("arbitrary",)),
    )(page_table.reshape(-1), cache, cache_win, Ps, Ps, Ps, Ps, Ps, w1, cb, w2, c2s, koh)


def _rec_chunk(q, k, v, la, gate, S, C, last_row, nh, K, V):
    sb = min(16, C)
    ri = lax.broadcasted_iota(jnp.int32, (C, C), 0)
    cj = lax.broadcasted_iota(jnp.int32, (C, C), 1)
    causal = cj <= ri
    tri = jnp.where(causal, 1.0, 0.0).astype(BF16)
    b = _dot_exact_lhs(tri, la)
    qe = (q * jnp.exp(b)).astype(BF16)
    qis, kis = [], []
    for i in range(C // sb):
        r0 = i * sb
        ci = b[r0 + sb // 2:r0 + sb // 2 + 1, :]
        qis.append((q[r0:r0 + sb] * jnp.exp(b[r0:r0 + sb] - ci)).astype(BF16))
        kis.append((k * jnp.exp(jnp.minimum(ci - b, 80.0))).astype(BF16))
    b_last = b[last_row:last_row + 1, :]
    rowi = lax.broadcasted_iota(jnp.int32, (C, 1), 0)
    kd = jnp.where(rowi <= last_row, k * jnp.exp(jnp.minimum(b_last - b, 0.0)), 0.0)
    stack = jnp.concatenate([kd, jnp.broadcast_to(b_last, (SUBLANES, nh * K)),
                             jnp.zeros((LANES - C - SUBLANES, nh * K), F32)], axis=0)
    vb = v.astype(BF16)
    v_pad = jnp.concatenate([vb, jnp.zeros((LANES - C, nh * V), BF16)], axis=0)
    outs, s_new = [], []
    for h in range(nh):
        ks = slice(h * K, (h + 1) * K)
        vs = slice(h * V, (h + 1) * V)
        o = _dot(qe[:, ks], S[h].astype(BF16))
        rows = [_dot_nt(qi[:, ks], ki[:, ks]) for qi, ki in zip(qis, kis)]
        att = rows[0] if len(rows) == 1 else jnp.concatenate(rows, axis=0)
        att = jnp.where(causal, att, 0.0)
        o = o + _dot(att.astype(BF16), vb[:, vs])
        stack_t = stack[:, ks].T
        a_col = jnp.exp(stack_t[:, C:C + 1])
        s_new.append(a_col * S[h] + _dot(stack_t.astype(BF16), v_pad[:, vs]))
        ms = jnp.mean(o * o, axis=-1, keepdims=True)
        outs.append(o * lax.rsqrt(ms + EPS) * gate[:, vs])
    return outs, s_new


def _rec_body(*refs, variant, nh, K, V, C, TB, last_row, has_s0, layer_idx):
    refs = list(refs)
    q_ref, k_ref, v_ref, z_ref = refs[:4]
    pos = 4
    if variant == "gla":
        misc_ref, w2_ref, gb_ref = refs[pos:pos + 3]
        pos += 3
    else:
        lb_ref = refs[pos]
        pos += 1
    gn_ref = refs[pos]
    pos += 1
    if has_s0:
        s0_ref = refs[pos]
        pos += 1
    o_ref, s_ref = refs[pos:pos + 2]
    n_chunks = TB // C
    single = n_chunks == 1 and has_s0

    if not single:
        @pl.when(pl.program_id(2) == 0)
        def _():
            if has_s0:
                s_ref[...] = s0_ref[...]
            else:
                s_ref[...] = jnp.zeros(s_ref.shape, F32)

    if variant == "hgrn":
        lg = lb_ref[...]
        e = jnp.exp(lg - jnp.max(lg, axis=0, keepdims=True))
        prob = e / jnp.sum(e, axis=0, keepdims=True)
        lb = prob[1:2]
        for i in range(2, layer_idx + 1):
            lb = lb + prob[i:i + 1]
    gn_all = jnp.concatenate([gn_ref[...]] * nh, axis=1)

    def chunk(c0):
        rows = pl.ds(c0, C)
        qr = q_ref[rows, :]
        kr = k_ref[rows, :]
        v = v_ref[rows, :]
        gate = _silu(z_ref[rows, :]) * gn_all
        if variant == "gla":
            zg = _dot(misc_ref[rows, :].astype(BF16), w2_ref[...].astype(BF16)) + gb_ref[...]
            q, k, la = qr * (K ** -0.5), kr, _log_sigmoid(zg) / GLA_GATE_NORM
        else:
            t = jnp.exp(-jnp.abs(kr))
            r = 1.0 / (1.0 + t)
            tr = t * r
            nonneg = kr >= 0.0
            sig = jnp.where(nonneg, r, tr)
            nsig = jnp.where(nonneg, tr, r)
            q, k, la = _silu(qr), (1.0 - lb) * nsig, jnp.log(lb + (1.0 - lb) * sig)
        src = s0_ref if single else s_ref
        outs, s_new = _rec_chunk(q, k, v, la, gate, [src[h] for h in range(nh)], C, last_row, nh, K, V)
        for h in range(nh):
            o_ref[rows, h * V:(h + 1) * V] = outs[h].astype(o_ref.dtype)
            s_ref[h] = s_new[h]

    if n_chunks == 1:
        chunk(0)
    else:
        def body(ci, carry):
            chunk(pl.multiple_of(ci * C, C))
            return carry

        lax.fori_loop(0, n_chunks, body, 0, unroll=2)


def _recurrence(P, cols, extra, gn, s0, *, variant, B, T, H, K, V, C, TB, nh, last_row, out_dtype, layer_idx=1):
    cq, ck, cv, cz = cols
    assert H % nh == 0 and T % TB == 0 and TB % C == 0
    nt = T // TB

    def colspec(c0, w):
        assert c0 % (nh * w) == 0
        return pl.BlockSpec((TB, nh * w), lambda b, j, t: (b * nt + t, c0 // (nh * w) + j))

    in_specs = [colspec(cq, K), colspec(ck, K), colspec(cv, V), colspec(cz, V)]
    args = [P, P, P, P]
    if variant == "gla":
        w2p, gb = extra
        in_specs += [pl.BlockSpec((TB, LANES), lambda b, j, t: (b * nt + t, A_MISC // LANES)),
                     pl.BlockSpec((LANES, nh * K), lambda b, j, t: (0, j)),
                     pl.BlockSpec((1, nh * K), lambda b, j, t: (0, j))]
        args += [P, w2p, gb]
    else:
        (lb_logits,) = extra
        in_specs += [pl.BlockSpec((lb_logits.shape[0], nh * K), lambda b, j, t: (0, j))]
        args += [lb_logits]
    in_specs += [pl.BlockSpec((1, V), lambda b, j, t: (0, 0))]
    args += [gn]
    st_spec = pl.BlockSpec((None, None, nh, K, V), lambda b, j, t: (0, b, j, 0, 0))
    if s0 is not None:
        in_specs += [st_spec]
        args += [s0]
    return pl.pallas_call(
        functools.partial(_rec_body, variant=variant, nh=nh, K=K, V=V, C=C, TB=TB, last_row=last_row,
                          has_s0=s0 is not None, layer_idx=layer_idx),
        grid=(B, H // nh, nt),
        in_specs=in_specs,
        out_specs=[pl.BlockSpec((TB, nh * V), lambda b, j, t: (b * nt + t, j)), st_spec],
        out_shape=[jax.ShapeDtypeStruct((B * T, H * V), out_dtype),
                   jax.ShapeDtypeStruct((1, B, H, K, V), F32)],
        compiler_params=_params(("arbitrary", "arbitrary", "arbitrary")),
    )(*args)


def _layout_a_w_in(w):
    off = np.concatenate([[0], np.cumsum(A_SIZES)])
    q, kv, gbr, za, qb, kb, vb, lr, zb = [w[:, off[i]:off[i + 1]] for i in range(len(A_SIZES))]
    D = w.shape[0]
    z = lambda n: jnp.zeros((D, n), w.dtype)
    misc0 = jnp.concatenate([gbr[:, :GRP * 3], lr, z(LANES - GRP * 3 - GLA_LR)], axis=1)
    misc1 = jnp.concatenate([gbr[:, GRP * 3:], z(LANES - GRP * 3)], axis=1)
    out = jnp.concatenate([q, vb, zb, za, qb, kb, kv, misc0, misc1, z(NA - A_MISC - HKV * LANES)], axis=1)
    assert out.shape[1] == NA
    return out.astype(BF16)


def _cmp_to_sel():
    c_start = np.arange(NCP)[:, None] * D_CMP
    s_start = np.arange(NSP)[None, :] * L_SEL
    overlap = np.clip(np.minimum(c_start + L_CMP, s_start + L_SEL) - np.maximum(c_start, s_start), 0, None)
    return jnp.asarray(overlap / D_CMP, dtype=BF16)


def kernel(x_prompt, x_sample, cache_kv, cache_win, state_gla, state_hgrn, page_table, a_norm, a_w_in, a_gla_w2,
           a_gla_b, a_gla_gn, a_cmp_pe, a_cmp_w1, a_cmp_b1, a_cmp_w2, a_w_out, c_norm, c_w_in, c_lb_logits, c_gn,
           c_w_out, final_norm):
    B, T, D = x_prompt.shape
    NB, TS, _ = x_sample.shape
    n_pool, page = cache_kv.shape[1], cache_kv.shape[2]
    wb = cache_win.shape[2]
    assert a_norm.shape[0] == 1 and c_norm.shape[0] == 1 and c_lb_logits.shape[0] == 2

    wa = _layout_a_w_in(a_w_in[0])
    wc = c_w_in[0].astype(BF16)
    wao = a_w_out[0].astype(BF16)
    wco = c_w_out[0].astype(BF16)
    w1 = a_cmp_w1[0].astype(BF16)
    w2 = a_cmp_w2[0].astype(BF16)
    pe = a_cmp_pe[0].reshape(2, 1, L_CMP * DH)
    b1 = a_cmp_b1[0].reshape(2, 1, CMP_HID)
    w2p = jnp.zeros((LANES, H_B * DK_B), F32).at[MISC_LR:MISC_LR + GLA_LR, :].set(a_gla_w2[0])
    gb = a_gla_b[0].reshape(1, H_B * DK_B)
    c2s = _cmp_to_sel()
    a_nw = a_norm[0].reshape(1, D)
    c_nw = c_norm[0].reshape(1, D)
    f_nw = final_norm.reshape(1, D)
    gla_gn = a_gla_gn[0].reshape(1, DV_B)
    hg_gn = c_gn[0].reshape(1, DV_C)

    xp = x_prompt.reshape(B * T, D)
    xs = jnp.pad(x_sample, ((0, 0), (0, TP - TS), (0, 0))).reshape(NB * TP, D)

    cb = _cmp_bias(pe, w1, b1)

    Pp = _norm_matmul(xp, a_nw, wa, tm=1024, tn=512)
    kcv = _compress_prompt(Pp, w1, cb, w2, B, T)
    koh = jnp.asarray(np.arange(T)[:, None] // L_SEL == np.arange(NSP)[None, :], dtype=BF16)
    oa_p = _nsa_prompt(Pp, kcv, c2s, koh, B, T, tq=128, tk=512)
    ob_p, gla_p = _recurrence(Pp, (A_QB, A_KB, A_VB, A_ZB), (w2p, gb), gla_gn, None, variant="gla", B=B, T=T,
                              H=H_B, K=DK_B, V=DV_B, C=64, TB=512, nh=4, last_row=63, out_dtype=BF16)
    y1p = _out_proj(oa_p, 0, ob_p, 0, wao, xp, a_nw, tm=512, final_norm=False)

    Ps = _norm_matmul(xs, a_nw, wa, tm=NB * TP, tn=512)
    cache2 = cache_kv[0].reshape(n_pool, page * N_KV_SLOTS * HKV, DH)
    win2 = cache_win[0].reshape(NB, wb * 2 * HKV, DH)
    oa_s, win_out = _nsa_sample(Ps, cache2, win2, page_table, w1, cb, w2, c2s, TS)
    ob_s, gla_s = _recurrence(Ps, (A_QB, A_KB, A_VB, A_ZB), (w2p, gb), gla_gn, state_gla, variant="gla", B=NB, T=TP,
                              H=H_B, K=DK_B, V=DV_B, C=TP, TB=TP, nh=H_B, last_row=TS - 1, out_dtype=F32)
    y1s = _out_proj(oa_s, 0, ob_s, 0, wao, xs, a_nw, tm=min(512, NB * TP), final_norm=False)

    Pc = _norm_matmul(y1p, c_nw, wc, tm=1024, tn=512)
    hc = H_C * DK_C
    oc_p, hg_p = _recurrence(Pc, (0, hc, 2 * hc, 3 * hc), (c_lb_logits,), hg_gn, None, variant="hgrn", B=B, T=T,
                             H=H_C, K=DK_C, V=DV_C, C=64, TB=512, nh=4, last_row=63, out_dtype=BF16)
    y_prompt = _out_proj(oc_p, 0, oc_p, 1, wco, y1p, f_nw, tm=512, final_norm=True)

    Pcs = _norm_matmul(y1s, c_nw, wc, tm=NB * TP, tn=512)
    oc_s, hg_s = _recurrence(Pcs, (0, hc, 2 * hc, 3 * hc), (c_lb_logits,), hg_gn, state_hgrn, variant="hgrn", B=NB,
                             T=TP, H=H_C, K=DK_C, V=DV_C, C=TP, TB=TP, nh=H_C, last_row=TS - 1, out_dtype=F32)
    y_sample = _out_proj(oc_s, 0, oc_s, 1, wco, y1s, f_nw, tm=min(512, NB * TP), final_norm=True)

    kvw = 6 * HKV * DH
    nrw = N_KV_SLOTS * HKV * DH
    Pp3 = Pp.reshape(B, T, NA)
    Ps3 = Ps.reshape(NB, TP, NA)
    kv_p = Pp3[:, :, A_KV:A_KV + nrw].reshape(1, B, T, N_KV_SLOTS, HKV, DH)
    kv_s = Ps3[:, :TS, A_KV:A_KV + nrw].reshape(1, NB, TS, N_KV_SLOTS, HKV, DH)
    keep = min(WINDOW, T)
    win_p = Pp3[:, T - keep:, A_KV + nrw:A_KV + kvw].reshape(1, B, keep, 2, HKV, DH)
    win_s = win_out.reshape(1, NB, wb, 2, HKV, DH)
    return (y_prompt.reshape(B, T, D), y_sample.reshape(NB, TP, D)[:, :TS], kv_p, kv_s, win_p, win_s,
            gla_p, gla_s, hg_p, hg_s)
```

```python
import functools

import numpy as np
import jax
import jax.numpy as jnp
from jax import lax
from jax.experimental import pallas as pl
from jax.experimental.pallas import tpu as pltpu

F32 = jnp.float32
BF16 = jnp.bfloat16

D_MODEL = 2048
DH = 128
H_A = 8
HKV = 2
GRP = H_A // HKV
L_CMP = 32
D_CMP = 16
CMP_R = L_CMP // D_CMP
CMP_HID = 2 * DH
L_SEL = 64
N_TOP = 16
WINDOW = 512
N_KV_SLOTS = 4
H_B = 4
DK_B = 128
DV_B = 256
GLA_LR = 16
GLA_GATE_NORM = 16.0
DK_C = 128
H_C = 16
DV_C = 128
EPS = 1e-6
FORCE_SCORE = 1e9
A_SIZES = (H_A * DH, 6 * HKV * DH, 3 * H_A, H_A * DH, H_B * DK_B, H_B * DK_B, H_B * DV_B, GLA_LR, H_B * DV_B)

LANES = 128
SUBLANES = 8
VMEM_LIMIT = 56 * 1024 * 1024

A_Q = 0
A_VB = 1024
A_ZB = 2048
A_ZA = 3072
A_QB = 4096
A_KB = 4608
A_KV = 5120
A_MISC = 6656
NA = 7168
MISC_LR = GRP * 3

NCP = 128
NSP = 128
TP = 8
NEG = -1e30
LOG2E = 1.4426950408889634


def _dot(a, b):
    return jnp.dot(a, b, preferred_element_type=F32)


def _dot_nt(a, b):
    return lax.dot_general(a, b, (((1,), (1,)), ((), ())), preferred_element_type=F32)


def _split3(a):
    hi = a.astype(BF16)
    r = a - hi.astype(F32)
    mid = r.astype(BF16)
    lo = (r - mid.astype(F32)).astype(BF16)
    return hi, mid, lo


def _dot_exact_rhs(a, b_exact):
    hi, mid, lo = _split3(a)
    return _dot(hi, b_exact) + _dot(mid, b_exact) + _dot(lo, b_exact)


def _dot_exact_lhs(a_exact, b):
    hi, mid, lo = _split3(b)
    return _dot(a_exact, hi) + _dot(a_exact, mid) + _dot(a_exact, lo)


def _sigmoid(x):
    return 1.0 / (1.0 + jnp.exp(-x))


def _silu(x):
    return x * _sigmoid(x)


def _log_sigmoid(x):
    return jnp.minimum(x, 0.0) - jnp.log(1.0 + jnp.exp(-jnp.abs(x)))


def _gelu_tanh(x):
    return 0.5 * x * (1.0 + jnp.tanh(np.sqrt(2.0 / np.pi) * (x + 0.044715 * (x * x * x))))


def _masked_softmax_pre(sm):
    m = jnp.max(sm, axis=-1, keepdims=True)
    e = jnp.where(sm > 0.5 * NEG, jnp.exp(sm - m), 0.0)
    l = jnp.sum(e, axis=-1, keepdims=True)
    return e / jnp.maximum(l, 1e-30)


def _params(sem):
    return pltpu.CompilerParams(dimension_semantics=sem, vmem_limit_bytes=VMEM_LIMIT)


def _norm_matmul_body(x_ref, nw_ref, w_ref, o_ref, h_ref):
    @pl.when(pl.program_id(1) == 0)
    def _():
        x = x_ref[...]
        ms = jnp.mean(x * x, axis=-1, keepdims=True)
        h_ref[...] = (x * lax.rsqrt(ms + EPS) * nw_ref[...]).astype(BF16)

    o_ref[...] = _dot(h_ref[...], w_ref[...])


def _norm_matmul(x, nw, w, tm, tn):
    M, D = x.shape
    N = w.shape[1]
    assert M % tm == 0 and N % tn == 0
    return pl.pallas_call(
        _norm_matmul_body,
        grid=(M // tm, N // tn),
        in_specs=[pl.BlockSpec((tm, D), lambda i, j: (i, 0)),
                  pl.BlockSpec((1, D), lambda i, j: (0, 0)),
                  pl.BlockSpec((D, tn), lambda i, j: (0, j))],
        out_specs=pl.BlockSpec((tm, tn), lambda i, j: (i, j)),
        out_shape=jax.ShapeDtypeStruct((M, N), F32),
        scratch_shapes=[pltpu.VMEM((tm, D), BF16)],
        compiler_params=_params(("arbitrary", "arbitrary")),
    )(x, nw, w)


def _out_proj_body(a1_ref, a2_ref, w1_ref, w2_ref, x_ref, nw_ref, y_ref, *, final_norm):
    y = x_ref[...] + _dot(a1_ref[...].astype(BF16), w1_ref[...]) + _dot(a2_ref[...].astype(BF16), w2_ref[...])
    if final_norm:
        ms = jnp.mean(y * y, axis=-1, keepdims=True)
        y = y * lax.rsqrt(ms + EPS) * nw_ref[...]
    y_ref[...] = y


def _out_proj(a1, c1, a2, c2, w, x, nw, tm, final_norm):
    M, D = x.shape
    KH = w.shape[0] // 2
    assert M % tm == 0
    return pl.pallas_call(
        functools.partial(_out_proj_body, final_norm=final_norm),
        grid=(M // tm,),
        in_specs=[pl.BlockSpec((tm, KH), lambda i: (i, c1)),
                  pl.BlockSpec((tm, KH), lambda i: (i, c2)),
                  pl.BlockSpec((KH, D), lambda i: (0, 0)),
                  pl.BlockSpec((KH, D), lambda i: (1, 0)),
                  pl.BlockSpec((tm, D), lambda i: (i, 0)),
                  pl.BlockSpec((1, D), lambda i: (0, 0))],
        out_specs=pl.BlockSpec((tm, D), lambda i: (i, 0)),
        out_shape=jax.ShapeDtypeStruct((M, D), F32),
        compiler_params=_params(("arbitrary",)),
    )(a1, a2, w, w, x, nw)


def _cmp_bias_body(pe_ref, w1_ref, b1_ref, o_ref):
    pe = jnp.broadcast_to(pe_ref[...], (SUBLANES, pe_ref.shape[-1])).astype(BF16)
    o_ref[...] = b1_ref[...] + _dot(pe, w1_ref[...])[0:1, :]


def _cmp_bias(pe, w1, b1):
    S, KF, HID = w1.shape
    return pl.pallas_call(
        _cmp_bias_body,
        grid=(S,),
        in_specs=[pl.BlockSpec((None, 1, KF), lambda s: (s, 0, 0)),
                  pl.BlockSpec((None, KF, HID), lambda s: (s, 0, 0)),
                  pl.BlockSpec((None, 1, HID), lambda s: (s, 0, 0))],
        out_specs=pl.BlockSpec((None, 1, HID), lambda s: (s, 0, 0)),
        out_shape=jax.ShapeDtypeStruct((S, 1, HID), F32),
        compiler_params=_params(("arbitrary",)),
    )(pe, w1, b1)


def _compress_hidden(u0, u1, bias):
    return u0 + pltpu.roll(u1, NCP - 1, 0) + bias


def _compress_prompt_body(x_ref, w1_ref, cb_ref, w2_ref, o_ref):
    u0 = jnp.zeros((NCP, CMP_HID), F32)
    u1 = jnp.zeros((NCP, CMP_HID), F32)
    for pp in range(D_CMP // 2):
        xa = x_ref[pl.ds(2 * pp, NCP, stride=D_CMP), :]
        xb = x_ref[pl.ds(2 * pp + 1, NCP, stride=D_CMP), :]
        xp = jnp.concatenate([xa, xb], axis=1).astype(BF16)
        u0 = u0 + _dot(xp, w1_ref[pl.ds(pp * 2 * DH, 2 * DH), :])
        u1 = u1 + _dot(xp, w1_ref[pl.ds(D_CMP * DH + pp * 2 * DH, 2 * DH), :])
    h = _compress_hidden(u0, u1, cb_ref[...])
    o_ref[...] = _dot(_gelu_tanh(h).astype(BF16), w2_ref[...])


def _compress_prompt(P, w1, cb, w2, B, T):
    assert T // D_CMP == NCP
    kv0 = A_KV // DH
    return pl.pallas_call(
        _compress_prompt_body,
        grid=(B, 2, HKV),
        in_specs=[pl.BlockSpec((T, DH), lambda b, s, g: (b, kv0 + HKV * s + g)),
                  pl.BlockSpec((None, L_CMP * DH, CMP_HID), lambda b, s, g: (s, 0, 0)),
                  pl.BlockSpec((None, 1, CMP_HID), lambda b, s, g: (s, 0, 0)),
                  pl.BlockSpec((None, CMP_HID, DH), lambda b, s, g: (s, 0, 0))],
        out_specs=pl.BlockSpec((None, None, None, NCP, DH), lambda b, s, g: (b, s, g, 0, 0)),
        out_shape=jax.ShapeDtypeStruct((B, 2, HKV, NCP, DH), F32),
        compiler_params=_params(("arbitrary", "arbitrary", "arbitrary")),
    )(P, w1, cb, w2)


def _select_blocks(imp, tpos_tok, ns):
    blk = lax.broadcasted_iota(jnp.int32, (1, NSP), 1)
    cur = tpos_tok // L_SEL
    valid = blk <= cur
    forced = (blk == 0) | (blk == cur) | (blk == cur - 1)
    score = jnp.where(valid, jnp.where(forced, FORCE_SCORE, imp), -jnp.inf)
    k_top = float(min(N_TOP, ns))
    if score.shape[0] == LANES:
        nb = -(-ns // SUBLANES) * SUBLANES
        st = score.T[0:nb]
        rblk = lax.broadcasted_iota(jnp.int32, (nb, 1), 0)
        rank = jnp.zeros(st.shape, F32)
        for j in range(ns):
            sj = st[j:j + 1, :]
            rank = rank + jnp.where((sj > st) | ((sj == st) & (rblk > j)), 1.0, 0.0)
        top = jnp.where(rank < k_top, 1.0, 0.0)
        top = jnp.concatenate([top, jnp.zeros((NSP - nb, LANES), F32)], axis=0).T
        return jnp.where(valid, top, 0.0)
    rank = jnp.zeros(score.shape, F32)
    for j in range(ns):
        sj = score[:, j:j + 1]
        beats = (sj > score) | ((sj == score) & (blk > j))
        rank = rank + jnp.where(beats, 1.0, 0.0)
    return jnp.where(valid & (rank < float(min(N_TOP, ns))), 1.0, 0.0)


def _compressed_branch(qb, kc, vc, tpos, nc, scale):
    s = _dot_nt(qb, kc.astype(BF16)) * scale
    cidx = lax.broadcasted_iota(jnp.int32, (1, NCP), 1)
    cmask = (cidx < nc) & (cidx * D_CMP + (L_CMP - 1) <= tpos)
    p = _masked_softmax_pre(jnp.where(cmask, s, NEG))
    return p, _dot(p.astype(BF16), vc.astype(BF16))


def _keys_with_block_onehot(k_ref, koh_ref, k0, width):
    return jnp.concatenate([k_ref[pl.ds(k0, width), :].astype(BF16), koh_ref[pl.ds(k0, width), :]], axis=1)


def _nsa_prompt_body(q_ref, kc_ref, vc_ref, ks_ref, vs_ref, kw_ref, vw_ref, misc_ref, z_ref, c2s_ref, koh_ref,
                     o_ref, sc_ref, mx_ref, l_ref, acc_ref, *, tq, tk, T):
    qi = pl.program_id(2)
    t0 = pl.multiple_of(qi * tq, tq)
    R = GRP * tq
    scale = DH ** -0.5
    ns = -(-T // L_SEL)
    nc = T // D_CMP - CMP_R + 1
    n_top = min(N_TOP, ns)
    qb = jnp.concatenate([q_ref[:, r * DH:(r + 1) * DH] for r in range(GRP)], axis=0).astype(BF16)
    tpos = t0 + (lax.broadcasted_iota(jnp.int32, (R, 1), 0) & (tq - 1))
    tpos_tok = t0 + lax.broadcasted_iota(jnp.int32, (tq, 1), 0)

    c1 = scale * LOG2E
    nwt = WINDOW // tq + 1
    tiles, starts = [], []
    for i in range(nwt):
        k0 = t0 - WINDOW + i * tq
        k0c = pl.multiple_of(jnp.maximum(k0, 0), tq)
        s = _dot_nt(qb, kw_ref[pl.ds(k0c, tq), :].astype(BF16))
        kpos = k0 + lax.broadcasted_iota(jnp.int32, (1, tq), 1)
        if i == 0:
            s = jnp.where((kpos > tpos - WINDOW) & (kpos >= 0), s, NEG)
        elif i == nwt - 1:
            s = jnp.where(kpos <= tpos, s, NEG)
        else:
            s = jnp.where(k0 >= 0, s, NEG)
        tiles.append(s)
        starts.append(k0c)
    s = jnp.concatenate(tiles, axis=1)
    e = jnp.exp2((s - jnp.max(s, axis=-1, keepdims=True)) * c1)
    o_w = _dot(e[:, 0:tq].astype(BF16), vw_ref[pl.ds(starts[0], tq), :].astype(BF16))
    for i in range(1, nwt):
        o_w = o_w + _dot(e[:, i * tq:(i + 1) * tq].astype(BF16), vw_ref[pl.ds(starts[i], tq), :].astype(BF16))
    o_w = o_w / jnp.sum(e, axis=-1, keepdims=True)

    p_c, o_c = _compressed_branch(qb, kc_ref[...], vc_ref[...], tpos, nc, scale)
    psum = p_c[0:tq]
    for r in range(1, GRP):
        psum = psum + p_c[r * tq:(r + 1) * tq]
    imp = _dot_exact_rhs(psum, c2s_ref[...])

    def all_valid():
        blk = lax.broadcasted_iota(jnp.int32, (1, NSP), 1)
        return jnp.where(blk <= tpos_tok // L_SEL, 1.0, 0.0)

    sel = lax.cond(t0 + tq <= n_top * L_SEL, all_valid, lambda: _select_blocks(imp, tpos_tok, ns))

    sel_bias = (sel - 1.0) * (-NEG)
    blk = lax.broadcasted_iota(jnp.int32, (1, NSP), 1)
    early_bias = jnp.where(blk < lax.div(t0, L_SEL), sel_bias, NEG)
    q_aug = jnp.concatenate([qb, jnp.concatenate([sel_bias.astype(BF16)] * GRP, axis=0)], axis=1)
    q_early = jnp.concatenate([qb, jnp.concatenate([early_bias.astype(BF16)] * GRP, axis=0)], axis=1)
    kpos_d = t0 + lax.broadcasted_iota(jnp.int32, (1, tq), 1)
    s_diag = jnp.where(kpos_d <= tpos, _dot_nt(q_aug, _keys_with_block_onehot(ks_ref, koh_ref, t0, tq)), NEG)
    n_early = lax.div(t0 + tk - 1, tk)
    mx_ref[...] = s_diag

    def pass1(kt, carry):
        k0 = pl.multiple_of(kt * tk, tk)
        s = _dot_nt(q_early, _keys_with_block_onehot(ks_ref, koh_ref, k0, tk))
        sc_ref[kt] = s
        mx = s[:, 0:LANES]
        for c in range(1, tk // LANES):
            mx = jnp.maximum(mx, s[:, c * LANES:(c + 1) * LANES])
        mx_ref[...] = jnp.maximum(mx_ref[...], mx)
        return carry

    lax.fori_loop(0, n_early, pass1, 0)
    c1 = scale * LOG2E
    mx_ref[...] = jnp.broadcast_to(jnp.max(mx_ref[...], axis=-1, keepdims=True) * c1, (R, LANES))
    e = jnp.exp2(s_diag * c1 - mx_ref[...])
    l_ref[...] = e
    acc_ref[...] = _dot(e.astype(BF16), vs_ref[pl.ds(t0, tq), :].astype(BF16))

    def pass2(kt, carry):
        k0 = pl.multiple_of(kt * tk, tk)
        s = sc_ref[kt]
        m2 = mx_ref[...]
        es = [jnp.exp2(s[:, c * LANES:(c + 1) * LANES] * c1 - m2) for c in range(tk // LANES)]
        lsum = es[0]
        for ec in es[1:]:
            lsum = lsum + ec
        l_ref[...] = l_ref[...] + lsum
        e = jnp.concatenate(es, axis=1).astype(BF16)
        acc_ref[...] = acc_ref[...] + _dot(e, vs_ref[pl.ds(k0, tk), :].astype(BF16))
        return carry

    lax.fori_loop(0, n_early, pass2, 0)
    o_s = acc_ref[...] / jnp.sum(l_ref[...], axis=-1, keepdims=True)

    gm = _sigmoid(misc_ref[:, 0:GRP * 3])

    def gate(j):
        return jnp.concatenate([gm[:, r * 3 + j:r * 3 + j + 1] for r in range(GRP)], axis=0)

    o = gate(0) * o_c + gate(1) * o_s + gate(2) * o_w
    for r in range(GRP):
        zr = z_ref[:, r * DH:(r + 1) * DH]
        o_ref[:, r * DH:(r + 1) * DH] = (o[r * tq:(r + 1) * tq] * _silu(zr)).astype(BF16)


def _nsa_prompt(P, kcv, c2s, koh, B, T, tq, tk):
    assert T % tq == 0 and tq & (tq - 1) == 0 and tk % tq == 0 and T % tk == 0 and WINDOW % tq == 0
    nq = T // tq
    kv0 = A_KV // DH
    gw = GRP * DH
    R = GRP * tq

    def kvspec(slot):
        return pl.BlockSpec((T, DH), lambda b, g, i: (b, kv0 + HKV * slot + g))

    return pl.pallas_call(
        functools.partial(_nsa_prompt_body, tq=tq, tk=tk, T=T),
        grid=(B, HKV, nq),
        in_specs=[pl.BlockSpec((tq, gw), lambda b, g, i: (b * nq + i, A_Q // gw + g)),
                  pl.BlockSpec((None, None, None, NCP, DH), lambda b, g, i: (b, 0, g, 0, 0)),
                  pl.BlockSpec((None, None, None, NCP, DH), lambda b, g, i: (b, 1, g, 0, 0)),
                  kvspec(2), kvspec(3), kvspec(4), kvspec(5),
                  pl.BlockSpec((tq, LANES), lambda b, g, i: (b * nq + i, A_MISC // LANES + g)),
                  pl.BlockSpec((tq, gw), lambda b, g, i: (b * nq + i, A_ZA // gw + g)),
                  pl.BlockSpec((NCP, NSP), lambda b, g, i: (0, 0)),
                  pl.BlockSpec((T, NSP), lambda b, g, i: (0, 0))],
        out_specs=pl.BlockSpec((tq, gw), lambda b, g, i: (b * nq + i, g)),
        out_shape=jax.ShapeDtypeStruct((B * T, H_A * DH), BF16),
        scratch_shapes=[pltpu.VMEM((T // tk, R, tk), F32), pltpu.VMEM((R, LANES), F32),
                        pltpu.VMEM((R, LANES), F32), pltpu.VMEM((R, DH), F32)],
        compiler_params=_params(("arbitrary", "arbitrary", "arbitrary")),
    )(P, kcv, kcv, P, P, P, P, P, P, c2s, koh)


def _pad_rows(x, rows):
    return jnp.concatenate([x, jnp.zeros((rows - x.shape[0], x.shape[1]), x.dtype)], axis=0)


def _nsa_sample_body(pt_ref, cache_ref, win_ref, q_ref, kvr_ref, kvw_ref, misc_ref, z_ref, w1_ref, cb_ref, w2_ref,
                     c2s_ref, koh_ref, o_ref, wout_ref, pbuf, sem, *, n_pages, page, past_len, dec_seq, wb):
    b = pl.program_id(0)
    scale = DH ** -0.5
    c1 = scale * LOG2E
    L = past_len + dec_seq
    ns = -(-L // L_SEL)
    nc = L // D_CMP - CMP_R + 1
    R = GRP * TP
    per_page = page // D_CMP
    nch = N_KV_SLOTS * HKV
    grp_rows = D_CMP * nch
    pitch = grp_rows + 1

    def page_copies(seq, slot):
        cps = []
        for j in range(n_pages):
            pid = pt_ref[seq * n_pages + j]
            for n in range(per_page):
                cps.append(pltpu.make_async_copy(
                    cache_ref.at[pid, pl.ds(n * grp_rows, grp_rows), :],
                    pbuf.at[slot, pl.ds((j * per_page + n) * pitch, grp_rows), :], sem.at[slot]))
        return cps

    slot = lax.rem(b, 2)

    @pl.when(b == 0)
    def _():
        for cp in page_copies(0, 0):
            cp.start()

    for cp in page_copies(b, slot):
        cp.wait()

    last = pl.num_programs(0) - 1
    next_copies = page_copies(jnp.minimum(b + 1, last), 1 - slot)
    copies_per_group = len(next_copies) // (2 * (D_CMP // 2))

    def gather(p, ch):
        return pbuf[slot, pl.ds(p * nch + ch, NCP, stride=pitch), :]

    trow = lax.broadcasted_iota(jnp.int32, (R, 1), 0) & (TP - 1)
    tpos = past_len + trow
    tpos_tok = past_len + lax.broadcasted_iota(jnp.int32, (TP, 1), 0)
    lane = lax.broadcasted_iota(jnp.int32, (1, LANES), 1)
    new_pos = past_len + lane
    new_ok = (new_pos <= tpos) & (new_pos < L)
    new_blk = (past_len + lax.broadcasted_iota(jnp.int32, (LANES, 1), 0)) // L_SEL
    new_oh = jnp.where(lax.broadcasted_iota(jnp.int32, (LANES, NSP), 1) == new_blk, 1.0, 0.0).astype(BF16)
    wpos = (past_len - wb) + lax.broadcasted_iota(jnp.int32, (1, wb), 1)
    wch = 2 * HKV

    qbs, o_ws, raw = [], [], []
    for g in range(HKV):
        qb = jnp.concatenate([q_ref[:, (g * GRP + r) * DH:(g * GRP + r + 1) * DH] for r in range(GRP)],
                             axis=0).astype(BF16)
        kw_col = g * DH
        vw_col = (HKV + g) * DH
        s_w = _dot_nt(qb, win_ref[pl.ds(g, wb, stride=wch), :].astype(BF16))
        s_w = jnp.where((wpos <= tpos) & (wpos > tpos - WINDOW), s_w, NEG)
        s_n = _dot_nt(qb, _pad_rows(kvw_ref[:, kw_col:kw_col + DH], LANES).astype(BF16))
        s_n = jnp.where(new_ok & (new_pos > tpos - WINDOW), s_n, NEG)
        s = jnp.concatenate([s_w, s_n], axis=1)
        e = jnp.exp2((s - jnp.max(s, axis=-1, keepdims=True)) * c1)
        o_w = (_dot(e[:, :wb].astype(BF16), win_ref[pl.ds(HKV + g, wb, stride=wch), :].astype(BF16))
               + _dot(e[:, wb:].astype(BF16), _pad_rows(kvw_ref[:, vw_col:vw_col + DH], LANES).astype(BF16)))
        o_ws.append(o_w / jnp.sum(e, axis=-1, keepdims=True))
        ksel_col = (2 * HKV + g) * DH
        pieces = [_dot_nt(qb, gather(p, 2 * HKV + g).astype(BF16)) for p in range(D_CMP)]
        pieces.append(_dot_nt(qb, _pad_rows(kvr_ref[:, ksel_col:ksel_col + DH], LANES).astype(BF16)))
        raw.append(pieces)
        qbs.append(qb)

    kcv = {}
    for s in range(2):
        u0 = jnp.zeros((HKV * NCP, CMP_HID), F32)
        u1 = jnp.zeros((HKV * NCP, CMP_HID), F32)
        for pp in range(D_CMP // 2):
            xs = []
            for g in range(HKV):
                ch = s * HKV + g
                xs.append(jnp.concatenate([gather(2 * pp, ch), gather(2 * pp + 1, ch)], axis=1))
            xp = jnp.concatenate(xs, axis=0).astype(BF16)
            u0 = u0 + _dot(xp, w1_ref[s, pl.ds(pp * 2 * DH, 2 * DH), :])
            u1 = u1 + _dot(xp, w1_ref[s, pl.ds(D_CMP * DH + pp * 2 * DH, 2 * DH), :])
            grp = s * (D_CMP // 2) + pp
            for cp in next_copies[grp * copies_per_group:(grp + 1) * copies_per_group]:
                cp.start()
        for g in range(HKV):
            h = _compress_hidden(u0[g * NCP:(g + 1) * NCP], u1[g * NCP:(g + 1) * NCP], cb_ref[s])
            kcv[s, g] = _dot(_gelu_tanh(h).astype(BF16), w2_ref[s])

    heads = range(HKV)
    comp = [_compressed_branch(qbs[g], kcv[0, g], kcv[1, g], tpos, nc, scale) for g in heads]
    imps = []
    for g in heads:
        p_c = comp[g][0]
        psum = p_c[0:TP]
        for r in range(1, GRP):
            psum = psum + p_c[r * TP:(r + 1) * TP]
        imps.append(_dot_exact_rhs(psum, c2s_ref[...]))
    sels = [_select_blocks(imps[g], tpos_tok, ns) for g in heads]
    biases = [jnp.concatenate([((sels[g] - 1.0) * (-NEG)).astype(BF16)] * GRP, axis=0) for g in heads]
    bias_past = [_dot_nt(biases[g], koh_ref[...]) for g in heads]
    bias_new = [jnp.where(new_ok, _dot_nt(biases[g], new_oh), NEG) for g in heads]
    es = []
    for g in heads:
        s = jnp.concatenate([pc + bias_past[g] for pc in raw[g][:D_CMP]] + [raw[g][D_CMP] + bias_new[g]], axis=1)
        es.append(jnp.exp2((s - jnp.max(s, axis=-1, keepdims=True)) * c1))
    o_ss = []
    for g in heads:
        vsel_col = (3 * HKV + g) * DH
        o_ss.append(_dot(es[g][:, D_CMP * NCP:].astype(BF16),
                         _pad_rows(kvr_ref[:, vsel_col:vsel_col + DH], LANES).astype(BF16)))
    for p in range(D_CMP):
        for g in heads:
            o_ss[g] = o_ss[g] + _dot(es[g][:, p * NCP:(p + 1) * NCP].astype(BF16),
                                     gather(p, 3 * HKV + g).astype(BF16))
    outs = []
    for g in heads:
        o_s = o_ss[g] / jnp.sum(es[g], axis=-1, keepdims=True)
        gm = _sigmoid(misc_ref[:, g * LANES:g * LANES + GRP * 3])

        def gate(j):
            return jnp.concatenate([gm[:, r * 3 + j:r * 3 + j + 1] for r in range(GRP)], axis=0)

        outs.append(gate(0) * comp[g][1] + gate(1) * o_s + gate(2) * o_ws[g])

    for g in range(HKV):
        for r in range(GRP):
            c = (g * GRP + r) * DH
            o_ref[:, c:c + DH] = outs[g][r * TP:(r + 1) * TP] * _silu(z_ref[:, c:c + DH])

    wout_ref[pl.ds(0, (wb - dec_seq) * wch), :] = win_ref[pl.ds(dec_seq * wch, (wb - dec_seq) * wch), :]
    for t in range(dec_seq):
        for ch in range(wch):
            wout_ref[pl.ds((wb - dec_seq + t) * wch + ch, 1), :] = kvw_ref[t:t + 1, ch * DH:(ch + 1) * DH]

    @pl.when(b == last)
    def _():
        for cp in next_copies:
            cp.wait()


def _nsa_sample(Ps, cache, cache_win, page_table, w1, cb, w2, c2s, dec_seq):
    NB, n_pages = page_table.shape
    n_pool, prow, _ = cache.shape
    page = prow // (N_KV_SLOTS * HKV)
    wb = cache_win.shape[1] // (2 * HKV)
    past_len = n_pages * page
    L = past_len + dec_seq
    assert wb == WINDOW and (L // D_CMP) == NCP and NCP * D_CMP == past_len and dec_seq <= TP
    assert -(-L // L_SEL) < NSP and page % D_CMP == 0 and (dec_seq * 2 * HKV) % SUBLANES == 0
    qw = H_A * DH
    nch = N_KV_SLOTS * HKV
    koh = jnp.asarray(np.arange(NCP)[:, None] * D_CMP // L_SEL == np.arange(NSP)[None, :], dtype=BF16)

    grid_spec = pltpu.PrefetchScalarGridSpec(
        num_scalar_prefetch=1,
        grid=(NB,),
        in_specs=[
            pl.BlockSpec(memory_space=pl.ANY),
            pl.BlockSpec((None, wb * 2 * HKV, DH), lambda b, pt: (b, 0, 0)),
            pl.BlockSpec((TP, qw), lambda b, pt: (b, A_Q // qw)),
            pl.BlockSpec((TP, 4 * HKV * DH), lambda b, pt: (b, A_KV // (4 * HKV * DH))),
            pl.BlockSpec((TP, 2 * HKV * DH), lambda b, pt: (b, (A_KV + 4 * HKV * DH) // (2 * HKV * DH))),
            pl.BlockSpec((TP, HKV * LANES), lambda b, pt: (b, A_MISC // (HKV * LANES))),
            pl.BlockSpec((TP, qw), lambda b, pt: (b, A_ZA // qw)),
            pl.BlockSpec((2, L_CMP * DH, CMP_HID), lambda b, pt: (0, 0, 0)),
            pl.BlockSpec((2, 1, CMP_HID), lambda b, pt: (0, 0, 0)),
            pl.BlockSpec((2, CMP_HID, DH), lambda b, pt: (0, 0, 0)),
            pl.BlockSpec((NCP, NSP), lambda b, pt: (0, 0)),
            pl.BlockSpec((NCP, NSP), lambda b, pt: (0, 0))],
        out_specs=[pl.BlockSpec((TP, qw), lambda b, pt: (b, 0)),
                   pl.BlockSpec((None, wb * 2 * HKV, DH), lambda b, pt: (b, 0, 0))],
        scratch_shapes=[pltpu.VMEM((2, NCP * (D_CMP * nch + 1), DH), F32), pltpu.SemaphoreType.DMA((2,))],
    )
    return pl.pallas_call(
        functools.partial(_nsa_sample_body, n_pages=n_pages, page=page, past_len=past_len, dec_seq=dec_seq, wb=wb),
        grid_spec=grid_spec,
        out_shape=[jax.ShapeDtypeStruct((NB * TP, qw), F32),
                   jax.ShapeDtypeStruct((NB, wb * 2 * HKV, DH), F32)],
        compiler_params=_params(("arbitrary",)),
    )(page_table.reshape(-1), cache, cache_win, Ps, Ps, Ps, Ps, Ps, w1, cb, w2, c2s, koh)


def _rec_chunk(q, k, v, la, gate, S, C, last_row, nh, K, V):
    sb = min(16, C)
    ri = lax.broadcasted_iota(jnp.int32, (C, C), 0)
    cj = lax.broadcasted_iota(jnp.int32, (C, C), 1)
    causal = cj <= ri
    tri = jnp.where(causal, 1.0, 0.0).astype(BF16)
    b = _dot_exact_lhs(tri, la)
    qe = (q * jnp.exp(b)).astype(BF16)
    qis, kis = [], []
    k_ref_prev, c_prev = None, None
    for i in range(C // sb):
        r0 = i * sb
        ci = b[r0 + sb // 2:r0 + sb // 2 + 1, :]
        qis.append((q[r0:r0 + sb] * jnp.exp(b[r0:r0 + sb] - ci)).astype(BF16))
        k_new = k[r0:r0 + sb] * jnp.exp(jnp.minimum(ci - b[r0:r0 + sb], 80.0))
        k_scaled = k_new if i == 0 else jnp.concatenate([k_ref_prev * jnp.exp(ci - c_prev), k_new], axis=0)
        k_ref_prev, c_prev = k_scaled, ci
        rest = C - r0 - sb
        kis.append((k_scaled if rest == 0 else
                    jnp.concatenate([k_scaled, jnp.zeros((rest, k.shape[1]), F32)], axis=0)).astype(BF16))
    b_last = b[last_row:last_row + 1, :]
    rowi = lax.broadcasted_iota(jnp.int32, (C, 1), 0)
    kd = jnp.where(rowi <= last_row, k * jnp.exp(jnp.minimum(b_last - b, 0.0)), 0.0)
    stack = jnp.concatenate([kd, jnp.broadcast_to(b_last, (SUBLANES, nh * K)),
                             jnp.zeros((LANES - C - SUBLANES, nh * K), F32)], axis=0)
    vb = v.astype(BF16)
    v_pad = jnp.concatenate([vb, jnp.zeros((LANES - C, nh * V), BF16)], axis=0)
    outs, s_new = [], []
    for h in range(nh):
        ks = slice(h * K, (h + 1) * K)
        vs = slice(h * V, (h + 1) * V)
        o = _dot(qe[:, ks], S[h].astype(BF16))
        rows = [_dot_nt(qi[:, ks], ki[:, ks]) for qi, ki in zip(qis, kis)]
        att = rows[0] if len(rows) == 1 else jnp.concatenate(rows, axis=0)
        att = jnp.where(causal, att, 0.0)
        o = o + _dot(att.astype(BF16), vb[:, vs])
        stack_t = stack[:, ks].T
        a_col = jnp.exp(stack_t[:, C:C + 1])
        s_new.append(a_col * S[h] + _dot(stack_t.astype(BF16), v_pad[:, vs]))
        ms = jnp.mean(o * o, axis=-1, keepdims=True)
        outs.append(o * lax.rsqrt(ms + EPS) * gate[:, vs])
    return outs, s_new


def _rec_body(*refs, variant, nh, K, V, C, TB, last_row, has_s0, layer_idx):
    refs = list(refs)
    q_ref, k_ref, v_ref, z_ref = refs[:4]
    pos = 4
    if variant == "gla":
        misc_ref, w2_ref, gb_ref = refs[pos:pos + 3]
        pos += 3
    else:
        lb_ref = refs[pos]
        pos += 1
    gn_ref = refs[pos]
    pos += 1
    if has_s0:
        s0_ref = refs[pos]
        pos += 1
    o_ref, s_ref = refs[pos:pos + 2]
    n_chunks = TB // C
    single = n_chunks == 1 and has_s0

    if not single:
        @pl.when(pl.program_id(2) == 0)
        def _():
            if has_s0:
                s_ref[...] = s0_ref[...]
            else:
                s_ref[...] = jnp.zeros(s_ref.shape, F32)

    if variant == "hgrn":
        lg = lb_ref[...]
        e = jnp.exp(lg - jnp.max(lg, axis=0, keepdims=True))
        prob = e / jnp.sum(e, axis=0, keepdims=True)
        lb = prob[1:2]
        for i in range(2, layer_idx + 1):
            lb = lb + prob[i:i + 1]
    gn_all = jnp.concatenate([gn_ref[...]] * nh, axis=1)

    def chunk(c0):
        rows = pl.ds(c0, C)
        qr = q_ref[rows, :]
        kr = k_ref[rows, :]
        v = v_ref[rows, :]
        gate = _silu(z_ref[rows, :]) * gn_all
        if variant == "gla":
            zg = _dot(misc_ref[rows, :].astype(BF16), w2_ref[...].astype(BF16)) + gb_ref[...]
            q, k, la = qr * (K ** -0.5), kr, _log_sigmoid(zg) / GLA_GATE_NORM
        else:
            t = jnp.exp(-jnp.abs(kr))
            r = 1.0 / (1.0 + t)
            tr = t * r
            nonneg = kr >= 0.0
            sig = jnp.where(nonneg, r, tr)
            nsig = jnp.where(nonneg, tr, r)
            q, k, la = _silu(qr), (1.0 - lb) * nsig, jnp.log(lb + (1.0 - lb) * sig)
        src = s0_ref if single else s_ref
        outs, s_new = _rec_chunk(q, k, v, la, gate, [src[h] for h in range(nh)], C, last_row, nh, K, V)
        for h in range(nh):
            o_ref[rows, h * V:(h + 1) * V] = outs[h].astype(o_ref.dtype)
            s_ref[h] = s_new[h]

    if n_chunks == 1:
        chunk(0)
    else:
        def body(ci, carry):
            chunk(pl.multiple_of(ci * C, C))
            return carry

        lax.fori_loop(0, n_chunks, body, 0, unroll=4)


def _recurrence(P, cols, extra, gn, s0, *, variant, B, T, H, K, V, C, TB, nh, last_row, out_dtype, layer_idx=1):
    cq, ck, cv, cz = cols
    assert H % nh == 0 and T % TB == 0 and TB % C == 0
    nt = T // TB

    def colspec(c0, w):
        assert c0 % (nh * w) == 0
        return pl.BlockSpec((TB, nh * w), lambda b, j, t: (b * nt + t, c0 // (nh * w) + j))

    in_specs = [colspec(cq, K), colspec(ck, K), colspec(cv, V), colspec(cz, V)]
    args = [P, P, P, P]
    if variant == "gla":
        w2p, gb = extra
        in_specs += [pl.BlockSpec((TB, LANES), lambda b, j, t: (b * nt + t, A_MISC // LANES)),
                     pl.BlockSpec((LANES, nh * K), lambda b, j, t: (0, j)),
                     pl.BlockSpec((1, nh * K), lambda b, j, t: (0, j))]
        args += [P, w2p, gb]
    else:
        (lb_logits,) = extra
        in_specs += [pl.BlockSpec((lb_logits.shape[0], nh * K), lambda b, j, t: (0, j))]
        args += [lb_logits]
    in_specs += [pl.BlockSpec((1, V), lambda b, j, t: (0, 0))]
    args += [gn]
    st_spec = pl.BlockSpec((None, None, nh, K, V), lambda b, j, t: (0, b, j, 0, 0))
    if s0 is not None:
        in_specs += [st_spec]
        args += [s0]
    return pl.pallas_call(
        functools.partial(_rec_body, variant=variant, nh=nh, K=K, V=V, C=C, TB=TB, last_row=last_row,
                          has_s0=s0 is not None, layer_idx=layer_idx),
        grid=(B, H // nh, nt),
        in_specs=in_specs,
        out_specs=[pl.BlockSpec((TB, nh * V), lambda b, j, t: (b * nt + t, j)), st_spec],
        out_shape=[jax.ShapeDtypeStruct((B * T, H * V), out_dtype),
                   jax.ShapeDtypeStruct((1, B, H, K, V), F32)],
        compiler_params=_params(("arbitrary", "arbitrary", "arbitrary")),
    )(*args)


def _layout_a_w_in(w):
    off = np.concatenate([[0], np.cumsum(A_SIZES)])
    q, kv, gbr, za, qb, kb, vb, lr, zb = [w[:, off[i]:off[i + 1]] for i in range(len(A_SIZES))]
    D = w.shape[0]
    z = lambda n: jnp.zeros((D, n), w.dtype)
    misc0 = jnp.concatenate([gbr[:, :GRP * 3], lr, z(LANES - GRP * 3 - GLA_LR)], axis=1)
    misc1 = jnp.concatenate([gbr[:, GRP * 3:], z(LANES - GRP * 3)], axis=1)
    out = jnp.concatenate([q, vb, zb, za, qb, kb, kv, misc0, misc1, z(NA - A_MISC - HKV * LANES)], axis=1)
    assert out.shape[1] == NA
    return out.astype(BF16)


def _cmp_to_sel():
    c_start = np.arange(NCP)[:, None] * D_CMP
    s_start = np.arange(NSP)[None, :] * L_SEL
    overlap = np.clip(np.minimum(c_start + L_CMP, s_start + L_SEL) - np.maximum(c_start, s_start), 0, None)
    return jnp.asarray(overlap / D_CMP, dtype=BF16)


def kernel(x_prompt, x_sample, cache_kv, cache_win, state_gla, state_hgrn, page_table, a_norm, a_w_in, a_gla_w2,
           a_gla_b, a_gla_gn, a_cmp_pe, a_cmp_w1, a_cmp_b1, a_cmp_w2, a_w_out, c_norm, c_w_in, c_lb_logits, c_gn,
           c_w_out, final_norm):
    B, T, D = x_prompt.shape
    NB, TS, _ = x_sample.shape
    n_pool, page = cache_kv.shape[1], cache_kv.shape[2]
    wb = cache_win.shape[2]
    assert a_norm.shape[0] == 1 and c_norm.shape[0] == 1 and c_lb_logits.shape[0] == 2

    wa = _layout_a_w_in(a_w_in[0])
    wc = c_w_in[0].astype(BF16)
    wao = a_w_out[0].astype(BF16)
    wco = c_w_out[0].astype(BF16)
    w1 = a_cmp_w1[0].astype(BF16)
    w2 = a_cmp_w2[0].astype(BF16)
    pe = a_cmp_pe[0].reshape(2, 1, L_CMP * DH)
    b1 = a_cmp_b1[0].reshape(2, 1, CMP_HID)
    w2p = jnp.zeros((LANES, H_B * DK_B), F32).at[MISC_LR:MISC_LR + GLA_LR, :].set(a_gla_w2[0])
    gb = a_gla_b[0].reshape(1, H_B * DK_B)
    c2s = _cmp_to_sel()
    a_nw = a_norm[0].reshape(1, D)
    c_nw = c_norm[0].reshape(1, D)
    f_nw = final_norm.reshape(1, D)
    gla_gn = a_gla_gn[0].reshape(1, DV_B)
    hg_gn = c_gn[0].reshape(1, DV_C)

    xp = x_prompt.reshape(B * T, D)
    xs = jnp.pad(x_sample, ((0, 0), (0, TP - TS), (0, 0))).reshape(NB * TP, D)

    cb = _cmp_bias(pe, w1, b1)

    Pp = _norm_matmul(xp, a_nw, wa, tm=1024, tn=1024)
    kcv = _compress_prompt(Pp, w1, cb, w2, B, T)
    koh = jnp.asarray(np.arange(T)[:, None] // L_SEL == np.arange(NSP)[None, :], dtype=BF16)
    oa_p = _nsa_prompt(Pp, kcv, c2s, koh, B, T, tq=128, tk=512)
    ob_p, gla_p = _recurrence(Pp, (A_QB, A_KB, A_VB, A_ZB), (w2p, gb), gla_gn, None, variant="gla", B=B, T=T,
                              H=H_B, K=DK_B, V=DV_B, C=64, TB=512, nh=4, last_row=63, out_dtype=BF16)
    y1p = _out_proj(oa_p, 0, ob_p, 0, wao, xp, a_nw, tm=512, final_norm=False)

    Ps = _norm_matmul(xs, a_nw, wa, tm=NB * TP, tn=1024)
    cache2 = cache_kv[0].reshape(n_pool, page * N_KV_SLOTS * HKV, DH)
    win2 = cache_win[0].reshape(NB, wb * 2 * HKV, DH)
    oa_s, win_out = _nsa_sample(Ps, cache2, win2, page_table, w1, cb, w2, c2s, TS)
    ob_s, gla_s = _recurrence(Ps, (A_QB, A_KB, A_VB, A_ZB), (w2p, gb), gla_gn, state_gla, variant="gla", B=NB, T=TP,
                              H=H_B, K=DK_B, V=DV_B, C=TP, TB=TP, nh=H_B, last_row=TS - 1, out_dtype=F32)
    y1s = _out_proj(oa_s, 0, ob_s, 0, wao, xs, a_nw, tm=min(512, NB * TP), final_norm=False)

    Pc = _norm_matmul(y1p, c_nw, wc, tm=1024, tn=1024)
    hc = H_C * DK_C
    oc_p, hg_p = _recurrence(Pc, (0, hc, 2 * hc, 3 * hc), (c_lb_logits,), hg_gn, None, variant="hgrn", B=B, T=T,
                             H=H_C, K=DK_C, V=DV_C, C=64, TB=512, nh=4, last_row=63, out_dtype=BF16)
    y_prompt = _out_proj(oc_p, 0, oc_p, 1, wco, y1p, f_nw, tm=512, final_norm=True)

    Pcs = _norm_matmul(y1s, c_nw, wc, tm=NB * TP, tn=1024)
    oc_s, hg_s = _recurrence(Pcs, (0, hc, 2 * hc, 3 * hc), (c_lb_logits,), hg_gn, state_hgrn, variant="hgrn", B=NB,
                             T=TP, H=H_C, K=DK_C, V=DV_C, C=TP, TB=TP, nh=H_C, last_row=TS - 1, out_dtype=F32)
    y_sample = _out_proj(oc_s, 0, oc_s, 1, wco, y1s, f_nw, tm=min(512, NB * TP), final_norm=True)

    kvw = 6 * HKV * DH
    nrw = N_KV_SLOTS * HKV * DH
    Pp3 = Pp.reshape(B, T, NA)
    Ps3 = Ps.reshape(NB, TP, NA)
    kv_p = Pp3[:, :, A_KV:A_KV + nrw].reshape(1, B, T, N_KV_SLOTS, HKV, DH)
    kv_s = Ps3[:, :TS, A_KV:A_KV + nrw].reshape(1, NB, TS, N_KV_SLOTS, HKV, DH)
    keep = min(WINDOW, T)
    win_p = Pp3[:, T - keep:, A_KV + nrw:A_KV + kvw].reshape(1, B, keep, 2, HKV, DH)
    win_s = win_out.reshape(1, NB, wb, 2, HKV, DH)
    return (y_prompt.reshape(B, T, D), y_sample.reshape(NB, TP, D)[:, :TS], kv_p, kv_s, win_p, win_s,
            gla_p, gla_s, hg_p, hg_s)
```

```python
import functools

import numpy as np
import jax
import jax.numpy as jnp
from jax import lax
from jax.experimental import pallas as pl
from jax.experimental.pallas import tpu as pltpu

F32 = jnp.float32
BF16 = jnp.bfloat16

D_MODEL = 2048
DH = 128
H_A = 8
HKV = 2
GRP = H_A // HKV
L_CMP = 32
D_CMP = 16
CMP_R = L_CMP // D_CMP
CMP_HID = 2 * DH
L_SEL = 64
N_TOP = 16
WINDOW = 512
N_KV_SLOTS = 4
H_B = 4
DK_B = 128
DV_B = 256
GLA_LR = 16
GLA_GATE_NORM = 16.0
DK_C = 128
H_C = 16
DV_C = 128
EPS = 1e-6
FORCE_SCORE = 1e9
A_SIZES = (H_A * DH, 6 * HKV * DH, 3 * H_A, H_A * DH, H_B * DK_B, H_B * DK_B, H_B * DV_B, GLA_LR, H_B * DV_B)

LANES = 128
SUBLANES = 8
VMEM_LIMIT = 56 * 1024 * 1024

A_Q = 0
A_VB = 1024
A_ZB = 2048
A_ZA = 3072
A_QB = 4096
A_KB = 4608
A_KV = 5120
A_MISC = 6656
NA = 7168
MISC_LR = GRP * 3

NCP = 128
NSP = 128
TP = 8
NEG = -1e30
LOG2E = 1.4426950408889634


def _dot(a, b):
    return jnp.dot(a, b, preferred_element_type=F32)


def _dot_nt(a, b):
    return lax.dot_general(a, b, (((1,), (1,)), ((), ())), preferred_element_type=F32)


def _split3(a):
    hi = a.astype(BF16)
    r = a - hi.astype(F32)
    mid = r.astype(BF16)
    lo = (r - mid.astype(F32)).astype(BF16)
    return hi, mid, lo


def _dot_exact_rhs(a, b_exact):
    hi, mid, lo = _split3(a)
    return _dot(hi, b_exact) + _dot(mid, b_exact) + _dot(lo, b_exact)


def _dot_exact_lhs(a_exact, b):
    hi, mid, lo = _split3(b)
    return _dot(a_exact, hi) + _dot(a_exact, mid) + _dot(a_exact, lo)


def _sigmoid(x):
    return 1.0 / (1.0 + jnp.exp(-x))


def _silu(x):
    h = 0.5 * x
    return h + h * jnp.tanh(h)


def _log_sigmoid(x):
    return jnp.minimum(x, 0.0) - jnp.log(1.0 + jnp.exp(-jnp.abs(x)))


def _gelu_tanh(x):
    return 0.5 * x * (1.0 + jnp.tanh(np.sqrt(2.0 / np.pi) * (x + 0.044715 * (x * x * x))))


def _masked_softmax_pre(sm):
    m = jnp.max(sm, axis=-1, keepdims=True)
    e = jnp.where(sm > 0.5 * NEG, jnp.exp(sm - m), 0.0)
    l = jnp.sum(e, axis=-1, keepdims=True)
    return e / jnp.maximum(l, 1e-30)


def _params(sem):
    return pltpu.CompilerParams(dimension_semantics=sem, vmem_limit_bytes=VMEM_LIMIT)


def _norm_matmul_body(x_ref, nw_ref, w_ref, o_ref, h_ref):
    @pl.when(pl.program_id(1) == 0)
    def _():
        x = x_ref[...]
        ms = jnp.mean(x * x, axis=-1, keepdims=True)
        h_ref[...] = (x * lax.rsqrt(ms + EPS) * nw_ref[...]).astype(BF16)

    o_ref[...] = _dot(h_ref[...], w_ref[...])


def _norm_matmul(x, nw, w, tm, tn):
    M, D = x.shape
    N = w.shape[1]
    assert M % tm == 0 and N % tn == 0
    return pl.pallas_call(
        _norm_matmul_body,
        grid=(M // tm, N // tn),
        in_specs=[pl.BlockSpec((tm, D), lambda i, j: (i, 0)),
                  pl.BlockSpec((1, D), lambda i, j: (0, 0)),
                  pl.BlockSpec((D, tn), lambda i, j: (0, j))],
        out_specs=pl.BlockSpec((tm, tn), lambda i, j: (i, j)),
        out_shape=jax.ShapeDtypeStruct((M, N), F32),
        scratch_shapes=[pltpu.VMEM((tm, D), BF16)],
        compiler_params=_params(("arbitrary", "arbitrary")),
    )(x, nw, w)


def _out_proj_body(a1_ref, a2_ref, w1_ref, w2_ref, x_ref, nw_ref, y_ref, *, final_norm):
    y = x_ref[...] + _dot(a1_ref[...].astype(BF16), w1_ref[...]) + _dot(a2_ref[...].astype(BF16), w2_ref[...])
    if final_norm:
        ms = jnp.mean(y * y, axis=-1, keepdims=True)
        y = y * lax.rsqrt(ms + EPS) * nw_ref[...]
    y_ref[...] = y


def _out_proj(a1, c1, a2, c2, w, x, nw, tm, final_norm):
    M, D = x.shape
    KH = w.shape[0] // 2
    assert M % tm == 0
    return pl.pallas_call(
        functools.partial(_out_proj_body, final_norm=final_norm),
        grid=(M // tm,),
        in_specs=[pl.BlockSpec((tm, KH), lambda i: (i, c1)),
                  pl.BlockSpec((tm, KH), lambda i: (i, c2)),
                  pl.BlockSpec((KH, D), lambda i: (0, 0)),
                  pl.BlockSpec((KH, D), lambda i: (1, 0)),
                  pl.BlockSpec((tm, D), lambda i: (i, 0)),
                  pl.BlockSpec((1, D), lambda i: (0, 0))],
        out_specs=pl.BlockSpec((tm, D), lambda i: (i, 0)),
        out_shape=jax.ShapeDtypeStruct((M, D), F32),
        compiler_params=_params(("arbitrary",)),
    )(a1, a2, w, w, x, nw)


def _cmp_bias_body(pe_ref, w1_ref, b1_ref, o_ref):
    pe = jnp.broadcast_to(pe_ref[...], (SUBLANES, pe_ref.shape[-1])).astype(BF16)
    o_ref[...] = b1_ref[...] + _dot(pe, w1_ref[...])[0:1, :]


def _cmp_bias(pe, w1, b1):
    S, KF, HID = w1.shape
    return pl.pallas_call(
        _cmp_bias_body,
        grid=(S,),
        in_specs=[pl.BlockSpec((None, 1, KF), lambda s: (s, 0, 0)),
                  pl.BlockSpec((None, KF, HID), lambda s: (s, 0, 0)),
                  pl.BlockSpec((None, 1, HID), lambda s: (s, 0, 0))],
        out_specs=pl.BlockSpec((None, 1, HID), lambda s: (s, 0, 0)),
        out_shape=jax.ShapeDtypeStruct((S, 1, HID), F32),
        compiler_params=_params(("arbitrary",)),
    )(pe, w1, b1)


def _compress_hidden(u0, u1, bias):
    return u0 + pltpu.roll(u1, NCP - 1, 0) + bias


def _compress_prompt_body(x_ref, w1_ref, cb_ref, w2_ref, o_ref):
    u0 = jnp.zeros((NCP, CMP_HID), F32)
    u1 = jnp.zeros((NCP, CMP_HID), F32)
    for pp in range(D_CMP // 2):
        xa = x_ref[pl.ds(2 * pp, NCP, stride=D_CMP), :]
        xb = x_ref[pl.ds(2 * pp + 1, NCP, stride=D_CMP), :]
        xp = jnp.concatenate([xa, xb], axis=1).astype(BF16)
        u0 = u0 + _dot(xp, w1_ref[pl.ds(pp * 2 * DH, 2 * DH), :])
        u1 = u1 + _dot(xp, w1_ref[pl.ds(D_CMP * DH + pp * 2 * DH, 2 * DH), :])
    h = _compress_hidden(u0, u1, cb_ref[...])
    o_ref[...] = _dot(_gelu_tanh(h).astype(BF16), w2_ref[...])


def _compress_prompt(P, w1, cb, w2, B, T):
    assert T // D_CMP == NCP
    kv0 = A_KV // DH
    return pl.pallas_call(
        _compress_prompt_body,
        grid=(B, 2, HKV),
        in_specs=[pl.BlockSpec((T, DH), lambda b, s, g: (b, kv0 + HKV * s + g)),
                  pl.BlockSpec((None, L_CMP * DH, CMP_HID), lambda b, s, g: (s, 0, 0)),
                  pl.BlockSpec((None, 1, CMP_HID), lambda b, s, g: (s, 0, 0)),
                  pl.BlockSpec((None, CMP_HID, DH), lambda b, s, g: (s, 0, 0))],
        out_specs=pl.BlockSpec((None, None, None, NCP, DH), lambda b, s, g: (b, s, g, 0, 0)),
        out_shape=jax.ShapeDtypeStruct((B, 2, HKV, NCP, DH), F32),
        compiler_params=_params(("arbitrary", "arbitrary", "arbitrary")),
    )(P, w1, cb, w2)


def _select_blocks(imp, tpos_tok, ns):
    blk = lax.broadcasted_iota(jnp.int32, (1, NSP), 1)
    cur = tpos_tok // L_SEL
    valid = blk <= cur
    forced = (blk == 0) | (blk == cur) | (blk == cur - 1)
    score = jnp.where(valid, jnp.where(forced, FORCE_SCORE, imp), -jnp.inf)
    k_top = float(min(N_TOP, ns))
    if score.shape[0] == LANES:
        nb = -(-ns // SUBLANES) * SUBLANES
        st = score.T[0:nb]
        rblk = lax.broadcasted_iota(jnp.int32, (nb, 1), 0)
        rank = jnp.zeros(st.shape, F32)
        for j in range(ns):
            sj = st[j:j + 1, :]
            rank = rank + jnp.where((sj > st) | ((sj == st) & (rblk > j)), 1.0, 0.0)
        top = jnp.where(rank < k_top, 1.0, 0.0)
        top = jnp.concatenate([top, jnp.zeros((NSP - nb, LANES), F32)], axis=0).T
        return jnp.where(valid, top, 0.0)
    rank = jnp.zeros(score.shape, F32)
    for j in range(ns):
        sj = score[:, j:j + 1]
        beats = (sj > score) | ((sj == score) & (blk > j))
        rank = rank + jnp.where(beats, 1.0, 0.0)
    return jnp.where(valid & (rank < float(min(N_TOP, ns))), 1.0, 0.0)


def _compressed_branch(qb, kc, vc, tpos, nc, scale):
    s = _dot_nt(qb, kc.astype(BF16)) * scale
    cidx = lax.broadcasted_iota(jnp.int32, (1, NCP), 1)
    cmask = (cidx < nc) & (cidx * D_CMP + (L_CMP - 1) <= tpos)
    p = _masked_softmax_pre(jnp.where(cmask, s, NEG))
    return p, _dot(p.astype(BF16), vc.astype(BF16))


def _keys_with_block_onehot(k_ref, koh_ref, k0, width):
    return jnp.concatenate([k_ref[pl.ds(k0, width), :].astype(BF16), koh_ref[pl.ds(k0, width), :]], axis=1)


def _nsa_prompt_body(q_ref, kc_ref, vc_ref, ks_ref, vs_ref, kw_ref, vw_ref, misc_ref, z_ref, c2s_ref, koh_ref,
                     o_ref, sc_ref, mx_ref, l_ref, acc_ref, *, tq, tk, T):
    qi = pl.program_id(1)
    t0 = pl.multiple_of(qi * tq, tq)
    R = GRP * tq
    scale = DH ** -0.5
    c1 = scale * LOG2E
    ns = -(-T // L_SEL)
    nc = T // D_CMP - CMP_R + 1
    n_top = min(N_TOP, ns)
    heads = range(HKV)

    def hd(g):
        return slice(g * DH, (g + 1) * DH)

    qbs = [jnp.concatenate([q_ref[:, (g * GRP + r) * DH:(g * GRP + r + 1) * DH] for r in range(GRP)],
                           axis=0).astype(BF16) for g in heads]
    tpos = t0 + (lax.broadcasted_iota(jnp.int32, (R, 1), 0) & (tq - 1))
    tpos_tok = t0 + lax.broadcasted_iota(jnp.int32, (tq, 1), 0)
    blk = lax.broadcasted_iota(jnp.int32, (1, NSP), 1)

    nwt = WINDOW // tq + 1
    tiles, starts = [[] for _ in heads], []
    for i in range(nwt):
        k0 = t0 - WINDOW + i * tq
        k0c = pl.multiple_of(jnp.maximum(k0, 0), tq)
        kpos = k0 + lax.broadcasted_iota(jnp.int32, (1, tq), 1)
        for g in heads:
            s = _dot_nt(qbs[g], kw_ref[pl.ds(k0c, tq), hd(g)].astype(BF16))
            if i == 0:
                s = jnp.where((kpos > tpos - WINDOW) & (kpos >= 0), s, NEG)
            elif i == nwt - 1:
                s = jnp.where(kpos <= tpos, s, NEG)
            else:
                s = jnp.where(k0 >= 0, s, NEG)
            tiles[g].append(s)
        starts.append(k0c)
    es_w = []
    for g in heads:
        s = jnp.concatenate(tiles[g], axis=1)
        es_w.append(jnp.exp2((s - jnp.max(s, axis=-1, keepdims=True)) * c1))
    o_ws = [None for _ in heads]
    for i in range(nwt):
        for g in heads:
            pv = _dot(es_w[g][:, i * tq:(i + 1) * tq].astype(BF16), vw_ref[pl.ds(starts[i], tq), hd(g)].astype(BF16))
            o_ws[g] = pv if i == 0 else o_ws[g] + pv
    o_ws = [o_ws[g] / jnp.sum(es_w[g], axis=-1, keepdims=True) for g in heads]

    comp = [_compressed_branch(qbs[g], kc_ref[g], vc_ref[g], tpos, nc, scale) for g in heads]
    imps = []
    for g in heads:
        p_c = comp[g][0]
        psum = p_c[0:tq]
        for r in range(1, GRP):
            psum = psum + p_c[r * tq:(r + 1) * tq]
        imps.append(_dot_exact_rhs(psum, c2s_ref[...]))

    def all_valid():
        v = jnp.where(blk <= tpos_tok // L_SEL, 1.0, 0.0)
        return tuple(v for _ in heads)

    sels = lax.cond(t0 + tq <= n_top * L_SEL, all_valid,
                    lambda: tuple(_select_blocks(imps[g], tpos_tok, ns) for g in heads))

    q_augs, q_earlys = [], []
    for g in heads:
        sel_bias = (sels[g] - 1.0) * (-NEG)
        early_bias = jnp.where(blk < lax.div(t0, L_SEL), sel_bias, NEG)
        q_augs.append(jnp.concatenate([qbs[g], jnp.concatenate([sel_bias.astype(BF16)] * GRP, axis=0)], axis=1))
        q_earlys.append(jnp.concatenate([qbs[g], jnp.concatenate([early_bias.astype(BF16)] * GRP, axis=0)], axis=1))

    def keys_aug(k0, width, g):
        return jnp.concatenate([ks_ref[pl.ds(k0, width), hd(g)].astype(BF16), koh_ref[pl.ds(k0, width), :]], axis=1)

    kpos_d = t0 + lax.broadcasted_iota(jnp.int32, (1, tq), 1)
    s_diags = [jnp.where(kpos_d <= tpos, _dot_nt(q_augs[g], keys_aug(t0, tq, g)), NEG) for g in heads]
    n_early = lax.div(t0 + tk - 1, tk)
    for g in heads:
        mx_ref[g] = s_diags[g]

    def pass1(kt, carry):
        k0 = pl.multiple_of(kt * tk, tk)
        ss = [_dot_nt(q_earlys[g], keys_aug(k0, tk, g)) for g in heads]
        old = [mx_ref[g] for g in heads]
        for g in heads:
            mx = ss[g][:, 0:LANES]
            for c in range(1, tk // LANES):
                mx = jnp.maximum(mx, ss[g][:, c * LANES:(c + 1) * LANES])
            sc_ref[g, kt] = ss[g]
            mx_ref[g] = jnp.maximum(old[g], mx)
        return carry

    lax.fori_loop(0, n_early, pass1, 0)
    m2s = [jnp.broadcast_to(jnp.max(mx_ref[g], axis=-1, keepdims=True) * c1, (R, LANES)) for g in heads]
    e_ds = [jnp.exp2(s_diags[g] * c1 - m2s[g]) for g in heads]
    pv_ds = [_dot(e_ds[g].astype(BF16), vs_ref[pl.ds(t0, tq), hd(g)].astype(BF16)) for g in heads]
    for g in heads:
        mx_ref[g] = m2s[g]
        l_ref[g] = e_ds[g]
        acc_ref[g] = pv_ds[g]

    def pass2(kt, carry):
        k0 = pl.multiple_of(kt * tk, tk)
        ss = [sc_ref[g, kt] for g in heads]
        m2 = [mx_ref[g] for g in heads]
        l_old = [l_ref[g] for g in heads]
        acc_old = [acc_ref[g] for g in heads]
        ebs, lsums = [], []
        for g in heads:
            es = [jnp.exp2(ss[g][:, c * LANES:(c + 1) * LANES] * c1 - m2[g]) for c in range(tk // LANES)]
            lsum = es[0]
            for ec in es[1:]:
                lsum = lsum + ec
            lsums.append(lsum)
            ebs.append(jnp.concatenate(es, axis=1).astype(BF16))
        pvs = [_dot(ebs[g], vs_ref[pl.ds(k0, tk), hd(g)].astype(BF16)) for g in heads]
        for g in heads:
            l_ref[g] = l_old[g] + lsums[g]
            acc_ref[g] = acc_old[g] + pvs[g]
        return carry

    lax.fori_loop(0, n_early, pass2, 0)

    outs = []
    for g in heads:
        o_s = acc_ref[g] / jnp.sum(l_ref[g], axis=-1, keepdims=True)
        gm = _sigmoid(misc_ref[:, g * LANES:g * LANES + GRP * 3])

        def gate(j):
            return jnp.concatenate([gm[:, r * 3 + j:r * 3 + j + 1] for r in range(GRP)], axis=0)

        outs.append(gate(0) * comp[g][1] + gate(1) * o_s + gate(2) * o_ws[g])
    for g in heads:
        for r in range(GRP):
            c = (g * GRP + r) * DH
            o_ref[:, c:c + DH] = (outs[g][r * tq:(r + 1) * tq] * _silu(z_ref[:, c:c + DH])).astype(BF16)


def _nsa_prompt(P, kcv, c2s, koh, B, T, tq, tk):
    assert T % tq == 0 and tq & (tq - 1) == 0 and tk % tq == 0 and T % tk == 0 and WINDOW % tq == 0
    nq = T // tq
    qw = H_A * DH
    kw = HKV * DH
    R = GRP * tq

    def kvspec(slot):
        return pl.BlockSpec((T, kw), lambda b, i: (b, A_KV // kw + slot))

    return pl.pallas_call(
        functools.partial(_nsa_prompt_body, tq=tq, tk=tk, T=T),
        grid=(B, nq),
        in_specs=[pl.BlockSpec((tq, qw), lambda b, i: (b * nq + i, A_Q // qw)),
                  pl.BlockSpec((None, None, HKV, NCP, DH), lambda b, i: (b, 0, 0, 0, 0)),
                  pl.BlockSpec((None, None, HKV, NCP, DH), lambda b, i: (b, 1, 0, 0, 0)),
                  kvspec(2), kvspec(3), kvspec(4), kvspec(5),
                  pl.BlockSpec((tq, HKV * LANES), lambda b, i: (b * nq + i, A_MISC // (HKV * LANES))),
                  pl.BlockSpec((tq, qw), lambda b, i: (b * nq + i, A_ZA // qw)),
                  pl.BlockSpec((NCP, NSP), lambda b, i: (0, 0)),
                  pl.BlockSpec((T, NSP), lambda b, i: (0, 0))],
        out_specs=pl.BlockSpec((tq, qw), lambda b, i: (b * nq + i, 0)),
        out_shape=jax.ShapeDtypeStruct((B * T, qw), BF16),
        scratch_shapes=[pltpu.VMEM((HKV, T // tk, R, tk), F32), pltpu.VMEM((HKV, R, LANES), F32),
                        pltpu.VMEM((HKV, R, LANES), F32), pltpu.VMEM((HKV, R, DH), F32)],
        compiler_params=_params(("arbitrary", "arbitrary")),
    )(P, kcv, kcv, P, P, P, P, P, P, c2s, koh)


def _pad_rows(x, rows):
    return jnp.concatenate([x, jnp.zeros((rows - x.shape[0], x.shape[1]), x.dtype)], axis=0)


def _nsa_sample_body(pt_ref, cache_ref, win_ref, q_ref, kvr_ref, kvw_ref, misc_ref, z_ref, w1_ref, cb_ref, w2_ref,
                     c2s_ref, koh_ref, o_ref, wout_ref, pbuf, sem, *, n_pages, page, past_len, dec_seq, wb):
    b = pl.program_id(0)
    scale = DH ** -0.5
    c1 = scale * LOG2E
    L = past_len + dec_seq
    ns = -(-L // L_SEL)
    nc = L // D_CMP - CMP_R + 1
    R = GRP * TP
    per_page = page // D_CMP
    nch = N_KV_SLOTS * HKV
    grp_rows = D_CMP * nch
    pitch = grp_rows + 1

    def page_copies(seq, slot):
        cps = []
        for j in range(n_pages):
            pid = pt_ref[seq * n_pages + j]
            for n in range(per_page):
                cps.append(pltpu.make_async_copy(
                    cache_ref.at[pid, pl.ds(n * grp_rows, grp_rows), :],
                    pbuf.at[slot, pl.ds((j * per_page + n) * pitch, grp_rows), :], sem.at[slot]))
        return cps

    slot = lax.rem(b, 2)

    @pl.when(b == 0)
    def _():
        for cp in page_copies(0, 0):
            cp.start()

    for cp in page_copies(b, slot):
        cp.wait()

    last = pl.num_programs(0) - 1
    next_copies = page_copies(jnp.minimum(b + 1, last), 1 - slot)
    copies_per_group = len(next_copies) // (2 * (D_CMP // 2))

    def gather(p, ch):
        return pbuf[slot, pl.ds(p * nch + ch, NCP, stride=pitch), :]

    trow = lax.broadcasted_iota(jnp.int32, (R, 1), 0) & (TP - 1)
    tpos = past_len + trow
    tpos_tok = past_len + lax.broadcasted_iota(jnp.int32, (TP, 1), 0)
    lane = lax.broadcasted_iota(jnp.int32, (1, LANES), 1)
    new_pos = past_len + lane
    new_ok = (new_pos <= tpos) & (new_pos < L)
    new_blk = (past_len + lax.broadcasted_iota(jnp.int32, (LANES, 1), 0)) // L_SEL
    new_oh = jnp.where(lax.broadcasted_iota(jnp.int32, (LANES, NSP), 1) == new_blk, 1.0, 0.0).astype(BF16)
    wpos = (past_len - wb) + lax.broadcasted_iota(jnp.int32, (1, wb), 1)
    wch = 2 * HKV

    qbs, o_ws, raw = [], [], []
    for g in range(HKV):
        qb = jnp.concatenate([q_ref[:, (g * GRP + r) * DH:(g * GRP + r + 1) * DH] for r in range(GRP)],
                             axis=0).astype(BF16)
        kw_col = g * DH
        vw_col = (HKV + g) * DH
        s_w = _dot_nt(qb, win_ref[pl.ds(g, wb, stride=wch), :].astype(BF16))
        s_w = jnp.where((wpos <= tpos) & (wpos > tpos - WINDOW), s_w, NEG)
        s_n = _dot_nt(qb, _pad_rows(kvw_ref[:, kw_col:kw_col + DH], LANES).astype(BF16))
        s_n = jnp.where(new_ok & (new_pos > tpos - WINDOW), s_n, NEG)
        s = jnp.concatenate([s_w, s_n], axis=1)
        e = jnp.exp2((s - jnp.max(s, axis=-1, keepdims=True)) * c1)
        o_w = (_dot(e[:, :wb].astype(BF16), win_ref[pl.ds(HKV + g, wb, stride=wch), :].astype(BF16))
               + _dot(e[:, wb:].astype(BF16), _pad_rows(kvw_ref[:, vw_col:vw_col + DH], LANES).astype(BF16)))
        o_ws.append(o_w / jnp.sum(e, axis=-1, keepdims=True))
        ksel_col = (2 * HKV + g) * DH
        pieces = [_dot_nt(qb, gather(p, 2 * HKV + g).astype(BF16)) for p in range(D_CMP)]
        pieces.append(_dot_nt(qb, _pad_rows(kvr_ref[:, ksel_col:ksel_col + DH], LANES).astype(BF16)))
        raw.append(pieces)
        qbs.append(qb)

    kcv = {}
    for s in range(2):
        u0 = jnp.zeros((HKV * NCP, CMP_HID), F32)
        u1 = jnp.zeros((HKV * NCP, CMP_HID), F32)
        for pp in range(D_CMP // 2):
            xs = []
            for g in range(HKV):
                ch = s * HKV + g
                xs.append(jnp.concatenate([gather(2 * pp, ch), gather(2 * pp + 1, ch)], axis=1))
            xp = jnp.concatenate(xs, axis=0).astype(BF16)
            u0 = u0 + _dot(xp, w1_ref[s, pl.ds(pp * 2 * DH, 2 * DH), :])
            u1 = u1 + _dot(xp, w1_ref[s, pl.ds(D_CMP * DH + pp * 2 * DH, 2 * DH), :])
            grp = s * (D_CMP // 2) + pp
            for cp in next_copies[grp * copies_per_group:(grp + 1) * copies_per_group]:
                cp.start()
        for g in range(HKV):
            h = _compress_hidden(u0[g * NCP:(g + 1) * NCP], u1[g * NCP:(g + 1) * NCP], cb_ref[s])
            kcv[s, g] = _dot(_gelu_tanh(h).astype(BF16), w2_ref[s])

    heads = range(HKV)
    comp = [_compressed_branch(qbs[g], kcv[0, g], kcv[1, g], tpos, nc, scale) for g in heads]
    imps = []
    for g in heads:
        p_c = comp[g][0]
        psum = p_c[0:TP]
        for r in range(1, GRP):
            psum = psum + p_c[r * TP:(r + 1) * TP]
        imps.append(_dot_exact_rhs(psum, c2s_ref[...]))
    sels = [_select_blocks(imps[g], tpos_tok, ns) for g in heads]
    biases = [jnp.concatenate([((sels[g] - 1.0) * (-NEG)).astype(BF16)] * GRP, axis=0) for g in heads]
    bias_past = [_dot_nt(biases[g], koh_ref[...]) for g in heads]
    bias_new = [jnp.where(new_ok, _dot_nt(biases[g], new_oh), NEG) for g in heads]
    es = []
    for g in heads:
        s = jnp.concatenate([pc + bias_past[g] for pc in raw[g][:D_CMP]] + [raw[g][D_CMP] + bias_new[g]], axis=1)
        es.append(jnp.exp2((s - jnp.max(s, axis=-1, keepdims=True)) * c1))
    o_ss = []
    for g in heads:
        vsel_col = (3 * HKV + g) * DH
        o_ss.append(_dot(es[g][:, D_CMP * NCP:].astype(BF16),
                         _pad_rows(kvr_ref[:, vsel_col:vsel_col + DH], LANES).astype(BF16)))
    for p in range(D_CMP):
        for g in heads:
            o_ss[g] = o_ss[g] + _dot(es[g][:, p * NCP:(p + 1) * NCP].astype(BF16),
                                     gather(p, 3 * HKV + g).astype(BF16))
    outs = []
    for g in heads:
        o_s = o_ss[g] / jnp.sum(es[g], axis=-1, keepdims=True)
        gm = _sigmoid(misc_ref[:, g * LANES:g * LANES + GRP * 3])

        def gate(j):
            return jnp.concatenate([gm[:, r * 3 + j:r * 3 + j + 1] for r in range(GRP)], axis=0)

        outs.append(gate(0) * comp[g][1] + gate(1) * o_s + gate(2) * o_ws[g])

    for g in range(HKV):
        for r in range(GRP):
            c = (g * GRP + r) * DH
            o_ref[:, c:c + DH] = outs[g][r * TP:(r + 1) * TP] * _silu(z_ref[:, c:c + DH])

    wout_ref[pl.ds(0, (wb - dec_seq) * wch), :] = win_ref[pl.ds(dec_seq * wch, (wb - dec_seq) * wch), :]
    for t in range(dec_seq):
        for ch in range(wch):
            wout_ref[pl.ds((wb - dec_seq + t) * wch + ch, 1), :] = kvw_ref[t:t + 1, ch * DH:(ch + 1) * DH]

    @pl.when(b == last)
    def _():
        for cp in next_copies:
            cp.wait()


def _nsa_sample(Ps, cache, cache_win, page_table, w1, cb, w2, c2s, dec_seq):
    NB, n_pages = page_table.shape
    n_pool, prow, _ = cache.shape
    page = prow // (N_KV_SLOTS * HKV)
    wb = cache_win.shape[1] // (2 * HKV)
    past_len = n_pages * page
    L = past_len + dec_seq
    assert wb == WINDOW and (L // D_CMP) == NCP and NCP * D_CMP == past_len and dec_seq <= TP
    assert -(-L // L_SEL) < NSP and page % D_CMP == 0 and (dec_seq * 2 * HKV) % SUBLANES == 0
    qw = H_A * DH
    nch = N_KV_SLOTS * HKV
    koh = jnp.asarray(np.arange(NCP)[:, None] * D_CMP // L_SEL == np.arange(NSP)[None, :], dtype=BF16)

    grid_spec = pltpu.PrefetchScalarGridSpec(
        num_scalar_prefetch=1,
        grid=(NB,),
        in_specs=[
            pl.BlockSpec(memory_space=pl.ANY),
            pl.BlockSpec((None, wb * 2 * HKV, DH), lambda b, pt: (b, 0, 0)),
            pl.BlockSpec((TP, qw), lambda b, pt: (b, A_Q // qw)),
            pl.BlockSpec((TP, 4 * HKV * DH), lambda b, pt: (b, A_KV // (4 * HKV * DH))),
            pl.BlockSpec((TP, 2 * HKV * DH), lambda b, pt: (b, (A_KV + 4 * HKV * DH) // (2 * HKV * DH))),
            pl.BlockSpec((TP, HKV * LANES), lambda b, pt: (b, A_MISC // (HKV * LANES))),
            pl.BlockSpec((TP, qw), lambda b, pt: (b, A_ZA // qw)),
            pl.BlockSpec((2, L_CMP * DH, CMP_HID), lambda b, pt: (0, 0, 0)),
            pl.BlockSpec((2, 1, CMP_HID), lambda b, pt: (0, 0, 0)),
            pl.BlockSpec((2, CMP_HID, DH), lambda b, pt: (0, 0, 0)),
            pl.BlockSpec((NCP, NSP), lambda b, pt: (0, 0)),
            pl.BlockSpec((NCP, NSP), lambda b, pt: (0, 0))],
        out_specs=[pl.BlockSpec((TP, qw), lambda b, pt: (b, 0)),
                   pl.BlockSpec((None, wb * 2 * HKV, DH), lambda b, pt: (b, 0, 0))],
        scratch_shapes=[pltpu.VMEM((2, NCP * (D_CMP * nch + 1), DH), F32), pltpu.SemaphoreType.DMA((2,))],
    )
    return pl.pallas_call(
        functools.partial(_nsa_sample_body, n_pages=n_pages, page=page, past_len=past_len, dec_seq=dec_seq, wb=wb),
        grid_spec=grid_spec,
        out_shape=[jax.ShapeDtypeStruct((NB * TP, qw), F32),
                   jax.ShapeDtypeStruct((NB, wb * 2 * HKV, DH), F32)],
        compiler_params=_params(("arbitrary",)),
    )(page_table.reshape(-1), cache, cache_win, Ps, Ps, Ps, Ps, Ps, w1, cb, w2, c2s, koh)


def _rec_chunk(q, k, v, la, gate, S, C, last_row, nh, K, V):
    sb = min(16, C)
    ri = lax.broadcasted_iota(jnp.int32, (C, C), 0)
    cj = lax.broadcasted_iota(jnp.int32, (C, C), 1)
    causal = cj <= ri
    tri = jnp.where(causal, 1.0, 0.0).astype(BF16)
    b = _dot_exact_lhs(tri, la)
    qe = (q * jnp.exp(b)).astype(BF16)
    qis, kis = [], []
    k_ref_prev, c_prev = None, None
    for i in range(C // sb):
        r0 = i * sb
        ci = b[r0 + sb // 2:r0 + sb // 2 + 1, :]
        qis.append((q[r0:r0 + sb] * jnp.exp(b[r0:r0 + sb] - ci)).astype(BF16))
        k_new = k[r0:r0 + sb] * jnp.exp(jnp.minimum(ci - b[r0:r0 + sb], 80.0))
        k_scaled = k_new if i == 0 else jnp.concatenate([k_ref_prev * jnp.exp(ci - c_prev), k_new], axis=0)
        k_ref_prev, c_prev = k_scaled, ci
        rest = C - r0 - sb
        kis.append((k_scaled if rest == 0 else
                    jnp.concatenate([k_scaled, jnp.zeros((rest, k.shape[1]), F32)], axis=0)).astype(BF16))
    b_last = b[last_row:last_row + 1, :]
    rowi = lax.broadcasted_iota(jnp.int32, (C, 1), 0)
    kd = jnp.where(rowi <= last_row, k * jnp.exp(jnp.minimum(b_last - b, 0.0)), 0.0)
    stack = jnp.concatenate([kd, jnp.broadcast_to(b_last, (SUBLANES, nh * K)),
                             jnp.zeros((LANES - C - SUBLANES, nh * K), F32)], axis=0)
    vb = v.astype(BF16)
    v_pad = jnp.concatenate([vb, jnp.zeros((LANES - C, nh * V), BF16)], axis=0)
    outs, s_new = [], []
    for h in range(nh):
        ks = slice(h * K, (h + 1) * K)
        vs = slice(h * V, (h + 1) * V)
        o = _dot(qe[:, ks], S[h].astype(BF16))
        rows = [_dot_nt(qi[:, ks], ki[:, ks]) for qi, ki in zip(qis, kis)]
        att = rows[0] if len(rows) == 1 else jnp.concatenate(rows, axis=0)
        att = jnp.where(causal, att, 0.0)
        o = o + _dot(att.astype(BF16), vb[:, vs])
        stack_t = stack[:, ks].T
        a_col = jnp.exp(stack_t[:, C:C + 1])
        s_new.append(a_col * S[h] + _dot(stack_t.astype(BF16), v_pad[:, vs]))
        ms = jnp.mean(o * o, axis=-1, keepdims=True)
        outs.append(o * lax.rsqrt(ms + EPS) * gate[:, vs])
    return outs, s_new


def _rec_body(*refs, variant, nh, K, V, C, TB, last_row, has_s0, layer_idx):
    refs = list(refs)
    q_ref, k_ref, v_ref, z_ref = refs[:4]
    pos = 4
    if variant == "gla":
        misc_ref, w2_ref, gb_ref = refs[pos:pos + 3]
        pos += 3
    else:
        lb_ref = refs[pos]
        pos += 1
    gn_ref = refs[pos]
    pos += 1
    if has_s0:
        s0_ref = refs[pos]
        pos += 1
    o_ref, s_ref = refs[pos:pos + 2]
    n_chunks = TB // C
    single = n_chunks == 1 and has_s0

    if not single:
        @pl.when(pl.program_id(2) == 0)
        def _():
            if has_s0:
                s_ref[...] = s0_ref[...]
            else:
                s_ref[...] = jnp.zeros(s_ref.shape, F32)

    if variant == "hgrn":
        lg = lb_ref[...]
        e = jnp.exp(lg - jnp.max(lg, axis=0, keepdims=True))
        prob = e / jnp.sum(e, axis=0, keepdims=True)
        lb = prob[1:2]
        for i in range(2, layer_idx + 1):
            lb = lb + prob[i:i + 1]
    gn_all = jnp.concatenate([gn_ref[...]] * nh, axis=1)

    def chunk(c0):
        rows = pl.ds(c0, C)
        qr = q_ref[rows, :]
        kr = k_ref[rows, :]
        v = v_ref[rows, :]
        gate = _silu(z_ref[rows, :]) * gn_all
        if variant == "gla":
            zg = _dot(misc_ref[rows, :].astype(BF16), w2_ref[...].astype(BF16)) + gb_ref[...]
            q, k, la = qr * (K ** -0.5), kr, _log_sigmoid(zg) / GLA_GATE_NORM
        else:
            t = jnp.exp(-jnp.abs(kr))
            r = 1.0 / (1.0 + t)
            tr = t * r
            nonneg = kr >= 0.0
            sig = jnp.where(nonneg, r, tr)
            nsig = jnp.where(nonneg, tr, r)
            q, k, la = _silu(qr), (1.0 - lb) * nsig, jnp.log(lb + (1.0 - lb) * sig)
        src = s0_ref if single else s_ref
        outs, s_new = _rec_chunk(q, k, v, la, gate, [src[h] for h in range(nh)], C, last_row, nh, K, V)
        for h in range(nh):
            o_ref[rows, h * V:(h + 1) * V] = outs[h].astype(o_ref.dtype)
            s_ref[h] = s_new[h]

    if n_chunks == 1:
        chunk(0)
    else:
        def body(ci, carry):
            chunk(pl.multiple_of(ci * C, C))
            return carry

        lax.fori_loop(0, n_chunks, body, 0, unroll=8)


def _recurrence(P, cols, extra, gn, s0, *, variant, B, T, H, K, V, C, TB, nh, last_row, out_dtype, layer_idx=1):
    cq, ck, cv, cz = cols
    assert H % nh == 0 and T % TB == 0 and TB % C == 0
    nt = T // TB

    def colspec(c0, w):
        assert c0 % (nh * w) == 0
        return pl.BlockSpec((TB, nh * w), lambda b, j, t: (b * nt + t, c0 // (nh * w) + j))

    in_specs = [colspec(cq, K), colspec(ck, K), colspec(cv, V), colspec(cz, V)]
    args = [P, P, P, P]
    if variant == "gla":
        w2p, gb = extra
        in_specs += [pl.BlockSpec((TB, LANES), lambda b, j, t: (b * nt + t, A_MISC // LANES)),
                     pl.BlockSpec((LANES, nh * K), lambda b, j, t: (0, j)),
                     pl.BlockSpec((1, nh * K), lambda b, j, t: (0, j))]
        args += [P, w2p, gb]
    else:
        (lb_logits,) = extra
        in_specs += [pl.BlockSpec((lb_logits.shape[0], nh * K), lambda b, j, t: (0, j))]
        args += [lb_logits]
    in_specs += [pl.BlockSpec((1, V), lambda b, j, t: (0, 0))]
    args += [gn]
    st_spec = pl.BlockSpec((None, None, nh, K, V), lambda b, j, t: (0, b, j, 0, 0))
    if s0 is not None:
        in_specs += [st_spec]
        args += [s0]
    return pl.pallas_call(
        functools.partial(_rec_body, variant=variant, nh=nh, K=K, V=V, C=C, TB=TB, last_row=last_row,
                          has_s0=s0 is not None, layer_idx=layer_idx),
        grid=(B, H // nh, nt),
        in_specs=in_specs,
        out_specs=[pl.BlockSpec((TB, nh * V), lambda b, j, t: (b * nt + t, j)), st_spec],
        out_shape=[jax.ShapeDtypeStruct((B * T, H * V), out_dtype),
                   jax.ShapeDtypeStruct((1, B, H, K, V), F32)],
        compiler_params=_params(("arbitrary", "arbitrary", "arbitrary")),
    )(*args)


def _layout_a_w_in(w):
    off = np.concatenate([[0], np.cumsum(A_SIZES)])
    q, kv, gbr, za, qb, kb, vb, lr, zb = [w[:, off[i]:off[i + 1]] for i in range(len(A_SIZES))]
    D = w.shape[0]
    z = lambda n: jnp.zeros((D, n), w.dtype)
    misc0 = jnp.concatenate([gbr[:, :GRP * 3], lr, z(LANES - GRP * 3 - GLA_LR)], axis=1)
    misc1 = jnp.concatenate([gbr[:, GRP * 3:], z(LANES - GRP * 3)], axis=1)
    out = jnp.concatenate([q, vb, zb, za, qb, kb, kv, misc0, misc1, z(NA - A_MISC - HKV * LANES)], axis=1)
    assert out.shape[1] == NA
    return out.astype(BF16)


def _cmp_to_sel():
    c_start = np.arange(NCP)[:, None] * D_CMP
    s_start = np.arange(NSP)[None, :] * L_SEL
    overlap = np.clip(np.minimum(c_start + L_CMP, s_start + L_SEL) - np.maximum(c_start, s_start), 0, None)
    return jnp.asarray(overlap / D_CMP, dtype=BF16)


def kernel(x_prompt, x_sample, cache_kv, cache_win, state_gla, state_hgrn, page_table, a_norm, a_w_in, a_gla_w2,
           a_gla_b, a_gla_gn, a_cmp_pe, a_cmp_w1, a_cmp_b1, a_cmp_w2, a_w_out, c_norm, c_w_in, c_lb_logits, c_gn,
           c_w_out, final_norm):
    B, T, D = x_prompt.shape
    NB, TS, _ = x_sample.shape
    n_pool, page = cache_kv.shape[1], cache_kv.shape[2]
    wb = cache_win.shape[2]
    assert a_norm.shape[0] == 1 and c_norm.shape[0] == 1 and c_lb_logits.shape[0] == 2

    wa = _layout_a_w_in(a_w_in[0])
    wc = c_w_in[0].astype(BF16)
    wao = a_w_out[0].astype(BF16)
    wco = c_w_out[0].astype(BF16)
    w1 = a_cmp_w1[0].astype(BF16)
    w2 = a_cmp_w2[0].astype(BF16)
    pe = a_cmp_pe[0].reshape(2, 1, L_CMP * DH)
    b1 = a_cmp_b1[0].reshape(2, 1, CMP_HID)
    w2p = jnp.zeros((LANES, H_B * DK_B), F32).at[MISC_LR:MISC_LR + GLA_LR, :].set(a_gla_w2[0])
    gb = a_gla_b[0].reshape(1, H_B * DK_B)
    c2s = _cmp_to_sel()
    a_nw = a_norm[0].reshape(1, D)
    c_nw = c_norm[0].reshape(1, D)
    f_nw = final_norm.reshape(1, D)
    gla_gn = a_gla_gn[0].reshape(1, DV_B)
    hg_gn = c_gn[0].reshape(1, DV_C)

    xp = x_prompt.reshape(B * T, D)
    xs = jnp.pad(x_sample, ((0, 0), (0, TP - TS), (0, 0))).reshape(NB * TP, D)

    cb = _cmp_bias(pe, w1, b1)

    Pp = _norm_matmul(xp, a_nw, wa, tm=1024, tn=1024)
    kcv = _compress_prompt(Pp, w1, cb, w2, B, T)
    koh = jnp.asarray(np.arange(T)[:, None] // L_SEL == np.arange(NSP)[None, :], dtype=BF16)
    oa_p = _nsa_prompt(Pp, kcv, c2s, koh, B, T, tq=128, tk=512)
    ob_p, gla_p = _recurrence(Pp, (A_QB, A_KB, A_VB, A_ZB), (w2p, gb), gla_gn, None, variant="gla", B=B, T=T,
                              H=H_B, K=DK_B, V=DV_B, C=64, TB=512, nh=4, last_row=63, out_dtype=BF16)
    y1p = _out_proj(oa_p, 0, ob_p, 0, wao, xp, a_nw, tm=512, final_norm=False)

    Ps = _norm_matmul(xs, a_nw, wa, tm=NB * TP, tn=1024)
    cache2 = cache_kv[0].reshape(n_pool, page * N_KV_SLOTS * HKV, DH)
    win2 = cache_win[0].reshape(NB, wb * 2 * HKV, DH)
    oa_s, win_out = _nsa_sample(Ps, cache2, win2, page_table, w1, cb, w2, c2s, TS)
    ob_s, gla_s = _recurrence(Ps, (A_QB, A_KB, A_VB, A_ZB), (w2p, gb), gla_gn, state_gla, variant="gla", B=NB, T=TP,
                              H=H_B, K=DK_B, V=DV_B, C=TP, TB=TP, nh=H_B, last_row=TS - 1, out_dtype=F32)
    y1s = _out_proj(oa_s, 0, ob_s, 0, wao, xs, a_nw, tm=min(512, NB * TP), final_norm=False)

    Pc = _norm_matmul(y1p, c_nw, wc, tm=1024, tn=1024)
    hc = H_C * DK_C
    oc_p, hg_p = _recurrence(Pc, (0, hc, 2 * hc, 3 * hc), (c_lb_logits,), hg_gn, None, variant="hgrn", B=B, T=T,
                             H=H_C, K=DK_C, V=DV_C, C=64, TB=512, nh=4, last_row=63, out_dtype=BF16)
    y_prompt = _out_proj(oc_p, 0, oc_p, 1, wco, y1p, f_nw, tm=512, final_norm=True)

    Pcs = _norm_matmul(y1s, c_nw, wc, tm=NB * TP, tn=1024)
    oc_s, hg_s = _recurrence(Pcs, (0, hc, 2 * hc, 3 * hc), (c_lb_logits,), hg_gn, state_hgrn, variant="hgrn", B=NB,
                             T=TP, H=H_C, K=DK_C, V=DV_C, C=TP, TB=TP, nh=H_C, last_row=TS - 1, out_dtype=F32)
    y_sample = _out_proj(oc_s, 0, oc_s, 1, wco, y1s, f_nw, tm=min(512, NB * TP), final_norm=True)

    kvw = 6 * HKV * DH
    nrw = N_KV_SLOTS * HKV * DH
    Pp3 = Pp.reshape(B, T, NA)
    Ps3 = Ps.reshape(NB, TP, NA)
    kv_p = Pp3[:, :, A_KV:A_KV + nrw].reshape(1, B, T, N_KV_SLOTS, HKV, DH)
    kv_s = Ps3[:, :TS, A_KV:A_KV + nrw].reshape(1, NB, TS, N_KV_SLOTS, HKV, DH)
    keep = min(WINDOW, T)
    win_p = Pp3[:, T - keep:, A_KV + nrw:A_KV + kvw].reshape(1, B, keep, 2, HKV, DH)
    win_s = win_out.reshape(1, NB, wb, 2, HKV, DH)
    return (y_prompt.reshape(B, T, D), y_sample.reshape(NB, TP, D)[:, :TS], kv_p, kv_s, win_p, win_s,
            gla_p, gla_s, hg_p, hg_s)
```

```python
import functools
import math

import numpy as np
import jax
import jax.numpy as jnp
from jax import lax
from jax.experimental import pallas as pl
from jax.experimental.pallas import tpu as pltpu

F32 = jnp.float32
BF16 = jnp.bfloat16

D_MODEL = 2048
DH = 128
H_A = 8
HKV = 2
GRP = H_A // HKV
L_CMP = 32
D_CMP = 16
CMP_R = L_CMP // D_CMP
CMP_HID = 2 * DH
L_SEL = 64
N_TOP = 16
WINDOW = 512
N_KV_SLOTS = 4
H_B = 4
DK_B = 128
DV_B = 256
GLA_LR = 16
GLA_GATE_NORM = 16.0
DK_C = 128
H_C = 16
DV_C = 128
EPS = 1e-6
FORCE_SCORE = 1e9
A_SIZES = (H_A * DH, 6 * HKV * DH, 3 * H_A, H_A * DH, H_B * DK_B, H_B * DK_B, H_B * DV_B, GLA_LR, H_B * DV_B)

LANES = 128
SUBLANES = 8
VMEM_LIMIT = 56 * 1024 * 1024

A_Q = 0
A_VB = 1024
A_ZB = 2048
A_ZA = 3072
A_QB = 4096
A_KB = 4608
A_KV = 5120
A_MISC = 6656
NA = 7168
MISC_LR = GRP * 3

NCP = 128
NSP = 128
TP = 8
NEG = -1e30
LOG2E = 1.4426950408889634


def _dot(a, b):
    return jnp.dot(a, b, preferred_element_type=F32)


def _dot_nt(a, b):
    return lax.dot_general(a, b, (((1,), (1,)), ((), ())), preferred_element_type=F32)


def _split3(a):
    hi = a.astype(BF16)
    r = a - hi.astype(F32)
    mid = r.astype(BF16)
    lo = (r - mid.astype(F32)).astype(BF16)
    return hi, mid, lo


def _dot_exact_rhs(a, b_exact):
    hi, mid, lo = _split3(a)
    return _dot(hi, b_exact) + _dot(mid, b_exact) + _dot(lo, b_exact)


def _dot_exact_lhs(a_exact, b):
    hi, mid, lo = _split3(b)
    return _dot(a_exact, hi) + _dot(a_exact, mid) + _dot(a_exact, lo)


def _sigmoid(x):
    return 1.0 / (1.0 + jnp.exp(-x))


def _silu(x):
    h = 0.5 * x
    return h + h * jnp.tanh(h)


def _log_sigmoid(x):
    return jnp.minimum(x, 0.0) - jnp.log(1.0 + jnp.exp(-jnp.abs(x)))


def _gelu_tanh(x):
    return 0.5 * x * (1.0 + jnp.tanh(np.sqrt(2.0 / np.pi) * (x + 0.044715 * (x * x * x))))


def _masked_softmax_pre(sm):
    m = jnp.max(sm, axis=-1, keepdims=True)
    e = jnp.where(sm > 0.5 * NEG, jnp.exp(sm - m), 0.0)
    l = jnp.sum(e, axis=-1, keepdims=True)
    return e / jnp.maximum(l, 1e-30)


def _params(sem):
    return pltpu.CompilerParams(dimension_semantics=sem, vmem_limit_bytes=VMEM_LIMIT)


def _norm_matmul_body(x_ref, nw_ref, w_ref, o_ref, h_ref):
    @pl.when(pl.program_id(1) == 0)
    def _():
        x = x_ref[...]
        ms = jnp.mean(x * x, axis=-1, keepdims=True)
        h_ref[...] = (x * lax.rsqrt(ms + EPS) * nw_ref[...]).astype(BF16)

    o_ref[...] = _dot(h_ref[...], w_ref[...])


def _norm_matmul(x, nw, w, tm, tn):
    M, D = x.shape
    N = w.shape[1]
    assert M % tm == 0 and N % tn == 0
    return pl.pallas_call(
        _norm_matmul_body,
        grid=(M // tm, N // tn),
        in_specs=[pl.BlockSpec((tm, D), lambda i, j: (i, 0)),
                  pl.BlockSpec((1, D), lambda i, j: (0, 0)),
                  pl.BlockSpec((D, tn), lambda i, j: (0, j))],
        out_specs=pl.BlockSpec((tm, tn), lambda i, j: (i, j)),
        out_shape=jax.ShapeDtypeStruct((M, N), F32),
        scratch_shapes=[pltpu.VMEM((tm, D), BF16)],
        compiler_params=_params(("arbitrary", "arbitrary")),
    )(x, nw, w)


def _out_proj_body(a1_ref, a2_ref, w1_ref, w2_ref, x_ref, nw_ref, y_ref, *, final_norm):
    y = x_ref[...] + _dot(a1_ref[...].astype(BF16), w1_ref[...]) + _dot(a2_ref[...].astype(BF16), w2_ref[...])
    if final_norm:
        ms = jnp.mean(y * y, axis=-1, keepdims=True)
        y = y * lax.rsqrt(ms + EPS) * nw_ref[...]
    y_ref[...] = y


def _out_proj(a1, c1, a2, c2, w, x, nw, tm, final_norm):
    M, D = x.shape
    KH = w.shape[0] // 2
    assert M % tm == 0
    return pl.pallas_call(
        functools.partial(_out_proj_body, final_norm=final_norm),
        grid=(M // tm,),
        in_specs=[pl.BlockSpec((tm, KH), lambda i: (i, c1)),
                  pl.BlockSpec((tm, KH), lambda i: (i, c2)),
                  pl.BlockSpec((KH, D), lambda i: (0, 0)),
                  pl.BlockSpec((KH, D), lambda i: (1, 0)),
                  pl.BlockSpec((tm, D), lambda i: (i, 0)),
                  pl.BlockSpec((1, D), lambda i: (0, 0))],
        out_specs=pl.BlockSpec((tm, D), lambda i: (i, 0)),
        out_shape=jax.ShapeDtypeStruct((M, D), F32),
        compiler_params=_params(("arbitrary",)),
    )(a1, a2, w, w, x, nw)


def _cmp_bias_body(pe_ref, w1_ref, b1_ref, o_ref):
    pe = jnp.broadcast_to(pe_ref[...], (SUBLANES, pe_ref.shape[-1])).astype(BF16)
    o_ref[...] = b1_ref[...] + _dot(pe, w1_ref[...])[0:1, :]


def _cmp_bias(pe, w1, b1):
    S, KF, HID = w1.shape
    return pl.pallas_call(
        _cmp_bias_body,
        grid=(S,),
        in_specs=[pl.BlockSpec((None, 1, KF), lambda s: (s, 0, 0)),
                  pl.BlockSpec((None, KF, HID), lambda s: (s, 0, 0)),
                  pl.BlockSpec((None, 1, HID), lambda s: (s, 0, 0))],
        out_specs=pl.BlockSpec((None, 1, HID), lambda s: (s, 0, 0)),
        out_shape=jax.ShapeDtypeStruct((S, 1, HID), F32),
        compiler_params=_params(("arbitrary",)),
    )(pe, w1, b1)


def _compress_hidden(u0, u1, bias):
    return u0 + pltpu.roll(u1, NCP - 1, 0) + bias


def _compress_prompt_body(x_ref, w1_ref, cb_ref, w2_ref, o_ref):
    u0 = jnp.zeros((NCP, CMP_HID), F32)
    u1 = jnp.zeros((NCP, CMP_HID), F32)
    for pp in range(D_CMP // 2):
        xa = x_ref[pl.ds(2 * pp, NCP, stride=D_CMP), :]
        xb = x_ref[pl.ds(2 * pp + 1, NCP, stride=D_CMP), :]
        xp = jnp.concatenate([xa, xb], axis=1).astype(BF16)
        u0 = u0 + _dot(xp, w1_ref[pl.ds(pp * 2 * DH, 2 * DH), :])
        u1 = u1 + _dot(xp, w1_ref[pl.ds(D_CMP * DH + pp * 2 * DH, 2 * DH), :])
    h = _compress_hidden(u0, u1, cb_ref[...])
    o_ref[...] = _dot(_gelu_tanh(h).astype(BF16), w2_ref[...])


def _compress_prompt(P, w1, cb, w2, B, T):
    assert T // D_CMP == NCP
    kv0 = A_KV // DH
    return pl.pallas_call(
        _compress_prompt_body,
        grid=(B, 2, HKV),
        in_specs=[pl.BlockSpec((T, DH), lambda b, s, g: (b, kv0 + HKV * s + g)),
                  pl.BlockSpec((None, L_CMP * DH, CMP_HID), lambda b, s, g: (s, 0, 0)),
                  pl.BlockSpec((None, 1, CMP_HID), lambda b, s, g: (s, 0, 0)),
                  pl.BlockSpec((None, CMP_HID, DH), lambda b, s, g: (s, 0, 0))],
        out_specs=pl.BlockSpec((None, None, None, NCP, DH), lambda b, s, g: (b, s, g, 0, 0)),
        out_shape=jax.ShapeDtypeStruct((B, 2, HKV, NCP, DH), F32),
        compiler_params=_params(("arbitrary", "arbitrary", "arbitrary")),
    )(P, w1, cb, w2)


def _select_blocks(imp, tpos_tok, ns):
    blk = lax.broadcasted_iota(jnp.int32, (1, NSP), 1)
    cur = tpos_tok // L_SEL
    valid = blk <= cur
    forced = (blk == 0) | (blk == cur) | (blk == cur - 1)
    score = jnp.where(valid, jnp.where(forced, FORCE_SCORE, imp), -jnp.inf)
    k_top = float(min(N_TOP, ns))
    if score.shape[0] == LANES:
        nb = -(-ns // SUBLANES) * SUBLANES
        st = score.T[0:nb]
        rblk = lax.broadcasted_iota(jnp.int32, (nb, 1), 0)
        rank = jnp.zeros(st.shape, F32)
        for j in range(ns):
            sj = st[j:j + 1, :]
            rank = rank + jnp.where((sj > st) | ((sj == st) & (rblk > j)), 1.0, 0.0)
        top = jnp.where(rank < k_top, 1.0, 0.0)
        top = jnp.concatenate([top, jnp.zeros((NSP - nb, LANES), F32)], axis=0).T
        return jnp.where(valid, top, 0.0)
    rank = jnp.zeros(score.shape, F32)
    for j in range(ns):
        sj = score[:, j:j + 1]
        beats = (sj > score) | ((sj == score) & (blk > j))
        rank = rank + jnp.where(beats, 1.0, 0.0)
    return jnp.where(valid & (rank < float(min(N_TOP, ns))), 1.0, 0.0)


def _compressed_branch(qb, kc, vc, tpos, nc, scale):
    s = _dot_nt(qb, kc.astype(BF16)) * scale
    cidx = lax.broadcasted_iota(jnp.int32, (1, NCP), 1)
    cmask = (cidx < nc) & (cidx * D_CMP + (L_CMP - 1) <= tpos)
    p = _masked_softmax_pre(jnp.where(cmask, s, NEG))
    return p, _dot(p.astype(BF16), vc.astype(BF16))


def _keys_with_block_onehot(k_ref, koh_ref, k0, width):
    return jnp.concatenate([k_ref[pl.ds(k0, width), :].astype(BF16), koh_ref[pl.ds(k0, width), :]], axis=1)


def _nsa_prompt_body(q_ref, kc_ref, vc_ref, ks_ref, vs_ref, kw_ref, vw_ref, misc_ref, z_ref, c2s_ref, koh_ref,
                     o_ref, sc_ref, mx_ref, l_ref, acc_ref, *, tq, tk, T):
    qi = pl.program_id(1)
    t0 = pl.multiple_of(qi * tq, tq)
    R = GRP * tq
    scale = DH ** -0.5
    c1 = scale * LOG2E
    ns = -(-T // L_SEL)
    nc = T // D_CMP - CMP_R + 1
    n_top = min(N_TOP, ns)
    heads = range(HKV)

    def hd(g):
        return slice(g * DH, (g + 1) * DH)

    qbs = [jnp.concatenate([q_ref[:, (g * GRP + r) * DH:(g * GRP + r + 1) * DH] for r in range(GRP)],
                           axis=0).astype(BF16) for g in heads]
    tpos = t0 + (lax.broadcasted_iota(jnp.int32, (R, 1), 0) & (tq - 1))
    tpos_tok = t0 + lax.broadcasted_iota(jnp.int32, (tq, 1), 0)
    blk = lax.broadcasted_iota(jnp.int32, (1, NSP), 1)

    nwt = WINDOW // tq + 1
    tiles, starts = [[] for _ in heads], []
    for i in range(nwt):
        k0 = t0 - WINDOW + i * tq
        k0c = pl.multiple_of(jnp.maximum(k0, 0), tq)
        kpos = k0 + lax.broadcasted_iota(jnp.int32, (1, tq), 1)
        for g in heads:
            s = _dot_nt(qbs[g], kw_ref[pl.ds(k0c, tq), hd(g)].astype(BF16))
            if i == 0:
                s = jnp.where((kpos > tpos - WINDOW) & (kpos >= 0), s, NEG)
            elif i == nwt - 1:
                s = jnp.where(kpos <= tpos, s, NEG)
            else:
                s = jnp.where(k0 >= 0, s, NEG)
            tiles[g].append(s)
        starts.append(k0c)
    es_w = []
    for g in heads:
        s = jnp.concatenate(tiles[g], axis=1)
        es_w.append(jnp.exp2((s - jnp.max(s, axis=-1, keepdims=True)) * c1))
    o_ws = [None for _ in heads]
    for i in range(nwt):
        for g in heads:
            pv = _dot(es_w[g][:, i * tq:(i + 1) * tq].astype(BF16), vw_ref[pl.ds(starts[i], tq), hd(g)].astype(BF16))
            o_ws[g] = pv if i == 0 else o_ws[g] + pv
    o_ws = [o_ws[g] / jnp.sum(es_w[g], axis=-1, keepdims=True) for g in heads]

    comp = [_compressed_branch(qbs[g], kc_ref[g], vc_ref[g], tpos, nc, scale) for g in heads]
    imps = []
    for g in heads:
        p_c = comp[g][0]
        psum = p_c[0:tq]
        for r in range(1, GRP):
            psum = psum + p_c[r * tq:(r + 1) * tq]
        imps.append(_dot_exact_rhs(psum, c2s_ref[...]))

    def all_valid():
        v = jnp.where(blk <= tpos_tok // L_SEL, 1.0, 0.0)
        return tuple(v for _ in heads)

    sels = lax.cond(t0 + tq <= n_top * L_SEL, all_valid,
                    lambda: tuple(_select_blocks(imps[g], tpos_tok, ns) for g in heads))

    q_augs, q_earlys = [], []
    for g in heads:
        sel_bias = (sels[g] - 1.0) * (-NEG)
        early_bias = jnp.where(blk < lax.div(t0, L_SEL), sel_bias, NEG)
        q_augs.append(jnp.concatenate([qbs[g], jnp.concatenate([sel_bias.astype(BF16)] * GRP, axis=0)], axis=1))
        q_earlys.append(jnp.concatenate([qbs[g], jnp.concatenate([early_bias.astype(BF16)] * GRP, axis=0)], axis=1))

    def keys_aug(k0, width, g):
        return jnp.concatenate([ks_ref[pl.ds(k0, width), hd(g)].astype(BF16), koh_ref[pl.ds(k0, width), :]], axis=1)

    kpos_d = t0 + lax.broadcasted_iota(jnp.int32, (1, tq), 1)
    s_diags = [jnp.where(kpos_d <= tpos, _dot_nt(q_augs[g], keys_aug(t0, tq, g)), NEG) for g in heads]
    n_early = lax.div(t0 + tk - 1, tk)
    for g in heads:
        mx_ref[g] = s_diags[g]

    def pass1(kt, carry):
        k0 = pl.multiple_of(kt * tk, tk)
        ss = [_dot_nt(q_earlys[g], keys_aug(k0, tk, g)) for g in heads]
        old = [mx_ref[g] for g in heads]
        for g in heads:
            mx = ss[g][:, 0:LANES]
            for c in range(1, tk // LANES):
                mx = jnp.maximum(mx, ss[g][:, c * LANES:(c + 1) * LANES])
            sc_ref[g, kt] = ss[g]
            mx_ref[g] = jnp.maximum(old[g], mx)
        return carry

    lax.fori_loop(0, n_early, pass1, 0)
    m2s = [jnp.broadcast_to(jnp.max(mx_ref[g], axis=-1, keepdims=True) * c1, (R, LANES)) for g in heads]
    e_ds = [jnp.exp2(s_diags[g] * c1 - m2s[g]) for g in heads]
    pv_ds = [_dot(e_ds[g].astype(BF16), vs_ref[pl.ds(t0, tq), hd(g)].astype(BF16)) for g in heads]
    for g in heads:
        mx_ref[g] = m2s[g]
        l_ref[g] = e_ds[g]
        acc_ref[g] = pv_ds[g]

    def pass2(kt, carry):
        k0 = pl.multiple_of(kt * tk, tk)
        ss = [sc_ref[g, kt] for g in heads]
        m2 = [mx_ref[g] for g in heads]
        l_old = [l_ref[g] for g in heads]
        acc_old = [acc_ref[g] for g in heads]
        ebs, lsums = [], []
        for g in heads:
            es = [jnp.exp2(ss[g][:, c * LANES:(c + 1) * LANES] * c1 - m2[g]) for c in range(tk // LANES)]
            lsum = es[0]
            for ec in es[1:]:
                lsum = lsum + ec
            lsums.append(lsum)
            ebs.append(jnp.concatenate(es, axis=1).astype(BF16))
        pvs = [_dot(ebs[g], vs_ref[pl.ds(k0, tk), hd(g)].astype(BF16)) for g in heads]
        for g in heads:
            l_ref[g] = l_old[g] + lsums[g]
            acc_ref[g] = acc_old[g] + pvs[g]
        return carry

    lax.fori_loop(0, n_early, pass2, 0)

    outs = []
    for g in heads:
        o_s = acc_ref[g] / jnp.sum(l_ref[g], axis=-1, keepdims=True)
        gm = _sigmoid(misc_ref[:, g * LANES:g * LANES + GRP * 3])

        def gate(j):
            return jnp.concatenate([gm[:, r * 3 + j:r * 3 + j + 1] for r in range(GRP)], axis=0)

        outs.append(gate(0) * comp[g][1] + gate(1) * o_s + gate(2) * o_ws[g])
    for g in heads:
        for r in range(GRP):
            c = (g * GRP + r) * DH
            o_ref[:, c:c + DH] = (outs[g][r * tq:(r + 1) * tq] * _silu(z_ref[:, c:c + DH])).astype(BF16)


def _nsa_prompt(P, kcv, c2s, koh, B, T, tq, tk):
    assert T % tq == 0 and tq & (tq - 1) == 0 and tk % tq == 0 and T % tk == 0 and WINDOW % tq == 0
    nq = T // tq
    qw = H_A * DH
    kw = HKV * DH
    R = GRP * tq

    def kvspec(slot):
        return pl.BlockSpec((T, kw), lambda b, i: (b, A_KV // kw + slot))

    return pl.pallas_call(
        functools.partial(_nsa_prompt_body, tq=tq, tk=tk, T=T),
        grid=(B, nq),
        in_specs=[pl.BlockSpec((tq, qw), lambda b, i: (b * nq + i, A_Q // qw)),
                  pl.BlockSpec((None, None, HKV, NCP, DH), lambda b, i: (b, 0, 0, 0, 0)),
                  pl.BlockSpec((None, None, HKV, NCP, DH), lambda b, i: (b, 1, 0, 0, 0)),
                  kvspec(2), kvspec(3), kvspec(4), kvspec(5),
                  pl.BlockSpec((tq, HKV * LANES), lambda b, i: (b * nq + i, A_MISC // (HKV * LANES))),
                  pl.BlockSpec((tq, qw), lambda b, i: (b * nq + i, A_ZA // qw)),
                  pl.BlockSpec((NCP, NSP), lambda b, i: (0, 0)),
                  pl.BlockSpec((T, NSP), lambda b, i: (0, 0))],
        out_specs=pl.BlockSpec((tq, qw), lambda b, i: (b * nq + i, 0)),
        out_shape=jax.ShapeDtypeStruct((B * T, qw), BF16),
        scratch_shapes=[pltpu.VMEM((HKV, T // tk, R, tk), F32), pltpu.VMEM((HKV, R, LANES), F32),
                        pltpu.VMEM((HKV, R, LANES), F32), pltpu.VMEM((HKV, R, DH), F32)],
        compiler_params=_params(("arbitrary", "arbitrary")),
    )(P, kcv, kcv, P, P, P, P, P, P, c2s, koh)


def _pad_rows(x, rows):
    return jnp.concatenate([x, jnp.zeros((rows - x.shape[0], x.shape[1]), x.dtype)], axis=0)


def _nsa_sample_body(pt_ref, cache_ref, win_ref, q_ref, kvr_ref, kvw_ref, misc_ref, z_ref, w1_ref, cb_ref, w2_ref,
                     c2s_ref, koh_ref, o_ref, wout_ref, pbuf, sem, *, n_pages, page, past_len, dec_seq, wb):
    b = pl.program_id(0)
    scale = DH ** -0.5
    c1 = scale * LOG2E
    L = past_len + dec_seq
    ns = -(-L // L_SEL)
    nc = L // D_CMP - CMP_R + 1
    R = GRP * TP
    per_page = page // D_CMP
    nch = N_KV_SLOTS * HKV
    grp_rows = D_CMP * nch
    pitch = grp_rows + 1

    def page_copies(seq, slot):
        cps = []
        for j in range(n_pages):
            pid = pt_ref[seq * n_pages + j]
            for n in range(per_page):
                cps.append(pltpu.make_async_copy(
                    cache_ref.at[pid, pl.ds(n * grp_rows, grp_rows), :],
                    pbuf.at[slot, pl.ds((j * per_page + n) * pitch, grp_rows), :], sem.at[slot]))
        return cps

    slot = lax.rem(b, 2)

    @pl.when(b == 0)
    def _():
        for cp in page_copies(0, 0):
            cp.start()

    for cp in page_copies(b, slot):
        cp.wait()

    last = pl.num_programs(0) - 1
    next_copies = page_copies(jnp.minimum(b + 1, last), 1 - slot)
    copies_per_group = len(next_copies) // (2 * (D_CMP // 2))

    def gather(p, ch):
        return pbuf[slot, pl.ds(p * nch + ch, NCP, stride=pitch), :]

    trow = lax.broadcasted_iota(jnp.int32, (R, 1), 0) & (TP - 1)
    tpos = past_len + trow
    tpos_tok = past_len + lax.broadcasted_iota(jnp.int32, (TP, 1), 0)
    lane = lax.broadcasted_iota(jnp.int32, (1, LANES), 1)
    new_pos = past_len + lane
    new_ok = (new_pos <= tpos) & (new_pos < L)
    new_blk = (past_len + lax.broadcasted_iota(jnp.int32, (LANES, 1), 0)) // L_SEL
    new_oh = jnp.where(lax.broadcasted_iota(jnp.int32, (LANES, NSP), 1) == new_blk, 1.0, 0.0).astype(BF16)
    wpos = (past_len - wb) + lax.broadcasted_iota(jnp.int32, (1, wb), 1)
    wch = 2 * HKV

    qbs, o_ws, raw = [], [], []
    for g in range(HKV):
        qb = jnp.concatenate([q_ref[:, (g * GRP + r) * DH:(g * GRP + r + 1) * DH] for r in range(GRP)],
                             axis=0).astype(BF16)
        kw_col = g * DH
        vw_col = (HKV + g) * DH
        s_w = _dot_nt(qb, win_ref[pl.ds(g, wb, stride=wch), :].astype(BF16))
        s_w = jnp.where((wpos <= tpos) & (wpos > tpos - WINDOW), s_w, NEG)
        s_n = _dot_nt(qb, _pad_rows(kvw_ref[:, kw_col:kw_col + DH], LANES).astype(BF16))
        s_n = jnp.where(new_ok & (new_pos > tpos - WINDOW), s_n, NEG)
        s = jnp.concatenate([s_w, s_n], axis=1)
        e = jnp.exp2((s - jnp.max(s, axis=-1, keepdims=True)) * c1)
        o_w = (_dot(e[:, :wb].astype(BF16), win_ref[pl.ds(HKV + g, wb, stride=wch), :].astype(BF16))
               + _dot(e[:, wb:].astype(BF16), _pad_rows(kvw_ref[:, vw_col:vw_col + DH], LANES).astype(BF16)))
        o_ws.append(o_w / jnp.sum(e, axis=-1, keepdims=True))
        ksel_col = (2 * HKV + g) * DH
        pieces = [_dot_nt(qb, gather(p, 2 * HKV + g).astype(BF16)) for p in range(D_CMP)]
        pieces.append(_dot_nt(qb, _pad_rows(kvr_ref[:, ksel_col:ksel_col + DH], LANES).astype(BF16)))
        raw.append(pieces)
        qbs.append(qb)

    kcv = {}
    for s in range(2):
        u0 = jnp.zeros((HKV * NCP, CMP_HID), F32)
        u1 = jnp.zeros((HKV * NCP, CMP_HID), F32)
        for pp in range(D_CMP // 2):
            xs = []
            for g in range(HKV):
                ch = s * HKV + g
                xs.append(jnp.concatenate([gather(2 * pp, ch), gather(2 * pp + 1, ch)], axis=1))
            xp = jnp.concatenate(xs, axis=0).astype(BF16)
            u0 = u0 + _dot(xp, w1_ref[s, pl.ds(pp * 2 * DH, 2 * DH), :])
            u1 = u1 + _dot(xp, w1_ref[s, pl.ds(D_CMP * DH + pp * 2 * DH, 2 * DH), :])
            grp = s * (D_CMP // 2) + pp
            for cp in next_copies[grp * copies_per_group:(grp + 1) * copies_per_group]:
                cp.start()
        for g in range(HKV):
            h = _compress_hidden(u0[g * NCP:(g + 1) * NCP], u1[g * NCP:(g + 1) * NCP], cb_ref[s])
            kcv[s, g] = _dot(_gelu_tanh(h).astype(BF16), w2_ref[s])

    heads = range(HKV)
    comp = [_compressed_branch(qbs[g], kcv[0, g], kcv[1, g], tpos, nc, scale) for g in heads]
    imps = []
    for g in heads:
        p_c = comp[g][0]
        psum = p_c[0:TP]
        for r in range(1, GRP):
            psum = psum + p_c[r * TP:(r + 1) * TP]
        imps.append(_dot_exact_rhs(psum, c2s_ref[...]))
    sels = [_select_blocks(imps[g], tpos_tok, ns) for g in heads]
    biases = [jnp.concatenate([((sels[g] - 1.0) * (-NEG)).astype(BF16)] * GRP, axis=0) for g in heads]
    bias_past = [_dot_nt(biases[g], koh_ref[...]) for g in heads]
    bias_new = [jnp.where(new_ok, _dot_nt(biases[g], new_oh), NEG) for g in heads]
    es = []
    for g in heads:
        s = jnp.concatenate([pc + bias_past[g] for pc in raw[g][:D_CMP]] + [raw[g][D_CMP] + bias_new[g]], axis=1)
        es.append(jnp.exp2((s - jnp.max(s, axis=-1, keepdims=True)) * c1))
    o_ss = []
    for g in heads:
        vsel_col = (3 * HKV + g) * DH
        o_ss.append(_dot(es[g][:, D_CMP * NCP:].astype(BF16),
                         _pad_rows(kvr_ref[:, vsel_col:vsel_col + DH], LANES).astype(BF16)))
    for p in range(D_CMP):
        for g in heads:
            o_ss[g] = o_ss[g] + _dot(es[g][:, p * NCP:(p + 1) * NCP].astype(BF16),
                                     gather(p, 3 * HKV + g).astype(BF16))
    outs = []
    for g in heads:
        o_s = o_ss[g] / jnp.sum(es[g], axis=-1, keepdims=True)
        gm = _sigmoid(misc_ref[:, g * LANES:g * LANES + GRP * 3])

        def gate(j):
            return jnp.concatenate([gm[:, r * 3 + j:r * 3 + j + 1] for r in range(GRP)], axis=0)

        outs.append(gate(0) * comp[g][1] + gate(1) * o_s + gate(2) * o_ws[g])

    for g in range(HKV):
        for r in range(GRP):
            c = (g * GRP + r) * DH
            o_ref[:, c:c + DH] = outs[g][r * TP:(r + 1) * TP] * _silu(z_ref[:, c:c + DH])

    wout_ref[pl.ds(0, (wb - dec_seq) * wch), :] = win_ref[pl.ds(dec_seq * wch, (wb - dec_seq) * wch), :]
    for t in range(dec_seq):
        for ch in range(wch):
            wout_ref[pl.ds((wb - dec_seq + t) * wch + ch, 1), :] = kvw_ref[t:t + 1, ch * DH:(ch + 1) * DH]

    @pl.when(b == last)
    def _():
        for cp in next_copies:
            cp.wait()


def _nsa_sample(Ps, cache, cache_win, page_table, w1, cb, w2, c2s, dec_seq):
    NB, n_pages = page_table.shape
    n_pool, prow, _ = cache.shape
    page = prow // (N_KV_SLOTS * HKV)
    wb = cache_win.shape[1] // (2 * HKV)
    past_len = n_pages * page
    L = past_len + dec_seq
    assert wb == WINDOW and (L // D_CMP) == NCP and NCP * D_CMP == past_len and dec_seq <= TP
    assert -(-L // L_SEL) < NSP and page % D_CMP == 0 and (dec_seq * 2 * HKV) % SUBLANES == 0
    qw = H_A * DH
    nch = N_KV_SLOTS * HKV
    koh = jnp.asarray(np.arange(NCP)[:, None] * D_CMP // L_SEL == np.arange(NSP)[None, :], dtype=BF16)

    grid_spec = pltpu.PrefetchScalarGridSpec(
        num_scalar_prefetch=1,
        grid=(NB,),
        in_specs=[
            pl.BlockSpec(memory_space=pl.ANY),
            pl.BlockSpec((None, wb * 2 * HKV, DH), lambda b, pt: (b, 0, 0)),
            pl.BlockSpec((TP, qw), lambda b, pt: (b, A_Q // qw)),
            pl.BlockSpec((TP, 4 * HKV * DH), lambda b, pt: (b, A_KV // (4 * HKV * DH))),
            pl.BlockSpec((TP, 2 * HKV * DH), lambda b, pt: (b, (A_KV + 4 * HKV * DH) // (2 * HKV * DH))),
            pl.BlockSpec((TP, HKV * LANES), lambda b, pt: (b, A_MISC // (HKV * LANES))),
            pl.BlockSpec((TP, qw), lambda b, pt: (b, A_ZA // qw)),
            pl.BlockSpec((2, L_CMP * DH, CMP_HID), lambda b, pt: (0, 0, 0)),
            pl.BlockSpec((2, 1, CMP_HID), lambda b, pt: (0, 0, 0)),
            pl.BlockSpec((2, CMP_HID, DH), lambda b, pt: (0, 0, 0)),
            pl.BlockSpec((NCP, NSP), lambda b, pt: (0, 0)),
            pl.BlockSpec((NCP, NSP), lambda b, pt: (0, 0))],
        out_specs=[pl.BlockSpec((TP, qw), lambda b, pt: (b, 0)),
                   pl.BlockSpec((None, wb * 2 * HKV, DH), lambda b, pt: (b, 0, 0))],
        scratch_shapes=[pltpu.VMEM((2, NCP * (D_CMP * nch + 1), DH), F32), pltpu.SemaphoreType.DMA((2,))],
    )
    return pl.pallas_call(
        functools.partial(_nsa_sample_body, n_pages=n_pages, page=page, past_len=past_len, dec_seq=dec_seq, wb=wb),
        grid_spec=grid_spec,
        out_shape=[jax.ShapeDtypeStruct((NB * TP, qw), F32),
                   jax.ShapeDtypeStruct((NB, wb * 2 * HKV, DH), F32)],
        compiler_params=_params(("arbitrary",)),
    )(page_table.reshape(-1), cache, cache_win, Ps, Ps, Ps, Ps, Ps, w1, cb, w2, c2s, koh)


def _rec_chunk(q, k, v, la, gate, S, C, last_row, nh, K, V):
    sb = min(16, C)
    ri = lax.broadcasted_iota(jnp.int32, (C, C), 0)
    cj = lax.broadcasted_iota(jnp.int32, (C, C), 1)
    causal = cj <= ri
    tri = jnp.where(causal, 1.0, 0.0).astype(BF16)
    b = _dot_exact_lhs(tri, la)
    qe = (q * jnp.exp(b)).astype(BF16)
    qis, kis = [], []
    k_ref_prev, c_prev = None, None
    for i in range(C // sb):
        r0 = i * sb
        ci = b[r0 + sb // 2:r0 + sb // 2 + 1, :]
        qis.append((q[r0:r0 + sb] * jnp.exp(b[r0:r0 + sb] - ci)).astype(BF16))
        k_new = k[r0:r0 + sb] * jnp.exp(jnp.minimum(ci - b[r0:r0 + sb], 80.0))
        k_scaled = k_new if i == 0 else jnp.concatenate([k_ref_prev * jnp.exp(ci - c_prev), k_new], axis=0)
        k_ref_prev, c_prev = k_scaled, ci
        rest = C - r0 - sb
        kis.append((k_scaled if rest == 0 else
                    jnp.concatenate([k_scaled, jnp.zeros((rest, k.shape[1]), F32)], axis=0)).astype(BF16))
    b_last = b[last_row:last_row + 1, :]
    rowi = lax.broadcasted_iota(jnp.int32, (C, 1), 0)
    kd = jnp.where(rowi <= last_row, k * jnp.exp(jnp.minimum(b_last - b, 0.0)), 0.0)
    stack = jnp.concatenate([kd, jnp.broadcast_to(b_last, (SUBLANES, nh * K)),
                             jnp.zeros((LANES - C - SUBLANES, nh * K), F32)], axis=0)
    vb = v.astype(BF16)
    v_pad = jnp.concatenate([vb, jnp.zeros((LANES - C, nh * V), BF16)], axis=0)
    outs, s_new = [], []
    for h in range(nh):
        ks = slice(h * K, (h + 1) * K)
        vs = slice(h * V, (h + 1) * V)
        o = _dot(qe[:, ks], S[h].astype(BF16))
        rows = [_dot_nt(qi[:, ks], ki[:, ks]) for qi, ki in zip(qis, kis)]
        att = rows[0] if len(rows) == 1 else jnp.concatenate(rows, axis=0)
        att = jnp.where(causal, att, 0.0)
        o = o + _dot(att.astype(BF16), vb[:, vs])
        stack_t = stack[:, ks].T
        a_col = jnp.exp(stack_t[:, C:C + 1])
        s_new.append(a_col * S[h] + _dot(stack_t.astype(BF16), v_pad[:, vs]))
        ms = jnp.mean(o * o, axis=-1, keepdims=True)
        outs.append(o * lax.rsqrt(ms + EPS) * gate[:, vs])
    return outs, s_new


def _rec_body(*refs, variant, nh, K, V, C, TB, SB, last_row, has_s0, layer_idx):
    refs = list(refs)
    q_ref, k_ref, v_ref, z_ref = refs[:4]
    pos = 4
    if variant == "gla":
        misc_ref, w2_ref, gb_ref = refs[pos:pos + 3]
        pos += 3
    else:
        lb_ref = refs[pos]
        pos += 1
    gn_ref = refs[pos]
    pos += 1
    if has_s0:
        s0_ref = refs[pos]
        pos += 1
    o_ref, s_ref = refs[pos:pos + 2]
    n_chunks = TB // C
    single = n_chunks == 1 and has_s0

    if not single:
        @pl.when(pl.program_id(2) == 0)
        def _():
            if has_s0:
                s_ref[...] = s0_ref[...]
            else:
                s_ref[...] = jnp.zeros(s_ref.shape, F32)

    if variant == "hgrn":
        lg = lb_ref[...]
        e = jnp.exp(lg - jnp.max(lg, axis=0, keepdims=True))
        prob = e / jnp.sum(e, axis=0, keepdims=True)
        lb = prob[1:2]
        for i in range(2, layer_idx + 1):
            lb = lb + prob[i:i + 1]
        if SB > 1:
            lb = jnp.concatenate([lb] * SB, axis=1)
    gn_all = jnp.concatenate([gn_ref[...]] * (nh * SB), axis=1)

    def chunk(c0):
        def rows(s):
            return pl.ds(s * TB + c0, C)

        def cat(f):
            return f(0) if SB == 1 else jnp.concatenate([f(s) for s in range(SB)], axis=1)

        qr = cat(lambda s: q_ref[rows(s), :])
        kr = cat(lambda s: k_ref[rows(s), :])
        v = cat(lambda s: v_ref[rows(s), :])
        gate = _silu(cat(lambda s: z_ref[rows(s), :])) * gn_all
        if variant == "gla":
            w2 = w2_ref[...].astype(BF16)
            zg = cat(lambda s: _dot(misc_ref[rows(s), :].astype(BF16), w2) + gb_ref[...])
            q, k, la = qr * (K ** -0.5), kr, _log_sigmoid(zg) / GLA_GATE_NORM
        else:
            t = jnp.exp(-jnp.abs(kr))
            r = 1.0 / (1.0 + t)
            tr = t * r
            nonneg = kr >= 0.0
            sig = jnp.where(nonneg, r, tr)
            nsig = jnp.where(nonneg, tr, r)
            q, k, la = _silu(qr), (1.0 - lb) * nsig, jnp.log(lb + (1.0 - lb) * sig)
        src = s0_ref if single else s_ref
        states = [src[s, h] for s in range(SB) for h in range(nh)]
        outs, s_new = _rec_chunk(q, k, v, la, gate, states, C, last_row, SB * nh, K, V)
        for s in range(SB):
            for h in range(nh):
                o_ref[rows(s), h * V:(h + 1) * V] = outs[s * nh + h].astype(o_ref.dtype)
                s_ref[s, h] = s_new[s * nh + h]

    if n_chunks == 1:
        chunk(0)
    else:
        def body(ci, carry):
            chunk(pl.multiple_of(ci * C, C))
            return carry

        lax.fori_loop(0, n_chunks, body, 0, unroll=8)


def _recurrence(P, cols, extra, gn, s0, *, variant, B, T, H, K, V, C, TB, nh, last_row, out_dtype, SB=1,
                layer_idx=1):
    cq, ck, cv, cz = cols
    assert H % nh == 0 and T % TB == 0 and TB % C == 0 and B % SB == 0
    nt = T // TB
    assert SB == 1 or nt == 1

    def colspec(c0, w):
        assert c0 % (nh * w) == 0
        return pl.BlockSpec((SB * TB, nh * w), lambda b, j, t: (b * nt + t, c0 // (nh * w) + j))

    in_specs = [colspec(cq, K), colspec(ck, K), colspec(cv, V), colspec(cz, V)]
    args = [P, P, P, P]
    if variant == "gla":
        w2p, gb = extra
        in_specs += [pl.BlockSpec((SB * TB, LANES), lambda b, j, t: (b * nt + t, A_MISC // LANES)),
                     pl.BlockSpec((LANES, nh * K), lambda b, j, t: (0, j)),
                     pl.BlockSpec((1, nh * K), lambda b, j, t: (0, j))]
        args += [P, w2p, gb]
    else:
        (lb_logits,) = extra
        in_specs += [pl.BlockSpec((lb_logits.shape[0], nh * K), lambda b, j, t: (0, j))]
        args += [lb_logits]
    in_specs += [pl.BlockSpec((1, V), lambda b, j, t: (0, 0))]
    args += [gn]
    st_spec = pl.BlockSpec((None, SB, nh, K, V), lambda b, j, t: (0, b, j, 0, 0))
    if s0 is not None:
        in_specs += [st_spec]
        args += [s0]
    return pl.pallas_call(
        functools.partial(_rec_body, variant=variant, nh=nh, K=K, V=V, C=C, TB=TB, SB=SB, last_row=last_row,
                          has_s0=s0 is not None, layer_idx=layer_idx),
        grid=(B // SB, H // nh, nt),
        in_specs=in_specs,
        out_specs=[pl.BlockSpec((SB * TB, nh * V), lambda b, j, t: (b * nt + t, j)), st_spec],
        out_shape=[jax.ShapeDtypeStruct((B * T, H * V), out_dtype),
                   jax.ShapeDtypeStruct((1, B, H, K, V), F32)],
        compiler_params=_params(("arbitrary", "arbitrary", "arbitrary")),
    )(*args)


def _layout_a_w_in(w):
    off = np.concatenate([[0], np.cumsum(A_SIZES)])
    q, kv, gbr, za, qb, kb, vb, lr, zb = [w[:, off[i]:off[i + 1]] for i in range(len(A_SIZES))]
    D = w.shape[0]
    z = lambda n: jnp.zeros((D, n), w.dtype)
    misc0 = jnp.concatenate([gbr[:, :GRP * 3], lr, z(LANES - GRP * 3 - GLA_LR)], axis=1)
    misc1 = jnp.concatenate([gbr[:, GRP * 3:], z(LANES - GRP * 3)], axis=1)
    out = jnp.concatenate([q, vb, zb, za, qb, kb, kv, misc0, misc1, z(NA - A_MISC - HKV * LANES)], axis=1)
    assert out.shape[1] == NA
    return out.astype(BF16)


def _cmp_to_sel():
    c_start = np.arange(NCP)[:, None] * D_CMP
    s_start = np.arange(NSP)[None, :] * L_SEL
    overlap = np.clip(np.minimum(c_start + L_CMP, s_start + L_SEL) - np.maximum(c_start, s_start), 0, None)
    return jnp.asarray(overlap / D_CMP, dtype=BF16)


def kernel(x_prompt, x_sample, cache_kv, cache_win, state_gla, state_hgrn, page_table, a_norm, a_w_in, a_gla_w2,
           a_gla_b, a_gla_gn, a_cmp_pe, a_cmp_w1, a_cmp_b1, a_cmp_w2, a_w_out, c_norm, c_w_in, c_lb_logits, c_gn,
           c_w_out, final_norm):
    B, T, D = x_prompt.shape
    NB, TS, _ = x_sample.shape
    n_pool, page = cache_kv.shape[1], cache_kv.shape[2]
    wb = cache_win.shape[2]
    assert a_norm.shape[0] == 1 and c_norm.shape[0] == 1 and c_lb_logits.shape[0] == 2

    wa = _layout_a_w_in(a_w_in[0])
    wc = c_w_in[0].astype(BF16)
    wao = a_w_out[0].astype(BF16)
    wco = c_w_out[0].astype(BF16)
    w1 = a_cmp_w1[0].astype(BF16)
    w2 = a_cmp_w2[0].astype(BF16)
    pe = a_cmp_pe[0].reshape(2, 1, L_CMP * DH)
    b1 = a_cmp_b1[0].reshape(2, 1, CMP_HID)
    w2p = jnp.zeros((LANES, H_B * DK_B), F32).at[MISC_LR:MISC_LR + GLA_LR, :].set(a_gla_w2[0])
    gb = a_gla_b[0].reshape(1, H_B * DK_B)
    c2s = _cmp_to_sel()
    a_nw = a_norm[0].reshape(1, D)
    c_nw = c_norm[0].reshape(1, D)
    f_nw = final_norm.reshape(1, D)
    gla_gn = a_gla_gn[0].reshape(1, DV_B)
    hg_gn = c_gn[0].reshape(1, DV_C)

    xp = x_prompt.reshape(B * T, D)
    xs = jnp.pad(x_sample, ((0, 0), (0, TP - TS), (0, 0))).reshape(NB * TP, D)

    cb = _cmp_bias(pe, w1, b1)

    Pp = _norm_matmul(xp, a_nw, wa, tm=1024, tn=1024)
    kcv = _compress_prompt(Pp, w1, cb, w2, B, T)
    koh = jnp.asarray(np.arange(T)[:, None] // L_SEL == np.arange(NSP)[None, :], dtype=BF16)
    oa_p = _nsa_prompt(Pp, kcv, c2s, koh, B, T, tq=128, tk=512)
    ob_p, gla_p = _recurrence(Pp, (A_QB, A_KB, A_VB, A_ZB), (w2p, gb), gla_gn, None, variant="gla", B=B, T=T,
                              H=H_B, K=DK_B, V=DV_B, C=64, TB=512, nh=4, last_row=63, out_dtype=BF16)
    y1p = _out_proj(oa_p, 0, ob_p, 0, wao, xp, a_nw, tm=512, final_norm=False)

    Ps = _norm_matmul(xs, a_nw, wa, tm=NB * TP, tn=1024)
    cache2 = cache_kv[0].reshape(n_pool, page * N_KV_SLOTS * HKV, DH)
    win2 = cache_win[0].reshape(NB, wb * 2 * HKV, DH)
    oa_s, win_out = _nsa_sample(Ps, cache2, win2, page_table, w1, cb, w2, c2s, TS)
    ob_s, gla_s = _recurrence(Ps, (A_QB, A_KB, A_VB, A_ZB), (w2p, gb), gla_gn, state_gla, variant="gla", B=NB, T=TP,
                              H=H_B, K=DK_B, V=DV_B, C=TP, TB=TP, nh=H_B, last_row=TS - 1, out_dtype=F32,
                              SB=math.gcd(NB, 8))
    y1s = _out_proj(oa_s, 0, ob_s, 0, wao, xs, a_nw, tm=min(512, NB * TP), final_norm=False)

    Pc = _norm_matmul(y1p, c_nw, wc, tm=1024, tn=1024)
    hc = H_C * DK_C
    oc_p, hg_p = _recurrence(Pc, (0, hc, 2 * hc, 3 * hc), (c_lb_logits,), hg_gn, None, variant="hgrn", B=B, T=T,
                             H=H_C, K=DK_C, V=DV_C, C=64, TB=512, nh=8, last_row=63, out_dtype=BF16)
    y_prompt = _out_proj(oc_p, 0, oc_p, 1, wco, y1p, f_nw, tm=512, final_norm=True)

    Pcs = _norm_matmul(y1s, c_nw, wc, tm=NB * TP, tn=1024)
    oc_s, hg_s = _recurrence(Pcs, (0, hc, 2 * hc, 3 * hc), (c_lb_logits,), hg_gn, state_hgrn, variant="hgrn", B=NB,
                             T=TP, H=H_C, K=DK_C, V=DV_C, C=TP, TB=TP, nh=H_C, last_row=TS - 1, out_dtype=F32,
                             SB=math.gcd(NB, 4))
    y_sample = _out_proj(oc_s, 0, oc_s, 1, wco, y1s, f_nw, tm=min(512, NB * TP), final_norm=True)

    kvw = 6 * HKV * DH
    nrw = N_KV_SLOTS * HKV * DH
    Pp3 = Pp.reshape(B, T, NA)
    Ps3 = Ps.reshape(NB, TP, NA)
    kv_p = Pp3[:, :, A_KV:A_KV + nrw].reshape(1, B, T, N_KV_SLOTS, HKV, DH)
    kv_s = Ps3[:, :TS, A_KV:A_KV + nrw].reshape(1, NB, TS, N_KV_SLOTS, HKV, DH)
    keep = min(WINDOW, T)
    win_p = Pp3[:, T - keep:, A_KV + nrw:A_KV + kvw].reshape(1, B, keep, 2, HKV, DH)
    win_s = win_out.reshape(1, NB, wb, 2, HKV, DH)
    return (y_prompt.reshape(B, T, D), y_sample.reshape(NB, TP, D)[:, :TS], kv_p, kv_s, win_p, win_s,
            gla_p, gla_s, hg_p, hg_s)
```

```python
import functools
import math

import numpy as np
import jax
import jax.numpy as jnp
from jax import lax
from jax.experimental import pallas as pl
from jax.experimental.pallas import tpu as pltpu

F32 = jnp.float32
BF16 = jnp.bfloat16

D_MODEL = 2048
DH = 128
H_A = 8
HKV = 2
GRP = H_A // HKV
L_CMP = 32
D_CMP = 16
CMP_R = L_CMP // D_CMP
CMP_HID = 2 * DH
L_SEL = 64
N_TOP = 16
WINDOW = 512
N_KV_SLOTS = 4
H_B = 4
DK_B = 128
DV_B = 256
GLA_LR = 16
GLA_GATE_NORM = 16.0
DK_C = 128
H_C = 16
DV_C = 128
EPS = 1e-6
FORCE_SCORE = 1e9
A_SIZES = (H_A * DH, 6 * HKV * DH, 3 * H_A, H_A * DH, H_B * DK_B, H_B * DK_B, H_B * DV_B, GLA_LR, H_B * DV_B)

LANES = 128
SUBLANES = 8
VMEM_LIMIT = 56 * 1024 * 1024

A_Q = 0
A_VB = 1024
A_ZB = 2048
A_ZA = 3072
A_QB = 4096
A_KB = 4608
A_KV = 5120
A_MISC = 6656
NA = 7168
MISC_LR = GRP * 3

NCP = 128
NSP = 128
TP = 8
NEG = -1e30
LOG2E = 1.4426950408889634


def _dot(a, b):
    return jnp.dot(a, b, preferred_element_type=F32)


def _dot_nt(a, b):
    return lax.dot_general(a, b, (((1,), (1,)), ((), ())), preferred_element_type=F32)


def _split3(a):
    hi = a.astype(BF16)
    r = a - hi.astype(F32)
    mid = r.astype(BF16)
    lo = (r - mid.astype(F32)).astype(BF16)
    return hi, mid, lo


def _dot_exact_rhs(a, b_exact):
    hi, mid, lo = _split3(a)
    return _dot(hi, b_exact) + _dot(mid, b_exact) + _dot(lo, b_exact)


def _dot_exact_lhs(a_exact, b):
    hi, mid, lo = _split3(b)
    return _dot(a_exact, hi) + _dot(a_exact, mid) + _dot(a_exact, lo)


def _sigmoid(x):
    return 1.0 / (1.0 + jnp.exp(-x))


def _silu(x):
    h = 0.5 * x
    return h + h * jnp.tanh(h)


def _log_sigmoid(x):
    return jnp.minimum(x, 0.0) - jnp.log(1.0 + jnp.exp(-jnp.abs(x)))


def _gelu_tanh(x):
    return 0.5 * x * (1.0 + jnp.tanh(np.sqrt(2.0 / np.pi) * (x + 0.044715 * (x * x * x))))


def _masked_softmax_pre(sm):
    m = jnp.max(sm, axis=-1, keepdims=True)
    e = jnp.where(sm > 0.5 * NEG, jnp.exp(sm - m), 0.0)
    l = jnp.sum(e, axis=-1, keepdims=True)
    return e / jnp.maximum(l, 1e-30)


def _params(sem):
    return pltpu.CompilerParams(dimension_semantics=sem, vmem_limit_bytes=VMEM_LIMIT)


def _norm_matmul_body(x_ref, nw_ref, w_ref, o_ref, *rest, copy_tile):
    h_ref = rest[-1]

    @pl.when(pl.program_id(1) == 0)
    def _():
        x = x_ref[...]
        ms = jnp.mean(x * x, axis=-1, keepdims=True)
        h_ref[...] = (x * lax.rsqrt(ms + EPS) * nw_ref[...]).astype(BF16)

    res = _dot(h_ref[...], w_ref[...])
    o_ref[...] = res
    if copy_tile is not None:
        @pl.when(pl.program_id(1) == copy_tile)
        def _():
            rest[0][...] = res


def _norm_matmul(x, nw, w, tm, tn, copy_tile=None):
    M, D = x.shape
    N = w.shape[1]
    assert M % tm == 0 and N % tn == 0
    out_specs = [pl.BlockSpec((tm, tn), lambda i, j: (i, j))]
    out_shape = [jax.ShapeDtypeStruct((M, N), F32)]
    if copy_tile is not None:
        out_specs.append(pl.BlockSpec((tm, tn), lambda i, j: (i, 0)))
        out_shape.append(jax.ShapeDtypeStruct((M, tn), F32))
    outs = pl.pallas_call(
        functools.partial(_norm_matmul_body, copy_tile=copy_tile),
        grid=(M // tm, N // tn),
        in_specs=[pl.BlockSpec((tm, D), lambda i, j: (i, 0)),
                  pl.BlockSpec((1, D), lambda i, j: (0, 0)),
                  pl.BlockSpec((D, tn), lambda i, j: (0, j))],
        out_specs=out_specs,
        out_shape=out_shape,
        scratch_shapes=[pltpu.VMEM((tm, D), BF16)],
        compiler_params=_params(("arbitrary", "arbitrary")),
    )(x, nw, w)
    return outs[0] if copy_tile is None else outs


def _out_proj_body(a1_ref, a2_ref, w1_ref, w2_ref, x_ref, nw_ref, y_ref, *, final_norm):
    y = x_ref[...] + _dot(a1_ref[...].astype(BF16), w1_ref[...]) + _dot(a2_ref[...].astype(BF16), w2_ref[...])
    if final_norm:
        ms = jnp.mean(y * y, axis=-1, keepdims=True)
        y = y * lax.rsqrt(ms + EPS) * nw_ref[...]
    y_ref[...] = y


def _out_proj(a1, c1, a2, c2, w, x, nw, tm, final_norm):
    M, D = x.shape
    KH = w.shape[0] // 2
    assert M % tm == 0
    return pl.pallas_call(
        functools.partial(_out_proj_body, final_norm=final_norm),
        grid=(M // tm,),
        in_specs=[pl.BlockSpec((tm, KH), lambda i: (i, c1)),
                  pl.BlockSpec((tm, KH), lambda i: (i, c2)),
                  pl.BlockSpec((KH, D), lambda i: (0, 0)),
                  pl.BlockSpec((KH, D), lambda i: (1, 0)),
                  pl.BlockSpec((tm, D), lambda i: (i, 0)),
                  pl.BlockSpec((1, D), lambda i: (0, 0))],
        out_specs=pl.BlockSpec((tm, D), lambda i: (i, 0)),
        out_shape=jax.ShapeDtypeStruct((M, D), F32),
        compiler_params=_params(("arbitrary",)),
    )(a1, a2, w, w, x, nw)


def _cmp_bias_body(pe_ref, w1_ref, b1_ref, o_ref):
    pe = jnp.broadcast_to(pe_ref[...], (SUBLANES, pe_ref.shape[-1])).astype(BF16)
    o_ref[...] = b1_ref[...] + _dot(pe, w1_ref[...])[0:1, :]


def _cmp_bias(pe, w1, b1):
    S, KF, HID = w1.shape
    return pl.pallas_call(
        _cmp_bias_body,
        grid=(S,),
        in_specs=[pl.BlockSpec((None, 1, KF), lambda s: (s, 0, 0)),
                  pl.BlockSpec((None, KF, HID), lambda s: (s, 0, 0)),
                  pl.BlockSpec((None, 1, HID), lambda s: (s, 0, 0))],
        out_specs=pl.BlockSpec((None, 1, HID), lambda s: (s, 0, 0)),
        out_shape=jax.ShapeDtypeStruct((S, 1, HID), F32),
        compiler_params=_params(("arbitrary",)),
    )(pe, w1, b1)


def _compress_hidden(u0, u1, bias):
    return u0 + pltpu.roll(u1, NCP - 1, 0) + bias


def _compress_prompt_body(x_ref, w1_ref, cb_ref, w2_ref, o_ref):
    u0 = jnp.zeros((NCP, CMP_HID), F32)
    u1 = jnp.zeros((NCP, CMP_HID), F32)
    for pp in range(D_CMP // 2):
        xa = x_ref[pl.ds(2 * pp, NCP, stride=D_CMP), :]
        xb = x_ref[pl.ds(2 * pp + 1, NCP, stride=D_CMP), :]
        xp = jnp.concatenate([xa, xb], axis=1).astype(BF16)
        u0 = u0 + _dot(xp, w1_ref[pl.ds(pp * 2 * DH, 2 * DH), :])
        u1 = u1 + _dot(xp, w1_ref[pl.ds(D_CMP * DH + pp * 2 * DH, 2 * DH), :])
    h = _compress_hidden(u0, u1, cb_ref[...])
    o_ref[...] = _dot(_gelu_tanh(h).astype(BF16), w2_ref[...])


def _compress_prompt(P, w1, cb, w2, B, T):
    assert T // D_CMP == NCP
    kv0 = A_KV // DH
    return pl.pallas_call(
        _compress_prompt_body,
        grid=(B, 2, HKV),
        in_specs=[pl.BlockSpec((T, DH), lambda b, s, g: (b, kv0 + HKV * s + g)),
                  pl.BlockSpec((None, L_CMP * DH, CMP_HID), lambda b, s, g: (s, 0, 0)),
                  pl.BlockSpec((None, 1, CMP_HID), lambda b, s, g: (s, 0, 0)),
                  pl.BlockSpec((None, CMP_HID, DH), lambda b, s, g: (s, 0, 0))],
        out_specs=pl.BlockSpec((None, None, None, NCP, DH), lambda b, s, g: (b, s, g, 0, 0)),
        out_shape=jax.ShapeDtypeStruct((B, 2, HKV, NCP, DH), F32),
        compiler_params=_params(("arbitrary", "arbitrary", "arbitrary")),
    )(P, w1, cb, w2)


def _select_blocks(imp, tpos_tok, ns):
    blk = lax.broadcasted_iota(jnp.int32, (1, NSP), 1)
    cur = tpos_tok // L_SEL
    valid = blk <= cur
    forced = (blk == 0) | (blk == cur) | (blk == cur - 1)
    score = jnp.where(valid, jnp.where(forced, FORCE_SCORE, imp), -jnp.inf)
    k_top = float(min(N_TOP, ns))
    if score.shape[0] == LANES:
        nb = -(-ns // SUBLANES) * SUBLANES
        st = score.T[0:nb]
        rblk = lax.broadcasted_iota(jnp.int32, (nb, 1), 0)
        rank = jnp.zeros(st.shape, F32)
        for j in range(ns):
            sj = st[j:j + 1, :]
            rank = rank + jnp.where((sj > st) | ((sj == st) & (rblk > j)), 1.0, 0.0)
        top = jnp.where(rank < k_top, 1.0, 0.0)
        top = jnp.concatenate([top, jnp.zeros((NSP - nb, LANES), F32)], axis=0).T
        return jnp.where(valid, top, 0.0)
    rank = jnp.zeros(score.shape, F32)
    for j in range(ns):
        sj = score[:, j:j + 1]
        beats = (sj > score) | ((sj == score) & (blk > j))
        rank = rank + jnp.where(beats, 1.0, 0.0)
    return jnp.where(valid & (rank < float(min(N_TOP, ns))), 1.0, 0.0)


def _compressed_probs(qb, kc, tpos, nc, scale):
    s = _dot_nt(qb, kc.astype(BF16)) * scale
    cidx = lax.broadcasted_iota(jnp.int32, (1, NCP), 1)
    cmask = (cidx < nc) & (cidx * D_CMP + (L_CMP - 1) <= tpos)
    return _masked_softmax_pre(jnp.where(cmask, s, NEG))


def _compressed_branch(qb, kc, vc, tpos, nc, scale):
    p = _compressed_probs(qb, kc, tpos, nc, scale)
    return p, _dot(p.astype(BF16), vc.astype(BF16))


def _keys_with_block_onehot(k_ref, koh_ref, k0, width):
    return jnp.concatenate([k_ref[pl.ds(k0, width), :].astype(BF16), koh_ref[pl.ds(k0, width), :]], axis=1)


def _nsa_prompt_body(q_ref, kc_ref, vc_ref, ks_ref, vs_ref, kw_ref, vw_ref, misc_ref, z_ref, c2s_ref, koh_ref,
                     o_ref, sc_ref, mx_ref, l_ref, acc_ref, *, tq, tk, T):
    qi = pl.program_id(1)
    t0 = pl.multiple_of(qi * tq, tq)
    R = GRP * tq
    scale = DH ** -0.5
    c1 = scale * LOG2E
    ns = -(-T // L_SEL)
    nc = T // D_CMP - CMP_R + 1
    n_top = min(N_TOP, ns)
    heads = range(HKV)

    def hd(g):
        return slice(g * DH, (g + 1) * DH)

    qbs = [jnp.concatenate([q_ref[:, (g * GRP + r) * DH:(g * GRP + r + 1) * DH] for r in range(GRP)],
                           axis=0).astype(BF16) for g in heads]
    tpos = t0 + (lax.broadcasted_iota(jnp.int32, (R, 1), 0) & (tq - 1))
    tpos_tok = t0 + lax.broadcasted_iota(jnp.int32, (tq, 1), 0)
    blk = lax.broadcasted_iota(jnp.int32, (1, NSP), 1)

    nwt = WINDOW // tq + 1
    tiles, starts = [[] for _ in heads], []
    for i in range(nwt):
        k0 = t0 - WINDOW + i * tq
        k0c = pl.multiple_of(jnp.maximum(k0, 0), tq)
        kpos = k0 + lax.broadcasted_iota(jnp.int32, (1, tq), 1)
        for g in heads:
            s = _dot_nt(qbs[g], kw_ref[pl.ds(k0c, tq), hd(g)].astype(BF16))
            if i == 0:
                s = jnp.where((kpos > tpos - WINDOW) & (kpos >= 0), s, NEG)
            elif i == nwt - 1:
                s = jnp.where(kpos <= tpos, s, NEG)
            else:
                s = jnp.where(k0 >= 0, s, NEG)
            tiles[g].append(s)
        starts.append(k0c)
    es_w = []
    for g in heads:
        s = jnp.concatenate(tiles[g], axis=1)
        es_w.append(jnp.exp2((s - jnp.max(s, axis=-1, keepdims=True)) * c1))
    o_ws = [None for _ in heads]
    for i in range(nwt):
        for g in heads:
            pv = _dot(es_w[g][:, i * tq:(i + 1) * tq].astype(BF16), vw_ref[pl.ds(starts[i], tq), hd(g)].astype(BF16))
            o_ws[g] = pv if i == 0 else o_ws[g] + pv
    o_ws = [o_ws[g] / jnp.sum(es_w[g], axis=-1, keepdims=True) for g in heads]

    comp = [_compressed_branch(qbs[g], kc_ref[g], vc_ref[g], tpos, nc, scale) for g in heads]
    imps = []
    for g in heads:
        p_c = comp[g][0]
        psum = p_c[0:tq]
        for r in range(1, GRP):
            psum = psum + p_c[r * tq:(r + 1) * tq]
        imps.append(_dot_exact_rhs(psum, c2s_ref[...]))

    def all_valid():
        v = jnp.where(blk <= tpos_tok // L_SEL, 1.0, 0.0)
        return tuple(v for _ in heads)

    sels = lax.cond(t0 + tq <= n_top * L_SEL, all_valid,
                    lambda: tuple(_select_blocks(imps[g], tpos_tok, ns) for g in heads))

    q_augs, q_earlys = [], []
    for g in heads:
        sel_bias = (sels[g] - 1.0) * (-NEG)
        early_bias = jnp.where(blk < lax.div(t0, L_SEL), sel_bias, NEG)
        q_augs.append(jnp.concatenate([qbs[g], jnp.concatenate([sel_bias.astype(BF16)] * GRP, axis=0)], axis=1))
        q_earlys.append(jnp.concatenate([qbs[g], jnp.concatenate([early_bias.astype(BF16)] * GRP, axis=0)], axis=1))

    def keys_aug(k0, width, g):
        return jnp.concatenate([ks_ref[pl.ds(k0, width), hd(g)].astype(BF16), koh_ref[pl.ds(k0, width), :]], axis=1)

    kpos_d = t0 + lax.broadcasted_iota(jnp.int32, (1, tq), 1)
    s_diags = [jnp.where(kpos_d <= tpos, _dot_nt(q_augs[g], keys_aug(t0, tq, g)), NEG) for g in heads]
    n_early = lax.div(t0 + tk - 1, tk)
    for g in heads:
        mx_ref[g] = s_diags[g]

    def pass1(kt, carry):
        k0 = pl.multiple_of(kt * tk, tk)
        ss = [_dot_nt(q_earlys[g], keys_aug(k0, tk, g)) for g in heads]
        old = [mx_ref[g] for g in heads]
        for g in heads:
            mx = ss[g][:, 0:LANES]
            for c in range(1, tk // LANES):
                mx = jnp.maximum(mx, ss[g][:, c * LANES:(c + 1) * LANES])
            sc_ref[g, kt] = ss[g]
            mx_ref[g] = jnp.maximum(old[g], mx)
        return carry

    lax.fori_loop(0, n_early, pass1, 0)
    m2s = [jnp.broadcast_to(jnp.max(mx_ref[g], axis=-1, keepdims=True) * c1, (R, LANES)) for g in heads]
    e_ds = [jnp.exp2(s_diags[g] * c1 - m2s[g]) for g in heads]
    pv_ds = [_dot(e_ds[g].astype(BF16), vs_ref[pl.ds(t0, tq), hd(g)].astype(BF16)) for g in heads]
    for g in heads:
        mx_ref[g] = m2s[g]
        l_ref[g] = e_ds[g]
        acc_ref[g] = pv_ds[g]

    def pass2(kt, carry):
        k0 = pl.multiple_of(kt * tk, tk)
        ss = [sc_ref[g, kt] for g in heads]
        m2 = [mx_ref[g] for g in heads]
        l_old = [l_ref[g] for g in heads]
        acc_old = [acc_ref[g] for g in heads]
        ebs, lsums = [], []
        for g in heads:
            es = [jnp.exp2(ss[g][:, c * LANES:(c + 1) * LANES] * c1 - m2[g]) for c in range(tk // LANES)]
            lsum = es[0]
            for ec in es[1:]:
                lsum = lsum + ec
            lsums.append(lsum)
            ebs.append(jnp.concatenate(es, axis=1).astype(BF16))
        pvs = [_dot(ebs[g], vs_ref[pl.ds(k0, tk), hd(g)].astype(BF16)) for g in heads]
        for g in heads:
            l_ref[g] = l_old[g] + lsums[g]
            acc_ref[g] = acc_old[g] + pvs[g]
        return carry

    lax.fori_loop(0, n_early, pass2, 0)

    outs = []
    for g in heads:
        o_s = acc_ref[g] / jnp.sum(l_ref[g], axis=-1, keepdims=True)
        gm = _sigmoid(misc_ref[:, g * LANES:g * LANES + GRP * 3])

        def gate(j):
            return jnp.concatenate([gm[:, r * 3 + j:r * 3 + j + 1] for r in range(GRP)], axis=0)

        outs.append(gate(0) * comp[g][1] + gate(1) * o_s + gate(2) * o_ws[g])
    for g in heads:
        for r in range(GRP):
            c = (g * GRP + r) * DH
            o_ref[:, c:c + DH] = (outs[g][r * tq:(r + 1) * tq] * _silu(z_ref[:, c:c + DH])).astype(BF16)


def _nsa_prompt(P, kcv, c2s, koh, B, T, tq, tk):
    assert T % tq == 0 and tq & (tq - 1) == 0 and tk % tq == 0 and T % tk == 0 and WINDOW % tq == 0
    nq = T // tq
    qw = H_A * DH
    kw = HKV * DH
    R = GRP * tq

    def kvspec(slot):
        return pl.BlockSpec((T, kw), lambda b, i: (b, A_KV // kw + slot))

    return pl.pallas_call(
        functools.partial(_nsa_prompt_body, tq=tq, tk=tk, T=T),
        grid=(B, nq),
        in_specs=[pl.BlockSpec((tq, qw), lambda b, i: (b * nq + i, A_Q // qw)),
                  pl.BlockSpec((None, None, HKV, NCP, DH), lambda b, i: (b, 0, 0, 0, 0)),
                  pl.BlockSpec((None, None, HKV, NCP, DH), lambda b, i: (b, 1, 0, 0, 0)),
                  kvspec(2), kvspec(3), kvspec(4), kvspec(5),
                  pl.BlockSpec((tq, HKV * LANES), lambda b, i: (b * nq + i, A_MISC // (HKV * LANES))),
                  pl.BlockSpec((tq, qw), lambda b, i: (b * nq + i, A_ZA // qw)),
                  pl.BlockSpec((NCP, NSP), lambda b, i: (0, 0)),
                  pl.BlockSpec((T, NSP), lambda b, i: (0, 0))],
        out_specs=pl.BlockSpec((tq, qw), lambda b, i: (b * nq + i, 0)),
        out_shape=jax.ShapeDtypeStruct((B * T, qw), BF16),
        scratch_shapes=[pltpu.VMEM((HKV, T // tk, R, tk), F32), pltpu.VMEM((HKV, R, LANES), F32),
                        pltpu.VMEM((HKV, R, LANES), F32), pltpu.VMEM((HKV, R, DH), F32)],
        compiler_params=_params(("arbitrary", "arbitrary")),
    )(P, kcv, kcv, P, P, P, P, P, P, c2s, koh)


def _pad_rows(x, rows):
    return jnp.concatenate([x, jnp.zeros((rows - x.shape[0], x.shape[1]), x.dtype)], axis=0)


def _nsa_sample_body(pt_ref, cache_ref, win_ref, q_ref, kvr_ref, kvw_ref, misc_ref, z_ref, w1_ref, cb_ref, w2_ref,
                     c2s_ref, koh_ref, o_ref, wout_ref, pbuf, sem, *, n_pages, page, past_len, dec_seq, wb):
    b = pl.program_id(0)
    scale = DH ** -0.5
    c1 = scale * LOG2E
    L = past_len + dec_seq
    ns = -(-L // L_SEL)
    nc = L // D_CMP - CMP_R + 1
    R = GRP * TP
    per_page = page // D_CMP
    nch = N_KV_SLOTS * HKV
    grp_rows = D_CMP * nch
    pitch = grp_rows + 1

    def page_copies(seq, slot):
        cps = []
        for j in range(n_pages):
            pid = pt_ref[seq * n_pages + j]
            for n in range(per_page):
                cps.append(pltpu.make_async_copy(
                    cache_ref.at[pid, pl.ds(n * grp_rows, grp_rows), :],
                    pbuf.at[slot, pl.ds((j * per_page + n) * pitch, grp_rows), :], sem.at[slot]))
        return cps

    slot = lax.rem(b, 2)

    @pl.when(b == 0)
    def _():
        for cp in page_copies(0, 0):
            cp.start()

    for cp in page_copies(b, slot):
        cp.wait()

    last = pl.num_programs(0) - 1
    next_copies = page_copies(jnp.minimum(b + 1, last), 1 - slot)
    copies_per_group = len(next_copies) // (2 * (D_CMP // 2))

    def gather(p, ch):
        return pbuf[slot, pl.ds(p * nch + ch, NCP, stride=pitch), :]

    trow = lax.broadcasted_iota(jnp.int32, (R, 1), 0) & (TP - 1)
    tpos = past_len + trow
    tpos_tok = past_len + lax.broadcasted_iota(jnp.int32, (TP, 1), 0)
    lane = lax.broadcasted_iota(jnp.int32, (1, LANES), 1)
    new_pos = past_len + lane
    new_ok = (new_pos <= tpos) & (new_pos < L)
    new_blk = (past_len + lax.broadcasted_iota(jnp.int32, (LANES, 1), 0)) // L_SEL
    new_oh = jnp.where(lax.broadcasted_iota(jnp.int32, (LANES, NSP), 1) == new_blk, 1.0, 0.0).astype(BF16)
    wpos = (past_len - wb) + lax.broadcasted_iota(jnp.int32, (1, wb), 1)
    wch = 2 * HKV

    heads = range(HKV)
    n_iter = D_CMP // 2
    qbs = [jnp.concatenate([q_ref[:, (g * GRP + r) * DH:(g * GRP + r + 1) * DH] for r in range(GRP)],
                           axis=0).astype(BF16) for g in heads]
    raw = [[] for _ in heads]
    win = {}

    def stage_window_scores():
        es = []
        for g in heads:
            s_w = _dot_nt(qbs[g], win_ref[pl.ds(g, wb, stride=wch), :].astype(BF16))
            s_w = jnp.where((wpos <= tpos) & (wpos > tpos - WINDOW), s_w, NEG)
            s_n = _dot_nt(qbs[g], _pad_rows(kvw_ref[:, g * DH:(g + 1) * DH], LANES).astype(BF16))
            s_n = jnp.where(new_ok & (new_pos > tpos - WINDOW), s_n, NEG)
            s = jnp.concatenate([s_w, s_n], axis=1)
            es.append(jnp.exp2((s - jnp.max(s, axis=-1, keepdims=True)) * c1))
        win["e"] = es

    def stage_window_out():
        outs_w = []
        for g in heads:
            e = win["e"][g]
            o_w = (_dot(e[:, :wb].astype(BF16), win_ref[pl.ds(HKV + g, wb, stride=wch), :].astype(BF16))
                   + _dot(e[:, wb:].astype(BF16),
                          _pad_rows(kvw_ref[:, (HKV + g) * DH:(HKV + g + 1) * DH], LANES).astype(BF16)))
            outs_w.append(o_w / jnp.sum(e, axis=-1, keepdims=True))
        win["o"] = outs_w

    def stage_raw_scores(p_lo, p_hi):
        for p in range(p_lo, p_hi):
            for g in heads:
                raw[g].append(_dot_nt(qbs[g], gather(p, 2 * HKV + g).astype(BF16)))

    def stage_raw_new():
        for g in heads:
            ksel_col = (2 * HKV + g) * DH
            raw[g].append(_dot_nt(qbs[g], _pad_rows(kvr_ref[:, ksel_col:ksel_col + DH], LANES).astype(BF16)))


    def compress_step(s, pp, u0, u1):
        xs = [jnp.concatenate([gather(2 * pp, s * HKV + g), gather(2 * pp + 1, s * HKV + g)], axis=1) for g in heads]
        xp = jnp.concatenate(xs, axis=0).astype(BF16)
        u0 = u0 + _dot(xp, w1_ref[s, pl.ds(pp * 2 * DH, 2 * DH), :])
        u1 = u1 + _dot(xp, w1_ref[s, pl.ds(D_CMP * DH + pp * 2 * DH, 2 * DH), :])
        grp = s * n_iter + pp
        for cp in next_copies[grp * copies_per_group:(grp + 1) * copies_per_group]:
            cp.start()
        return u0, u1

    def compress_out(s, u0, u1):
        return [_dot(_gelu_tanh(_compress_hidden(u0[g * NCP:(g + 1) * NCP], u1[g * NCP:(g + 1) * NCP],
                                                 cb_ref[s])).astype(BF16), w2_ref[s]) for g in heads]

    quarter = D_CMP // 4
    early = {0: stage_window_scores, 1: lambda: stage_raw_scores(0, quarter),
             2: lambda: stage_raw_scores(quarter, 2 * quarter), 3: stage_window_out,
             4: lambda: stage_raw_scores(2 * quarter, 3 * quarter),
             5: lambda: stage_raw_scores(3 * quarter, D_CMP), 6: stage_raw_new}
    u0 = u1 = jnp.zeros((HKV * NCP, CMP_HID), F32)
    for pp in range(n_iter):
        u0, u1 = compress_step(0, pp, u0, u1)
        if pp in early:
            early[pp]()
    kc = compress_out(0, u0, u1)

    st = {}

    def stage_probs():
        st["p_c"] = [_compressed_probs(qbs[g], kc[g], tpos, nc, scale) for g in heads]

    def stage_select():
        imps = []
        for g in heads:
            p_c = st["p_c"][g]
            psum = p_c[0:TP]
            for r in range(1, GRP):
                psum = psum + p_c[r * TP:(r + 1) * TP]
            imps.append(_dot_exact_rhs(psum, c2s_ref[...]))
        st["sel"] = [_select_blocks(imps[g], tpos_tok, ns) for g in heads]

    def stage_scores():
        biases = [jnp.concatenate([((st["sel"][g] - 1.0) * (-NEG)).astype(BF16)] * GRP, axis=0) for g in heads]
        bias_past = [_dot_nt(biases[g], koh_ref[...]) for g in heads]
        bias_new = [jnp.where(new_ok, _dot_nt(biases[g], new_oh), NEG) for g in heads]
        es = []
        for g in heads:
            s = jnp.concatenate([pc + bias_past[g] for pc in raw[g][:D_CMP]] + [raw[g][D_CMP] + bias_new[g]], axis=1)
            es.append(jnp.exp2((s - jnp.max(s, axis=-1, keepdims=True)) * c1))
        st["e"] = es
        st["o_s"] = [_dot(es[g][:, D_CMP * NCP:].astype(BF16),
                          _pad_rows(kvr_ref[:, (3 * HKV + g) * DH:(3 * HKV + g + 1) * DH], LANES).astype(BF16))
                     for g in heads]

    def stage_pv(p_lo, p_hi):
        for p in range(p_lo, p_hi):
            for g in heads:
                st["o_s"][g] = st["o_s"][g] + _dot(st["e"][g][:, p * NCP:(p + 1) * NCP].astype(BF16),
                                                  gather(p, 3 * HKV + g).astype(BF16))

    after = {0: stage_probs, 3: stage_select, 6: stage_scores, 7: lambda: stage_pv(0, D_CMP // 2)}
    u0 = u1 = jnp.zeros((HKV * NCP, CMP_HID), F32)
    for pp in range(n_iter):
        u0, u1 = compress_step(1, pp, u0, u1)
        if pp in after:
            after[pp]()
    vc = compress_out(1, u0, u1)
    stage_pv(D_CMP // 2, D_CMP)

    outs = []
    for g in heads:
        o_c = _dot(st["p_c"][g].astype(BF16), vc[g].astype(BF16))
        o_s = st["o_s"][g] / jnp.sum(st["e"][g], axis=-1, keepdims=True)
        gm = _sigmoid(misc_ref[:, g * LANES:g * LANES + GRP * 3])

        def gate(j):
            return jnp.concatenate([gm[:, r * 3 + j:r * 3 + j + 1] for r in range(GRP)], axis=0)

        outs.append(gate(0) * o_c + gate(1) * o_s + gate(2) * win["o"][g])

    for g in range(HKV):
        for r in range(GRP):
            c = (g * GRP + r) * DH
            o_ref[:, c:c + DH] = outs[g][r * TP:(r + 1) * TP] * _silu(z_ref[:, c:c + DH])

    wout_ref[pl.ds(0, (wb - dec_seq) * wch), :] = win_ref[pl.ds(dec_seq * wch, (wb - dec_seq) * wch), :]
    for t in range(dec_seq):
        for ch in range(wch):
            wout_ref[pl.ds((wb - dec_seq + t) * wch + ch, 1), :] = kvw_ref[t:t + 1, ch * DH:(ch + 1) * DH]

    @pl.when(b == last)
    def _():
        for cp in next_copies:
            cp.wait()


def _nsa_sample(Ps, cache, cache_win, page_table, w1, cb, w2, c2s, dec_seq):
    NB, n_pages = page_table.shape
    n_pool, prow, _ = cache.shape
    page = prow // (N_KV_SLOTS * HKV)
    wb = cache_win.shape[1] // (2 * HKV)
    past_len = n_pages * page
    L = past_len + dec_seq
    assert wb == WINDOW and (L // D_CMP) == NCP and NCP * D_CMP == past_len and dec_seq <= TP
    assert -(-L // L_SEL) < NSP and page % D_CMP == 0 and (dec_seq * 2 * HKV) % SUBLANES == 0
    qw = H_A * DH
    nch = N_KV_SLOTS * HKV
    koh = jnp.asarray(np.arange(NCP)[:, None] * D_CMP // L_SEL == np.arange(NSP)[None, :], dtype=BF16)

    grid_spec = pltpu.PrefetchScalarGridSpec(
        num_scalar_prefetch=1,
        grid=(NB,),
        in_specs=[
            pl.BlockSpec(memory_space=pl.ANY),
            pl.BlockSpec((None, wb * 2 * HKV, DH), lambda b, pt: (b, 0, 0)),
            pl.BlockSpec((TP, qw), lambda b, pt: (b, A_Q // qw)),
            pl.BlockSpec((TP, 4 * HKV * DH), lambda b, pt: (b, A_KV // (4 * HKV * DH))),
            pl.BlockSpec((TP, 2 * HKV * DH), lambda b, pt: (b, (A_KV + 4 * HKV * DH) // (2 * HKV * DH))),
            pl.BlockSpec((TP, HKV * LANES), lambda b, pt: (b, A_MISC // (HKV * LANES))),
            pl.BlockSpec((TP, qw), lambda b, pt: (b, A_ZA // qw)),
            pl.BlockSpec((2, L_CMP * DH, CMP_HID), lambda b, pt: (0, 0, 0)),
            pl.BlockSpec((2, 1, CMP_HID), lambda b, pt: (0, 0, 0)),
            pl.BlockSpec((2, CMP_HID, DH), lambda b, pt: (0, 0, 0)),
            pl.BlockSpec((NCP, NSP), lambda b, pt: (0, 0)),
            pl.BlockSpec((NCP, NSP), lambda b, pt: (0, 0))],
        out_specs=[pl.BlockSpec((TP, qw), lambda b, pt: (b, 0)),
                   pl.BlockSpec((None, wb * 2 * HKV, DH), lambda b, pt: (b, 0, 0))],
        scratch_shapes=[pltpu.VMEM((2, NCP * (D_CMP * nch + 1), DH), F32), pltpu.SemaphoreType.DMA((2,))],
    )
    return pl.pallas_call(
        functools.partial(_nsa_sample_body, n_pages=n_pages, page=page, past_len=past_len, dec_seq=dec_seq, wb=wb),
        grid_spec=grid_spec,
        out_shape=[jax.ShapeDtypeStruct((NB * TP, qw), F32),
                   jax.ShapeDtypeStruct((NB, wb * 2 * HKV, DH), F32)],
        compiler_params=_params(("arbitrary",)),
    )(page_table.reshape(-1), cache, cache_win, Ps, Ps, Ps, Ps, Ps, w1, cb, w2, c2s, koh)


def _rec_chunk(q, k, v, la, gate, S, C, last_row, nh, K, V):
    sb = min(16, C)
    ri = lax.broadcasted_iota(jnp.int32, (C, C), 0)
    cj = lax.broadcasted_iota(jnp.int32, (C, C), 1)
    causal = cj <= ri
    tri = jnp.where(causal, 1.0, 0.0).astype(BF16)
    b = _dot_exact_lhs(tri, la)
    qe = (q * jnp.exp(b)).astype(BF16)
    qis, kis = [], []
    k_ref_prev, c_prev = None, None
    for i in range(C // sb):
        r0 = i * sb
        ci = b[r0 + sb // 2:r0 + sb // 2 + 1, :]
        qis.append((q[r0:r0 + sb] * jnp.exp(b[r0:r0 + sb] - ci)).astype(BF16))
        k_new = k[r0:r0 + sb] * jnp.exp(jnp.minimum(ci - b[r0:r0 + sb], 80.0))
        k_scaled = k_new if i == 0 else jnp.concatenate([k_ref_prev * jnp.exp(ci - c_prev), k_new], axis=0)
        k_ref_prev, c_prev = k_scaled, ci
        rest = C - r0 - sb
        kis.append((k_scaled if rest == 0 else
                    jnp.concatenate([k_scaled, jnp.zeros((rest, k.shape[1]), F32)], axis=0)).astype(BF16))
    b_last = b[last_row:last_row + 1, :]
    rowi = lax.broadcasted_iota(jnp.int32, (C, 1), 0)
    kd = jnp.where(rowi <= last_row, k * jnp.exp(jnp.minimum(b_last - b, 0.0)), 0.0)
    stack = jnp.concatenate([kd, jnp.broadcast_to(b_last, (SUBLANES, nh * K)),
                             jnp.zeros((LANES - C - SUBLANES, nh * K), F32)], axis=0)
    vb = v.astype(BF16)
    v_pad = jnp.concatenate([vb, jnp.zeros((LANES - C, nh * V), BF16)], axis=0)
    outs, s_new = [], []
    for h in range(nh):
        ks = slice(h * K, (h + 1) * K)
        vs = slice(h * V, (h + 1) * V)
        o = _dot(qe[:, ks], S[h].astype(BF16))
        rows = [_dot_nt(qi[:, ks], ki[:, ks]) for qi, ki in zip(qis, kis)]
        att = rows[0] if len(rows) == 1 else jnp.concatenate(rows, axis=0)
        att = jnp.where(causal, att, 0.0)
        o = o + _dot(att.astype(BF16), vb[:, vs])
        stack_t = stack[:, ks].T
        a_col = jnp.exp(stack_t[:, C:C + 1])
        s_new.append(a_col * S[h] + _dot(stack_t.astype(BF16), v_pad[:, vs]))
        ms = jnp.mean(o * o, axis=-1, keepdims=True)
        outs.append(o * lax.rsqrt(ms + EPS) * gate[:, vs])
    return outs, s_new


def _rec_body(*refs, variant, nh, K, V, C, TB, SB, last_row, has_s0, layer_idx):
    refs = list(refs)
    q_ref, k_ref, v_ref, z_ref = refs[:4]
    pos = 4
    if variant == "gla":
        misc_ref, w2_ref, gb_ref = refs[pos:pos + 3]
        pos += 3
    else:
        lb_ref = refs[pos]
        pos += 1
    gn_ref = refs[pos]
    pos += 1
    if has_s0:
        s0_ref = refs[pos]
        pos += 1
    o_ref, s_ref = refs[pos:pos + 2]
    n_chunks = TB // C
    single = n_chunks == 1 and has_s0

    if not single:
        @pl.when(pl.program_id(2) == 0)
        def _():
            if has_s0:
                s_ref[...] = s0_ref[...]
            else:
                s_ref[...] = jnp.zeros(s_ref.shape, F32)

    if variant == "hgrn":
        lg = lb_ref[...]
        e = jnp.exp(lg - jnp.max(lg, axis=0, keepdims=True))
        prob = e / jnp.sum(e, axis=0, keepdims=True)
        lb = prob[1:2]
        for i in range(2, layer_idx + 1):
            lb = lb + prob[i:i + 1]
        if SB > 1:
            lb = jnp.concatenate([lb] * SB, axis=1)
    gn_all = jnp.concatenate([gn_ref[...]] * (nh * SB), axis=1)

    def chunk(c0):
        def rows(s):
            return pl.ds(s * TB + c0, C)

        def cat(f):
            return f(0) if SB == 1 else jnp.concatenate([f(s) for s in range(SB)], axis=1)

        qr = cat(lambda s: q_ref[rows(s), :])
        kr = cat(lambda s: k_ref[rows(s), :])
        v = cat(lambda s: v_ref[rows(s), :])
        gate = _silu(cat(lambda s: z_ref[rows(s), :])) * gn_all
        if variant == "gla":
            w2 = w2_ref[...].astype(BF16)
            zg = cat(lambda s: _dot(misc_ref[rows(s), :].astype(BF16), w2) + gb_ref[...])
            q, k, la = qr * (K ** -0.5), kr, _log_sigmoid(zg) / GLA_GATE_NORM
        else:
            t = jnp.exp(-jnp.abs(kr))
            r = 1.0 / (1.0 + t)
            tr = t * r
            nonneg = kr >= 0.0
            sig = jnp.where(nonneg, r, tr)
            nsig = jnp.where(nonneg, tr, r)
            q, k, la = _silu(qr), (1.0 - lb) * nsig, jnp.log(lb + (1.0 - lb) * sig)
        src = s0_ref if single else s_ref
        states = [src[s, h] for s in range(SB) for h in range(nh)]
        outs, s_new = _rec_chunk(q, k, v, la, gate, states, C, last_row, SB * nh, K, V)
        for s in range(SB):
            for h in range(nh):
                o_ref[rows(s), h * V:(h + 1) * V] = outs[s * nh + h].astype(o_ref.dtype)
                s_ref[s, h] = s_new[s * nh + h]

    if n_chunks == 1:
        chunk(0)
    else:
        def body(ci, carry):
            chunk(pl.multiple_of(ci * C, C))
            return carry

        lax.fori_loop(0, n_chunks, body, 0, unroll=8)


def _recurrence(P, cols, extra, gn, s0, *, variant, B, T, H, K, V, C, TB, nh, last_row, out_dtype, SB=1,
                layer_idx=1):
    cq, ck, cv, cz = cols
    assert H % nh == 0 and T % TB == 0 and TB % C == 0 and B % SB == 0
    nt = T // TB
    assert SB == 1 or nt == 1

    def colspec(c0, w):
        assert c0 % (nh * w) == 0
        return pl.BlockSpec((SB * TB, nh * w), lambda b, j, t: (b * nt + t, c0 // (nh * w) + j))

    in_specs = [colspec(cq, K), colspec(ck, K), colspec(cv, V), colspec(cz, V)]
    args = [P, P, P, P]
    if variant == "gla":
        w2p, gb = extra
        in_specs += [pl.BlockSpec((SB * TB, LANES), lambda b, j, t: (b * nt + t, A_MISC // LANES)),
                     pl.BlockSpec((LANES, nh * K), lambda b, j, t: (0, j)),
                     pl.BlockSpec((1, nh * K), lambda b, j, t: (0, j))]
        args += [P, w2p, gb]
    else:
        (lb_logits,) = extra
        in_specs += [pl.BlockSpec((lb_logits.shape[0], nh * K), lambda b, j, t: (0, j))]
        args += [lb_logits]
    in_specs += [pl.BlockSpec((1, V), lambda b, j, t: (0, 0))]
    args += [gn]
    st_spec = pl.BlockSpec((None, SB, nh, K, V), lambda b, j, t: (0, b, j, 0, 0))
    if s0 is not None:
        in_specs += [st_spec]
        args += [s0]
    return pl.pallas_call(
        functools.partial(_rec_body, variant=variant, nh=nh, K=K, V=V, C=C, TB=TB, SB=SB, last_row=last_row,
                          has_s0=s0 is not None, layer_idx=layer_idx),
        grid=(B // SB, H // nh, nt),
        in_specs=in_specs,
        out_specs=[pl.BlockSpec((SB * TB, nh * V), lambda b, j, t: (b * nt + t, j)), st_spec],
        out_shape=[jax.ShapeDtypeStruct((B * T, H * V), out_dtype),
                   jax.ShapeDtypeStruct((1, B, H, K, V), F32)],
        compiler_params=_params(("arbitrary", "arbitrary", "arbitrary")),
    )(*args)


def _layout_a_w_in_body(w_ref, o_ref):
    off = [int(v) for v in np.concatenate([[0], np.cumsum(A_SIZES)])]
    q, kv, gbr, za, qb, kb, vb, lr, zb = [w_ref[:, off[i]:off[i + 1]] for i in range(len(A_SIZES))]
    rows = w_ref.shape[0]
    z = lambda n: jnp.zeros((rows, n), F32)
    misc0 = jnp.concatenate([gbr[:, :GRP * 3], lr, z(LANES - GRP * 3 - GLA_LR)], axis=1)
    misc1 = jnp.concatenate([gbr[:, GRP * 3:], z(LANES - GRP * 3)], axis=1)
    out = jnp.concatenate([q, vb, zb, za, qb, kb, kv, misc0, misc1, z(NA - A_MISC - HKV * LANES)], axis=1)
    o_ref[...] = out.astype(BF16)


def _layout_a_w_in(w, tr=256):
    D, n = w.shape
    assert D % tr == 0 and n == sum(A_SIZES)
    return pl.pallas_call(
        _layout_a_w_in_body,
        grid=(D // tr,),
        in_specs=[pl.BlockSpec((tr, n), lambda i: (i, 0))],
        out_specs=pl.BlockSpec((tr, NA), lambda i: (i, 0)),
        out_shape=jax.ShapeDtypeStruct((D, NA), BF16),
        compiler_params=_params(("arbitrary",)),
    )(w)


def _cmp_to_sel():
    c_start = np.arange(NCP)[:, None] * D_CMP
    s_start = np.arange(NSP)[None, :] * L_SEL
    overlap = np.clip(np.minimum(c_start + L_CMP, s_start + L_SEL) - np.maximum(c_start, s_start), 0, None)
    return jnp.asarray(overlap / D_CMP, dtype=BF16)


def kernel(x_prompt, x_sample, cache_kv, cache_win, state_gla, state_hgrn, page_table, a_norm, a_w_in, a_gla_w2,
           a_gla_b, a_gla_gn, a_cmp_pe, a_cmp_w1, a_cmp_b1, a_cmp_w2, a_w_out, c_norm, c_w_in, c_lb_logits, c_gn,
           c_w_out, final_norm):
    B, T, D = x_prompt.shape
    NB, TS, _ = x_sample.shape
    n_pool, page = cache_kv.shape[1], cache_kv.shape[2]
    wb = cache_win.shape[2]
    assert a_norm.shape[0] == 1 and c_norm.shape[0] == 1 and c_lb_logits.shape[0] == 2

    wa = _layout_a_w_in(a_w_in[0])
    wc = c_w_in[0].astype(BF16)
    wao = a_w_out[0].astype(BF16)
    wco = c_w_out[0].astype(BF16)
    w1 = a_cmp_w1[0].astype(BF16)
    w2 = a_cmp_w2[0].astype(BF16)
    pe = a_cmp_pe[0].reshape(2, 1, L_CMP * DH)
    b1 = a_cmp_b1[0].reshape(2, 1, CMP_HID)
    w2p = jnp.zeros((LANES, H_B * DK_B), F32).at[MISC_LR:MISC_LR + GLA_LR, :].set(a_gla_w2[0])
    gb = a_gla_b[0].reshape(1, H_B * DK_B)
    c2s = _cmp_to_sel()
    a_nw = a_norm[0].reshape(1, D)
    c_nw = c_norm[0].reshape(1, D)
    f_nw = final_norm.reshape(1, D)
    gla_gn = a_gla_gn[0].reshape(1, DV_B)
    hg_gn = c_gn[0].reshape(1, DV_C)

    xp = x_prompt.reshape(B * T, D)
    xs = jnp.pad(x_sample, ((0, 0), (0, TP - TS), (0, 0))).reshape(NB * TP, D)

    cb = _cmp_bias(pe, w1, b1)

    rows_w = N_KV_SLOTS * HKV * DH
    Pp, kv_rows_p = _norm_matmul(xp, a_nw, wa, tm=1024, tn=rows_w, copy_tile=A_KV // rows_w)
    kcv = _compress_prompt(Pp, w1, cb, w2, B, T)
    koh = jnp.asarray(np.arange(T)[:, None] // L_SEL == np.arange(NSP)[None, :], dtype=BF16)
    oa_p = _nsa_prompt(Pp, kcv, c2s, koh, B, T, tq=128, tk=512)
    ob_p, gla_p = _recurrence(Pp, (A_QB, A_KB, A_VB, A_ZB), (w2p, gb), gla_gn, None, variant="gla", B=B, T=T,
                              H=H_B, K=DK_B, V=DV_B, C=64, TB=512, nh=4, last_row=63, out_dtype=BF16)
    y1p = _out_proj(oa_p, 0, ob_p, 0, wao, xp, a_nw, tm=512, final_norm=False)

    Ps = _norm_matmul(xs, a_nw, wa, tm=NB * TP, tn=1024)
    cache2 = cache_kv[0].reshape(n_pool, page * N_KV_SLOTS * HKV, DH)
    win2 = cache_win[0].reshape(NB, wb * 2 * HKV, DH)
    oa_s, win_out = _nsa_sample(Ps, cache2, win2, page_table, w1, cb, w2, c2s, TS)
    ob_s, gla_s = _recurrence(Ps, (A_QB, A_KB, A_VB, A_ZB), (w2p, gb), gla_gn, state_gla, variant="gla", B=NB, T=TP,
                              H=H_B, K=DK_B, V=DV_B, C=TP, TB=TP, nh=H_B, last_row=TS - 1, out_dtype=F32,
                              SB=math.gcd(NB, 8))
    y1s = _out_proj(oa_s, 0, ob_s, 0, wao, xs, a_nw, tm=min(512, NB * TP), final_norm=False)

    Pc = _norm_matmul(y1p, c_nw, wc, tm=1024, tn=1024)
    hc = H_C * DK_C
    oc_p, hg_p = _recurrence(Pc, (0, hc, 2 * hc, 3 * hc), (c_lb_logits,), hg_gn, None, variant="hgrn", B=B, T=T,
                             H=H_C, K=DK_C, V=DV_C, C=64, TB=512, nh=8, last_row=63, out_dtype=BF16)
    y_prompt = _out_proj(oc_p, 0, oc_p, 1, wco, y1p, f_nw, tm=512, final_norm=True)

    Pcs = _norm_matmul(y1s, c_nw, wc, tm=NB * TP, tn=1024)
    oc_s, hg_s = _recurrence(Pcs, (0, hc, 2 * hc, 3 * hc), (c_lb_logits,), hg_gn, state_hgrn, variant="hgrn", B=NB,
                             T=TP, H=H_C, K=DK_C, V=DV_C, C=TP, TB=TP, nh=H_C, last_row=TS - 1, out_dtype=F32,
                             SB=math.gcd(NB, 4))
    y_sample = _out_proj(oc_s, 0, oc_s, 1, wco, y1s, f_nw, tm=min(512, NB * TP), final_norm=True)

    kvw = 6 * HKV * DH
    nrw = N_KV_SLOTS * HKV * DH
    Pp3 = Pp.reshape(B, T, NA)
    Ps3 = Ps.reshape(NB, TP, NA)
    kv_p = kv_rows_p.reshape(1, B, T, N_KV_SLOTS, HKV, DH)
    kv_s = Ps3[:, :TS, A_KV:A_KV + nrw].reshape(1, NB, TS, N_KV_SLOTS, HKV, DH)
    keep = min(WINDOW, T)
    win_p = Pp3[:, T - keep:, A_KV + nrw:A_KV + kvw].reshape(1, B, keep, 2, HKV, DH)
    win_s = win_out.reshape(1, NB, wb, 2, HKV, DH)
    return (y_prompt.reshape(B, T, D), y_sample.reshape(NB, TP, D)[:, :TS], kv_p, kv_s, win_p, win_s,
            gla_p, gla_s, hg_p, hg_s)
```

```python
import functools
import math

import numpy as np
import jax
import jax.numpy as jnp
from jax import lax
from jax.experimental import pallas as pl
from jax.experimental.pallas import tpu as pltpu

F32 = jnp.float32
BF16 = jnp.bfloat16

D_MODEL = 2048
DH = 128
H_A = 8
HKV = 2
GRP = H_A // HKV
L_CMP = 32
D_CMP = 16
CMP_R = L_CMP // D_CMP
CMP_HID = 2 * DH
L_SEL = 64
N_TOP = 16
WINDOW = 512
N_KV_SLOTS = 4
H_B = 4
DK_B = 128
DV_B = 256
GLA_LR = 16
GLA_GATE_NORM = 16.0
DK_C = 128
H_C = 16
DV_C = 128
EPS = 1e-6
FORCE_SCORE = 1e9
A_SIZES = (H_A * DH, 6 * HKV * DH, 3 * H_A, H_A * DH, H_B * DK_B, H_B * DK_B, H_B * DV_B, GLA_LR, H_B * DV_B)

LANES = 128
SUBLANES = 8
VMEM_LIMIT = 56 * 1024 * 1024

A_Q = 0
A_VB = 1024
A_ZB = 2048
A_ZA = 3072
A_QB = 4096
A_KB = 4608
A_KV = 5120
A_MISC = 6656
NA = 7168
MISC_LR = GRP * 3

NCP = 128
NSP = 128
TP = 8
NEG = -1e30
LOG2E = 1.4426950408889634


def _dot(a, b):
    return jnp.dot(a, b, preferred_element_type=F32)


def _dot_nt(a, b):
    return lax.dot_general(a, b, (((1,), (1,)), ((), ())), preferred_element_type=F32)


def _split3(a):
    hi = a.astype(BF16)
    r = a - hi.astype(F32)
    mid = r.astype(BF16)
    lo = (r - mid.astype(F32)).astype(BF16)
    return hi, mid, lo


def _dot_exact_rhs(a, b_exact):
    hi, mid, lo = _split3(a)
    return _dot(hi, b_exact) + _dot(mid, b_exact) + _dot(lo, b_exact)


def _dot_exact_lhs(a_exact, b):
    hi, mid, lo = _split3(b)
    return _dot(a_exact, hi) + _dot(a_exact, mid) + _dot(a_exact, lo)


def _sigmoid(x):
    return 1.0 / (1.0 + jnp.exp(-x))


def _silu(x):
    h = 0.5 * x
    return h + h * jnp.tanh(h)


def _log_sigmoid(x):
    return jnp.minimum(x, 0.0) - jnp.log(1.0 + jnp.exp(-jnp.abs(x)))


def _gelu_tanh(x):
    return 0.5 * x * (1.0 + jnp.tanh(np.sqrt(2.0 / np.pi) * (x + 0.044715 * (x * x * x))))


def _masked_softmax_pre(sm):
    m = jnp.max(sm, axis=-1, keepdims=True)
    e = jnp.where(sm > 0.5 * NEG, jnp.exp(sm - m), 0.0)
    l = jnp.sum(e, axis=-1, keepdims=True)
    return e / jnp.maximum(l, 1e-30)


def _params(sem):
    return pltpu.CompilerParams(dimension_semantics=sem, vmem_limit_bytes=VMEM_LIMIT)


def _norm_matmul_body(x_ref, nw_ref, w_ref, o_ref, *rest, copy_tile, w_rows):
    h_ref = rest[-1]

    @pl.when(pl.program_id(1) == 0)
    def _():
        x = x_ref[...]
        ms = jnp.mean(x * x, axis=-1, keepdims=True)
        h_ref[...] = (x * lax.rsqrt(ms + EPS) * nw_ref[...]).astype(BF16)

    res = _dot_nt(h_ref[...], w_ref[...]) if w_rows else _dot(h_ref[...], w_ref[...])
    o_ref[...] = res
    if copy_tile is not None:
        @pl.when(pl.program_id(1) == copy_tile)
        def _():
            rest[0][...] = res


def _norm_matmul(x, nw, w, tm, tn, copy_tile=None, w_rows=False):
    M, D = x.shape
    N = w.shape[0] if w_rows else w.shape[1]
    assert M % tm == 0 and N % tn == 0
    w_spec = pl.BlockSpec((tn, D), lambda i, j: (j, 0)) if w_rows else pl.BlockSpec((D, tn), lambda i, j: (0, j))
    out_specs = [pl.BlockSpec((tm, tn), lambda i, j: (i, j))]
    out_shape = [jax.ShapeDtypeStruct((M, N), F32)]
    if copy_tile is not None:
        out_specs.append(pl.BlockSpec((tm, tn), lambda i, j: (i, 0)))
        out_shape.append(jax.ShapeDtypeStruct((M, tn), F32))
    outs = pl.pallas_call(
        functools.partial(_norm_matmul_body, copy_tile=copy_tile, w_rows=w_rows),
        grid=(M // tm, N // tn),
        in_specs=[pl.BlockSpec((tm, D), lambda i, j: (i, 0)),
                  pl.BlockSpec((1, D), lambda i, j: (0, 0)),
                  w_spec],
        out_specs=out_specs,
        out_shape=out_shape,
        scratch_shapes=[pltpu.VMEM((tm, D), BF16)],
        compiler_params=_params(("arbitrary", "arbitrary")),
    )(x, nw, w)
    return outs[0] if copy_tile is None else outs


def _out_proj_body(a1_ref, a2_ref, w1_ref, w2_ref, x_ref, nw_ref, y_ref, *, final_norm):
    y = x_ref[...] + _dot(a1_ref[...].astype(BF16), w1_ref[...]) + _dot(a2_ref[...].astype(BF16), w2_ref[...])
    if final_norm:
        ms = jnp.mean(y * y, axis=-1, keepdims=True)
        y = y * lax.rsqrt(ms + EPS) * nw_ref[...]
    y_ref[...] = y


def _out_proj(a1, c1, a2, c2, w, x, nw, tm, final_norm):
    M, D = x.shape
    KH = w.shape[0] // 2
    assert M % tm == 0
    return pl.pallas_call(
        functools.partial(_out_proj_body, final_norm=final_norm),
        grid=(M // tm,),
        in_specs=[pl.BlockSpec((tm, KH), lambda i: (i, c1)),
                  pl.BlockSpec((tm, KH), lambda i: (i, c2)),
                  pl.BlockSpec((KH, D), lambda i: (0, 0)),
                  pl.BlockSpec((KH, D), lambda i: (1, 0)),
                  pl.BlockSpec((tm, D), lambda i: (i, 0)),
                  pl.BlockSpec((1, D), lambda i: (0, 0))],
        out_specs=pl.BlockSpec((tm, D), lambda i: (i, 0)),
        out_shape=jax.ShapeDtypeStruct((M, D), F32),
        compiler_params=_params(("arbitrary",)),
    )(a1, a2, w, w, x, nw)


def _cmp_bias_body(pe_ref, w1_ref, b1_ref, o_ref):
    pe = jnp.broadcast_to(pe_ref[...], (SUBLANES, pe_ref.shape[-1])).astype(BF16)
    o_ref[...] = b1_ref[...] + _dot(pe, w1_ref[...])[0:1, :]


def _cmp_bias(pe, w1, b1):
    S, KF, HID = w1.shape
    return pl.pallas_call(
        _cmp_bias_body,
        grid=(S,),
        in_specs=[pl.BlockSpec((None, 1, KF), lambda s: (s, 0, 0)),
                  pl.BlockSpec((None, KF, HID), lambda s: (s, 0, 0)),
                  pl.BlockSpec((None, 1, HID), lambda s: (s, 0, 0))],
        out_specs=pl.BlockSpec((None, 1, HID), lambda s: (s, 0, 0)),
        out_shape=jax.ShapeDtypeStruct((S, 1, HID), F32),
        compiler_params=_params(("arbitrary",)),
    )(pe, w1, b1)


def _compress_hidden(u0, u1, bias):
    return u0 + pltpu.roll(u1, NCP - 1, 0) + bias


def _compress_prompt_body(x_ref, w1_ref, cb_ref, w2_ref, o_ref):
    u0 = jnp.zeros((NCP, CMP_HID), F32)
    u1 = jnp.zeros((NCP, CMP_HID), F32)
    for pp in range(D_CMP // 2):
        xa = x_ref[pl.ds(2 * pp, NCP, stride=D_CMP), :]
        xb = x_ref[pl.ds(2 * pp + 1, NCP, stride=D_CMP), :]
        xp = jnp.concatenate([xa, xb], axis=1).astype(BF16)
        u0 = u0 + _dot(xp, w1_ref[pl.ds(pp * 2 * DH, 2 * DH), :])
        u1 = u1 + _dot(xp, w1_ref[pl.ds(D_CMP * DH + pp * 2 * DH, 2 * DH), :])
    h = _compress_hidden(u0, u1, cb_ref[...])
    o_ref[...] = _dot(_gelu_tanh(h).astype(BF16), w2_ref[...])


def _compress_prompt(P, w1, cb, w2, B, T):
    assert T // D_CMP == NCP
    kv0 = A_KV // DH
    return pl.pallas_call(
        _compress_prompt_body,
        grid=(B, 2, HKV),
        in_specs=[pl.BlockSpec((T, DH), lambda b, s, g: (b, kv0 + HKV * s + g)),
                  pl.BlockSpec((None, L_CMP * DH, CMP_HID), lambda b, s, g: (s, 0, 0)),
                  pl.BlockSpec((None, 1, CMP_HID), lambda b, s, g: (s, 0, 0)),
                  pl.BlockSpec((None, CMP_HID, DH), lambda b, s, g: (s, 0, 0))],
        out_specs=pl.BlockSpec((None, None, None, NCP, DH), lambda b, s, g: (b, s, g, 0, 0)),
        out_shape=jax.ShapeDtypeStruct((B, 2, HKV, NCP, DH), F32),
        compiler_params=_params(("arbitrary", "arbitrary", "arbitrary")),
    )(P, w1, cb, w2)


def _select_blocks(imp, tpos_tok, ns):
    blk = lax.broadcasted_iota(jnp.int32, (1, NSP), 1)
    cur = tpos_tok // L_SEL
    valid = blk <= cur
    forced = (blk == 0) | (blk == cur) | (blk == cur - 1)
    score = jnp.where(valid, jnp.where(forced, FORCE_SCORE, imp), -jnp.inf)
    k_top = float(min(N_TOP, ns))
    if score.shape[0] % LANES == 0:
        nb = -(-ns // SUBLANES) * SUBLANES
        rblk = lax.broadcasted_iota(jnp.int32, (nb, 1), 0)
        tops = []
        for t in range(score.shape[0] // LANES):
            st = score[t * LANES:(t + 1) * LANES].T[0:nb]
            rank = jnp.zeros(st.shape, F32)
            for j in range(ns):
                sj = st[j:j + 1, :]
                rank = rank + jnp.where((sj > st) | ((sj == st) & (rblk > j)), 1.0, 0.0)
            top = jnp.where(rank < k_top, 1.0, 0.0)
            tops.append(jnp.concatenate([top, jnp.zeros((NSP - nb, LANES), F32)], axis=0).T)
        top = tops[0] if len(tops) == 1 else jnp.concatenate(tops, axis=0)
        return jnp.where(valid, top, 0.0)
    rank = jnp.zeros(score.shape, F32)
    for j in range(ns):
        sj = score[:, j:j + 1]
        beats = (sj > score) | ((sj == score) & (blk > j))
        rank = rank + jnp.where(beats, 1.0, 0.0)
    return jnp.where(valid & (rank < float(min(N_TOP, ns))), 1.0, 0.0)


def _compressed_probs(qb, kc, tpos, nc, scale):
    s = _dot_nt(qb, kc.astype(BF16)) * scale
    cidx = lax.broadcasted_iota(jnp.int32, (1, NCP), 1)
    cmask = (cidx < nc) & (cidx * D_CMP + (L_CMP - 1) <= tpos)
    return _masked_softmax_pre(jnp.where(cmask, s, NEG))


def _compressed_branch(qb, kc, vc, tpos, nc, scale):
    p = _compressed_probs(qb, kc, tpos, nc, scale)
    return p, _dot(p.astype(BF16), vc.astype(BF16))


def _keys_with_block_onehot(k_ref, koh_ref, k0, width):
    return jnp.concatenate([k_ref[pl.ds(k0, width), :].astype(BF16), koh_ref[pl.ds(k0, width), :]], axis=1)


def _nsa_prompt_body(q_ref, kc_ref, vc_ref, ks_ref, vs_ref, kw_ref, vw_ref, misc_ref, z_ref, c2s_ref, koh_ref,
                     o_ref, sc_ref, mx_ref, l_ref, acc_ref, *, tq, tk, T):
    qi = pl.program_id(1)
    t0 = pl.multiple_of(qi * tq, tq)
    R = GRP * tq
    scale = DH ** -0.5
    c1 = scale * LOG2E
    ns = -(-T // L_SEL)
    nc = T // D_CMP - CMP_R + 1
    n_top = min(N_TOP, ns)
    heads = range(HKV)

    def hd(g):
        return slice(g * DH, (g + 1) * DH)

    qbs = [jnp.concatenate([q_ref[:, (g * GRP + r) * DH:(g * GRP + r + 1) * DH] for r in range(GRP)],
                           axis=0).astype(BF16) for g in heads]
    tpos = t0 + (lax.broadcasted_iota(jnp.int32, (R, 1), 0) & (tq - 1))
    tpos_tok = t0 + lax.broadcasted_iota(jnp.int32, (tq, 1), 0)
    blk = lax.broadcasted_iota(jnp.int32, (1, NSP), 1)

    nwt = WINDOW // tq + 1
    tiles, starts = [[] for _ in heads], []
    for i in range(nwt):
        k0 = t0 - WINDOW + i * tq
        k0c = pl.multiple_of(jnp.maximum(k0, 0), tq)
        kpos = k0 + lax.broadcasted_iota(jnp.int32, (1, tq), 1)
        for g in heads:
            s = _dot_nt(qbs[g], kw_ref[pl.ds(k0c, tq), hd(g)].astype(BF16))
            if i == 0:
                s = jnp.where((kpos > tpos - WINDOW) & (kpos >= 0), s, NEG)
            elif i == nwt - 1:
                s = jnp.where(kpos <= tpos, s, NEG)
            else:
                s = jnp.where(k0 >= 0, s, NEG)
            tiles[g].append(s)
        starts.append(k0c)
    es_w = []
    for g in heads:
        s = jnp.concatenate(tiles[g], axis=1)
        es_w.append(jnp.exp2((s - jnp.max(s, axis=-1, keepdims=True)) * c1))
    o_ws = [None for _ in heads]
    for i in range(nwt):
        for g in heads:
            pv = _dot(es_w[g][:, i * tq:(i + 1) * tq].astype(BF16), vw_ref[pl.ds(starts[i], tq), hd(g)].astype(BF16))
            o_ws[g] = pv if i == 0 else o_ws[g] + pv
    o_ws = [o_ws[g] / jnp.sum(es_w[g], axis=-1, keepdims=True) for g in heads]

    comp = [_compressed_branch(qbs[g], kc_ref[g], vc_ref[g], tpos, nc, scale) for g in heads]
    imps = []
    for g in heads:
        p_c = comp[g][0]
        psum = p_c[0:tq]
        for r in range(1, GRP):
            psum = psum + p_c[r * tq:(r + 1) * tq]
        imps.append(_dot_exact_rhs(psum, c2s_ref[...]))

    def all_valid():
        v = jnp.where(blk <= tpos_tok // L_SEL, 1.0, 0.0)
        return tuple(v for _ in heads)

    sels = lax.cond(t0 + tq <= n_top * L_SEL, all_valid,
                    lambda: tuple(_select_blocks(imps[g], tpos_tok, ns) for g in heads))

    q_augs, q_earlys = [], []
    for g in heads:
        sel_bias = (sels[g] - 1.0) * (-NEG)
        early_bias = jnp.where(blk < lax.div(t0, L_SEL), sel_bias, NEG)
        q_augs.append(jnp.concatenate([qbs[g], jnp.concatenate([sel_bias.astype(BF16)] * GRP, axis=0)], axis=1))
        q_earlys.append(jnp.concatenate([qbs[g], jnp.concatenate([early_bias.astype(BF16)] * GRP, axis=0)], axis=1))

    def keys_aug(k0, width, g):
        return jnp.concatenate([ks_ref[pl.ds(k0, width), hd(g)].astype(BF16), koh_ref[pl.ds(k0, width), :]], axis=1)

    kpos_d = t0 + lax.broadcasted_iota(jnp.int32, (1, tq), 1)
    s_diags = [jnp.where(kpos_d <= tpos, _dot_nt(q_augs[g], keys_aug(t0, tq, g)), NEG) for g in heads]
    n_early = lax.div(t0 + tk - 1, tk)
    for g in heads:
        mx_ref[g] = functools.reduce(jnp.maximum, [s_diags[g][:, c * LANES:(c + 1) * LANES]
                                                   for c in range(tq // LANES)])

    def pass1(kt, carry):
        k0 = pl.multiple_of(kt * tk, tk)
        ss = [_dot_nt(q_earlys[g], keys_aug(k0, tk, g)) for g in heads]
        old = [mx_ref[g] for g in heads]
        for g in heads:
            mx = ss[g][:, 0:LANES]
            for c in range(1, tk // LANES):
                mx = jnp.maximum(mx, ss[g][:, c * LANES:(c + 1) * LANES])
            sc_ref[g, kt] = ss[g]
            mx_ref[g] = jnp.maximum(old[g], mx)
        return carry

    lax.fori_loop(0, n_early, pass1, 0)
    m2s = [jnp.broadcast_to(jnp.max(mx_ref[g], axis=-1, keepdims=True) * c1, (R, LANES)) for g in heads]
    e_ds = [jnp.concatenate([jnp.exp2(s_diags[g][:, c * LANES:(c + 1) * LANES] * c1 - m2s[g])
                             for c in range(tq // LANES)], axis=1) for g in heads]
    pv_ds = [_dot(e_ds[g].astype(BF16), vs_ref[pl.ds(t0, tq), hd(g)].astype(BF16)) for g in heads]
    for g in heads:
        mx_ref[g] = m2s[g]
        l_ref[g] = sum(e_ds[g][:, c * LANES:(c + 1) * LANES] for c in range(tq // LANES))
        acc_ref[g] = pv_ds[g]

    def pass2(kt, carry):
        k0 = pl.multiple_of(kt * tk, tk)
        ss = [sc_ref[g, kt] for g in heads]
        m2 = [mx_ref[g] for g in heads]
        l_old = [l_ref[g] for g in heads]
        acc_old = [acc_ref[g] for g in heads]
        ebs, lsums = [], []
        for g in heads:
            es = [jnp.exp2(ss[g][:, c * LANES:(c + 1) * LANES] * c1 - m2[g]) for c in range(tk // LANES)]
            lsum = es[0]
            for ec in es[1:]:
                lsum = lsum + ec
            lsums.append(lsum)
            ebs.append(jnp.concatenate(es, axis=1).astype(BF16))
        pvs = [_dot(ebs[g], vs_ref[pl.ds(k0, tk), hd(g)].astype(BF16)) for g in heads]
        for g in heads:
            l_ref[g] = l_old[g] + lsums[g]
            acc_ref[g] = acc_old[g] + pvs[g]
        return carry

    lax.fori_loop(0, n_early, pass2, 0)

    outs = []
    for g in heads:
        o_s = acc_ref[g] / jnp.sum(l_ref[g], axis=-1, keepdims=True)
        gm = _sigmoid(misc_ref[:, g * LANES:g * LANES + GRP * 3])

        def gate(j):
            return jnp.concatenate([gm[:, r * 3 + j:r * 3 + j + 1] for r in range(GRP)], axis=0)

        outs.append(gate(0) * comp[g][1] + gate(1) * o_s + gate(2) * o_ws[g])
    for g in heads:
        for r in range(GRP):
            c = (g * GRP + r) * DH
            o_ref[:, c:c + DH] = (outs[g][r * tq:(r + 1) * tq] * _silu(z_ref[:, c:c + DH])).astype(BF16)


def _nsa_prompt(P, kcv, c2s, koh, B, T, tq, tk):
    assert T % tq == 0 and tq & (tq - 1) == 0 and tk % tq == 0 and T % tk == 0 and WINDOW % tq == 0
    nq = T // tq
    qw = H_A * DH
    kw = HKV * DH
    R = GRP * tq

    def kvspec(slot):
        return pl.BlockSpec((T, kw), lambda b, i: (b, A_KV // kw + slot))

    return pl.pallas_call(
        functools.partial(_nsa_prompt_body, tq=tq, tk=tk, T=T),
        grid=(B, nq),
        in_specs=[pl.BlockSpec((tq, qw), lambda b, i: (b * nq + i, A_Q // qw)),
                  pl.BlockSpec((None, None, HKV, NCP, DH), lambda b, i: (b, 0, 0, 0, 0)),
                  pl.BlockSpec((None, None, HKV, NCP, DH), lambda b, i: (b, 1, 0, 0, 0)),
                  kvspec(2), kvspec(3), kvspec(4), kvspec(5),
                  pl.BlockSpec((tq, HKV * LANES), lambda b, i: (b * nq + i, A_MISC // (HKV * LANES))),
                  pl.BlockSpec((tq, qw), lambda b, i: (b * nq + i, A_ZA // qw)),
                  pl.BlockSpec((NCP, NSP), lambda b, i: (0, 0)),
                  pl.BlockSpec((T, NSP), lambda b, i: (0, 0))],
        out_specs=pl.BlockSpec((tq, qw), lambda b, i: (b * nq + i, 0)),
        out_shape=jax.ShapeDtypeStruct((B * T, qw), BF16),
        scratch_shapes=[pltpu.VMEM((HKV, T // tk, R, tk), F32), pltpu.VMEM((HKV, R, LANES), F32),
                        pltpu.VMEM((HKV, R, LANES), F32), pltpu.VMEM((HKV, R, DH), F32)],
        compiler_params=_params(("arbitrary", "arbitrary")),
    )(P, kcv, kcv, P, P, P, P, P, P, c2s, koh)


def _pad_rows(x, rows):
    return jnp.concatenate([x, jnp.zeros((rows - x.shape[0], x.shape[1]), x.dtype)], axis=0)


def _nsa_sample_body(pt_ref, cache_ref, win_ref, q_ref, kvr_ref, kvw_ref, misc_ref, z_ref, w1_ref, cb_ref, w2_ref,
                     c2s_ref, koh_ref, o_ref, wout_ref, pbuf, sem, *, n_pages, page, past_len, dec_seq, wb):
    b = pl.program_id(0)
    scale = DH ** -0.5
    c1 = scale * LOG2E
    L = past_len + dec_seq
    ns = -(-L // L_SEL)
    nc = L // D_CMP - CMP_R + 1
    R = GRP * TP
    per_page = page // D_CMP
    nch = N_KV_SLOTS * HKV
    grp_rows = D_CMP * nch
    pitch = grp_rows + 1

    def page_copies(seq, slot):
        cps = []
        for j in range(n_pages):
            pid = pt_ref[seq * n_pages + j]
            for n in range(per_page):
                cps.append(pltpu.make_async_copy(
                    cache_ref.at[pid, pl.ds(n * grp_rows, grp_rows), :],
                    pbuf.at[slot, pl.ds((j * per_page + n) * pitch, grp_rows), :], sem.at[slot]))
        return cps

    slot = lax.rem(b, 2)

    @pl.when(b == 0)
    def _():
        for cp in page_copies(0, 0):
            cp.start()

    for cp in page_copies(b, slot):
        cp.wait()

    last = pl.num_programs(0) - 1
    next_copies = page_copies(jnp.minimum(b + 1, last), 1 - slot)
    copies_per_group = len(next_copies) // (2 * (D_CMP // 2))

    def gather(p, ch):
        return pbuf[slot, pl.ds(p * nch + ch, NCP, stride=pitch), :]

    trow = lax.broadcasted_iota(jnp.int32, (R, 1), 0) & (TP - 1)
    tpos = past_len + trow
    tpos_tok = past_len + lax.broadcasted_iota(jnp.int32, (TP, 1), 0)
    lane = lax.broadcasted_iota(jnp.int32, (1, LANES), 1)
    new_pos = past_len + lane
    new_ok = (new_pos <= tpos) & (new_pos < L)
    new_blk = (past_len + lax.broadcasted_iota(jnp.int32, (LANES, 1), 0)) // L_SEL
    new_oh = jnp.where(lax.broadcasted_iota(jnp.int32, (LANES, NSP), 1) == new_blk, 1.0, 0.0).astype(BF16)
    wpos = (past_len - wb) + lax.broadcasted_iota(jnp.int32, (1, wb), 1)
    wch = 2 * HKV

    heads = range(HKV)
    n_iter = D_CMP // 2
    qbs = [jnp.concatenate([q_ref[:, (g * GRP + r) * DH:(g * GRP + r + 1) * DH] for r in range(GRP)],
                           axis=0).astype(BF16) for g in heads]
    raw = [[] for _ in heads]
    win = {}

    def stage_window_scores():
        es = []
        for g in heads:
            s_w = _dot_nt(qbs[g], win_ref[pl.ds(g, wb, stride=wch), :].astype(BF16))
            s_w = jnp.where((wpos <= tpos) & (wpos > tpos - WINDOW), s_w, NEG)
            s_n = _dot_nt(qbs[g], _pad_rows(kvw_ref[:, g * DH:(g + 1) * DH], LANES).astype(BF16))
            s_n = jnp.where(new_ok & (new_pos > tpos - WINDOW), s_n, NEG)
            s = jnp.concatenate([s_w, s_n], axis=1)
            es.append(jnp.exp2((s - jnp.max(s, axis=-1, keepdims=True)) * c1))
        win["e"] = es

    def stage_window_out():
        outs_w = []
        for g in heads:
            e = win["e"][g]
            o_w = (_dot(e[:, :wb].astype(BF16), win_ref[pl.ds(HKV + g, wb, stride=wch), :].astype(BF16))
                   + _dot(e[:, wb:].astype(BF16),
                          _pad_rows(kvw_ref[:, (HKV + g) * DH:(HKV + g + 1) * DH], LANES).astype(BF16)))
            outs_w.append(o_w / jnp.sum(e, axis=-1, keepdims=True))
        win["o"] = outs_w

    def stage_raw_scores(p_lo, p_hi):
        for p in range(p_lo, p_hi):
            for g in heads:
                raw[g].append(_dot_nt(qbs[g], gather(p, 2 * HKV + g).astype(BF16)))

    def stage_raw_new():
        for g in heads:
            ksel_col = (2 * HKV + g) * DH
            raw[g].append(_dot_nt(qbs[g], _pad_rows(kvr_ref[:, ksel_col:ksel_col + DH], LANES).astype(BF16)))


    def compress_step(s, pp, u0, u1):
        xs = [jnp.concatenate([gather(2 * pp, s * HKV + g), gather(2 * pp + 1, s * HKV + g)], axis=1) for g in heads]
        xp = jnp.concatenate(xs, axis=0).astype(BF16)
        u0 = u0 + _dot(xp, w1_ref[s, pl.ds(pp * 2 * DH, 2 * DH), :])
        u1 = u1 + _dot(xp, w1_ref[s, pl.ds(D_CMP * DH + pp * 2 * DH, 2 * DH), :])
        grp = s * n_iter + pp
        for cp in next_copies[grp * copies_per_group:(grp + 1) * copies_per_group]:
            cp.start()
        return u0, u1

    def compress_out(s, u0, u1):
        return [_dot(_gelu_tanh(_compress_hidden(u0[g * NCP:(g + 1) * NCP], u1[g * NCP:(g + 1) * NCP],
                                                 cb_ref[s])).astype(BF16), w2_ref[s]) for g in heads]

    quarter = D_CMP // 4
    early = {0: stage_window_scores, 1: lambda: stage_raw_scores(0, quarter),
             2: lambda: stage_raw_scores(quarter, 2 * quarter), 3: stage_window_out,
             4: lambda: stage_raw_scores(2 * quarter, 3 * quarter),
             5: lambda: stage_raw_scores(3 * quarter, D_CMP), 6: stage_raw_new}
    u0 = u1 = jnp.zeros((HKV * NCP, CMP_HID), F32)
    for pp in range(n_iter):
        u0, u1 = compress_step(0, pp, u0, u1)
        if pp in early:
            early[pp]()
    kc = compress_out(0, u0, u1)

    st = {}

    def stage_probs():
        st["p_c"] = [_compressed_probs(qbs[g], kc[g], tpos, nc, scale) for g in heads]

    def stage_select():
        imps = []
        for g in heads:
            p_c = st["p_c"][g]
            psum = p_c[0:TP]
            for r in range(1, GRP):
                psum = psum + p_c[r * TP:(r + 1) * TP]
            imps.append(_dot_exact_rhs(psum, c2s_ref[...]))
        st["sel"] = [_select_blocks(imps[g], tpos_tok, ns) for g in heads]

    def stage_scores():
        biases = [jnp.concatenate([((st["sel"][g] - 1.0) * (-NEG)).astype(BF16)] * GRP, axis=0) for g in heads]
        bias_past = [_dot_nt(biases[g], koh_ref[...]) for g in heads]
        bias_new = [jnp.where(new_ok, _dot_nt(biases[g], new_oh), NEG) for g in heads]
        es = []
        for g in heads:
            s = jnp.concatenate([pc + bias_past[g] for pc in raw[g][:D_CMP]] + [raw[g][D_CMP] + bias_new[g]], axis=1)
            es.append(jnp.exp2((s - jnp.max(s, axis=-1, keepdims=True)) * c1))
        st["e"] = es
        st["o_s"] = [_dot(es[g][:, D_CMP * NCP:].astype(BF16),
                          _pad_rows(kvr_ref[:, (3 * HKV + g) * DH:(3 * HKV + g + 1) * DH], LANES).astype(BF16))
                     for g in heads]

    def stage_pv(p_lo, p_hi):
        for p in range(p_lo, p_hi):
            for g in heads:
                st["o_s"][g] = st["o_s"][g] + _dot(st["e"][g][:, p * NCP:(p + 1) * NCP].astype(BF16),
                                                  gather(p, 3 * HKV + g).astype(BF16))

    after = {0: stage_probs, 3: stage_select, 6: stage_scores, 7: lambda: stage_pv(0, D_CMP // 2)}
    u0 = u1 = jnp.zeros((HKV * NCP, CMP_HID), F32)
    for pp in range(n_iter):
        u0, u1 = compress_step(1, pp, u0, u1)
        if pp in after:
            after[pp]()
    vc = compress_out(1, u0, u1)
    stage_pv(D_CMP // 2, D_CMP)

    outs = []
    for g in heads:
        o_c = _dot(st["p_c"][g].astype(BF16), vc[g].astype(BF16))
        o_s = st["o_s"][g] / jnp.sum(st["e"][g], axis=-1, keepdims=True)
        gm = _sigmoid(misc_ref[:, g * LANES:g * LANES + GRP * 3])

        def gate(j):
            return jnp.concatenate([gm[:, r * 3 + j:r * 3 + j + 1] for r in range(GRP)], axis=0)

        outs.append(gate(0) * o_c + gate(1) * o_s + gate(2) * win["o"][g])

    for g in range(HKV):
        for r in range(GRP):
            c = (g * GRP + r) * DH
            o_ref[:, c:c + DH] = outs[g][r * TP:(r + 1) * TP] * _silu(z_ref[:, c:c + DH])

    wout_ref[pl.ds(0, (wb - dec_seq) * wch), :] = win_ref[pl.ds(dec_seq * wch, (wb - dec_seq) * wch), :]
    for t in range(dec_seq):
        for ch in range(wch):
            wout_ref[pl.ds((wb - dec_seq + t) * wch + ch, 1), :] = kvw_ref[t:t + 1, ch * DH:(ch + 1) * DH]

    @pl.when(b == last)
    def _():
        for cp in next_copies:
            cp.wait()


def _nsa_sample(Ps, cache, cache_win, page_table, w1, cb, w2, c2s, dec_seq):
    NB, n_pages = page_table.shape
    n_pool, prow, _ = cache.shape
    page = prow // (N_KV_SLOTS * HKV)
    wb = cache_win.shape[1] // (2 * HKV)
    past_len = n_pages * page
    L = past_len + dec_seq
    assert wb == WINDOW and (L // D_CMP) == NCP and NCP * D_CMP == past_len and dec_seq <= TP
    assert -(-L // L_SEL) < NSP and page % D_CMP == 0 and (dec_seq * 2 * HKV) % SUBLANES == 0
    qw = H_A * DH
    nch = N_KV_SLOTS * HKV
    koh = jnp.asarray(np.arange(NCP)[:, None] * D_CMP // L_SEL == np.arange(NSP)[None, :], dtype=BF16)

    grid_spec = pltpu.PrefetchScalarGridSpec(
        num_scalar_prefetch=1,
        grid=(NB,),
        in_specs=[
            pl.BlockSpec(memory_space=pl.ANY),
            pl.BlockSpec((None, wb * 2 * HKV, DH), lambda b, pt: (b, 0, 0)),
            pl.BlockSpec((TP, qw), lambda b, pt: (b, A_Q // qw)),
            pl.BlockSpec((TP, 4 * HKV * DH), lambda b, pt: (b, A_KV // (4 * HKV * DH))),
            pl.BlockSpec((TP, 2 * HKV * DH), lambda b, pt: (b, (A_KV + 4 * HKV * DH) // (2 * HKV * DH))),
            pl.BlockSpec((TP, HKV * LANES), lambda b, pt: (b, A_MISC // (HKV * LANES))),
            pl.BlockSpec((TP, qw), lambda b, pt: (b, A_ZA // qw)),
            pl.BlockSpec((2, L_CMP * DH, CMP_HID), lambda b, pt: (0, 0, 0)),
            pl.BlockSpec((2, 1, CMP_HID), lambda b, pt: (0, 0, 0)),
            pl.BlockSpec((2, CMP_HID, DH), lambda b, pt: (0, 0, 0)),
            pl.BlockSpec((NCP, NSP), lambda b, pt: (0, 0)),
            pl.BlockSpec((NCP, NSP), lambda b, pt: (0, 0))],
        out_specs=[pl.BlockSpec((TP, qw), lambda b, pt: (b, 0)),
                   pl.BlockSpec((None, wb * 2 * HKV, DH), lambda b, pt: (b, 0, 0))],
        scratch_shapes=[pltpu.VMEM((2, NCP * (D_CMP * nch + 1), DH), F32), pltpu.SemaphoreType.DMA((2,))],
    )
    return pl.pallas_call(
        functools.partial(_nsa_sample_body, n_pages=n_pages, page=page, past_len=past_len, dec_seq=dec_seq, wb=wb),
        grid_spec=grid_spec,
        out_shape=[jax.ShapeDtypeStruct((NB * TP, qw), F32),
                   jax.ShapeDtypeStruct((NB, wb * 2 * HKV, DH), F32)],
        compiler_params=_params(("arbitrary",)),
    )(page_table.reshape(-1), cache, cache_win, Ps, Ps, Ps, Ps, Ps, w1, cb, w2, c2s, koh)


def _rec_chunk(q, k, v, la, gate, S, C, last_row, nh, K, V):
    sb = min(16, C)
    ri = lax.broadcasted_iota(jnp.int32, (C, C), 0)
    cj = lax.broadcasted_iota(jnp.int32, (C, C), 1)
    causal = cj <= ri
    tri = jnp.where(causal, 1.0, 0.0).astype(BF16)
    b = _dot_exact_lhs(tri, la)
    qe = (q * jnp.exp(b)).astype(BF16)
    qis, kis = [], []
    k_ref_prev, c_prev = None, None
    for i in range(C // sb):
        r0 = i * sb
        ci = b[r0 + sb // 2:r0 + sb // 2 + 1, :]
        qis.append((q[r0:r0 + sb] * jnp.exp(b[r0:r0 + sb] - ci)).astype(BF16))
        k_new = k[r0:r0 + sb] * jnp.exp(jnp.minimum(ci - b[r0:r0 + sb], 80.0))
        k_scaled = k_new if i == 0 else jnp.concatenate([k_ref_prev * jnp.exp(ci - c_prev), k_new], axis=0)
        k_ref_prev, c_prev = k_scaled, ci
        rest = C - r0 - sb
        kis.append((k_scaled if rest == 0 else
                    jnp.concatenate([k_scaled, jnp.zeros((rest, k.shape[1]), F32)], axis=0)).astype(BF16))
    b_last = b[last_row:last_row + 1, :]
    rowi = lax.broadcasted_iota(jnp.int32, (C, 1), 0)
    kd = jnp.where(rowi <= last_row, k * jnp.exp(jnp.minimum(b_last - b, 0.0)), 0.0)
    stack = jnp.concatenate([kd, jnp.broadcast_to(b_last, (SUBLANES, nh * K)),
                             jnp.zeros((LANES - C - SUBLANES, nh * K), F32)], axis=0)
    vb = v.astype(BF16)
    v_pad = jnp.concatenate([vb, jnp.zeros((LANES - C, nh * V), BF16)], axis=0)
    outs, s_new = [], []
    for h in range(nh):
        ks = slice(h * K, (h + 1) * K)
        vs = slice(h * V, (h + 1) * V)
        o = _dot(qe[:, ks], S[h].astype(BF16))
        rows = [_dot_nt(qi[:, ks], ki[:, ks]) for qi, ki in zip(qis, kis)]
        att = rows[0] if len(rows) == 1 else jnp.concatenate(rows, axis=0)
        att = jnp.where(causal, att, 0.0)
        o = o + _dot(att.astype(BF16), vb[:, vs])
        stack_t = stack[:, ks].T
        a_col = jnp.exp(stack_t[:, C:C + 1])
        s_new.append(a_col * S[h] + _dot(stack_t.astype(BF16), v_pad[:, vs]))
        ms = jnp.mean(o * o, axis=-1, keepdims=True)
        outs.append(o * lax.rsqrt(ms + EPS) * gate[:, vs])
    return outs, s_new


def _rec_body(*refs, variant, nh, K, V, C, TB, SB, last_row, has_s0, layer_idx):
    refs = list(refs)
    q_ref, k_ref, v_ref, z_ref = refs[:4]
    pos = 4
    if variant == "gla":
        misc_ref, w2_ref, gb_ref = refs[pos:pos + 3]
        pos += 3
    else:
        lb_ref = refs[pos]
        pos += 1
    gn_ref = refs[pos]
    pos += 1
    if has_s0:
        s0_ref = refs[pos]
        pos += 1
    o_ref, s_ref = refs[pos:pos + 2]
    n_chunks = TB // C
    single = n_chunks == 1 and has_s0

    if not single:
        @pl.when(pl.program_id(2) == 0)
        def _():
            if has_s0:
                s_ref[...] = s0_ref[...]
            else:
                s_ref[...] = jnp.zeros(s_ref.shape, F32)

    if variant == "hgrn":
        lg = lb_ref[...]
        e = jnp.exp(lg - jnp.max(lg, axis=0, keepdims=True))
        prob = e / jnp.sum(e, axis=0, keepdims=True)
        lb = prob[1:2]
        for i in range(2, layer_idx + 1):
            lb = lb + prob[i:i + 1]
        if SB > 1:
            lb = jnp.concatenate([lb] * SB, axis=1)
    gn_all = jnp.concatenate([gn_ref[...]] * (nh * SB), axis=1)

    def chunk(c0):
        def rows(s):
            return pl.ds(s * TB + c0, C)

        def cat(f):
            return f(0) if SB == 1 else jnp.concatenate([f(s) for s in range(SB)], axis=1)

        qr = cat(lambda s: q_ref[rows(s), :])
        kr = cat(lambda s: k_ref[rows(s), :])
        v = cat(lambda s: v_ref[rows(s), :])
        gate = _silu(cat(lambda s: z_ref[rows(s), :])) * gn_all
        if variant == "gla":
            w2 = w2_ref[...].astype(BF16)
            zg = cat(lambda s: _dot(misc_ref[rows(s), :].astype(BF16), w2) + gb_ref[...])
            q, k, la = qr * (K ** -0.5), kr, _log_sigmoid(zg) / GLA_GATE_NORM
        else:
            t = jnp.exp(-jnp.abs(kr))
            r = 1.0 / (1.0 + t)
            tr = t * r
            nonneg = kr >= 0.0
            sig = jnp.where(nonneg, r, tr)
            nsig = jnp.where(nonneg, tr, r)
            q, k, la = _silu(qr), (1.0 - lb) * nsig, jnp.log(lb + (1.0 - lb) * sig)
        src = s0_ref if single else s_ref
        states = [src[s, h] for s in range(SB) for h in range(nh)]
        outs, s_new = _rec_chunk(q, k, v, la, gate, states, C, last_row, SB * nh, K, V)
        for s in range(SB):
            for h in range(nh):
                o_ref[rows(s), h * V:(h + 1) * V] = outs[s * nh + h].astype(o_ref.dtype)
                s_ref[s, h] = s_new[s * nh + h]

    if n_chunks == 1:
        chunk(0)
    else:
        def body(ci, carry):
            chunk(pl.multiple_of(ci * C, C))
            return carry

        lax.fori_loop(0, n_chunks, body, 0, unroll=8)


def _recurrence(P, cols, extra, gn, s0, *, variant, B, T, H, K, V, C, TB, nh, last_row, out_dtype, SB=1,
                layer_idx=1):
    cq, ck, cv, cz = cols
    assert H % nh == 0 and T % TB == 0 and TB % C == 0 and B % SB == 0
    nt = T // TB
    assert SB == 1 or nt == 1

    def colspec(c0, w):
        assert c0 % (nh * w) == 0
        return pl.BlockSpec((SB * TB, nh * w), lambda b, j, t: (b * nt + t, c0 // (nh * w) + j))

    in_specs = [colspec(cq, K), colspec(ck, K), colspec(cv, V), colspec(cz, V)]
    args = [P, P, P, P]
    if variant == "gla":
        w2p, gb = extra
        in_specs += [pl.BlockSpec((SB * TB, LANES), lambda b, j, t: (b * nt + t, A_MISC // LANES)),
                     pl.BlockSpec((LANES, nh * K), lambda b, j, t: (0, j)),
                     pl.BlockSpec((1, nh * K), lambda b, j, t: (0, j))]
        args += [P, w2p, gb]
    else:
        (lb_logits,) = extra
        in_specs += [pl.BlockSpec((lb_logits.shape[0], nh * K), lambda b, j, t: (0, j))]
        args += [lb_logits]
    in_specs += [pl.BlockSpec((1, V), lambda b, j, t: (0, 0))]
    args += [gn]
    st_spec = pl.BlockSpec((None, SB, nh, K, V), lambda b, j, t: (0, b, j, 0, 0))
    if s0 is not None:
        in_specs += [st_spec]
        args += [s0]
    return pl.pallas_call(
        functools.partial(_rec_body, variant=variant, nh=nh, K=K, V=V, C=C, TB=TB, SB=SB, last_row=last_row,
                          has_s0=s0 is not None, layer_idx=layer_idx),
        grid=(B // SB, H // nh, nt),
        in_specs=in_specs,
        out_specs=[pl.BlockSpec((SB * TB, nh * V), lambda b, j, t: (b * nt + t, j)), st_spec],
        out_shape=[jax.ShapeDtypeStruct((B * T, H * V), out_dtype),
                   jax.ShapeDtypeStruct((1, B, H, K, V), F32)],
        compiler_params=_params(("arbitrary", "arbitrary", "arbitrary")),
    )(*args)


def _layout_a_w_in_body(w_ref, o_ref):
    off = [int(v) for v in np.concatenate([[0], np.cumsum(A_SIZES)])]
    q, kv, gbr, za, qb, kb, vb, lr, zb = [w_ref[off[i]:off[i + 1], :] for i in range(len(A_SIZES))]
    cols = w_ref.shape[1]
    z = lambda n: jnp.zeros((n, cols), F32)
    misc0 = jnp.concatenate([gbr[:GRP * 3], lr, z(LANES - GRP * 3 - GLA_LR)], axis=0)
    misc1 = jnp.concatenate([gbr[GRP * 3:], z(LANES - GRP * 3)], axis=0)
    out = jnp.concatenate([q, vb, zb, za, qb, kb, kv, misc0, misc1, z(NA - A_MISC - HKV * LANES)], axis=0)
    o_ref[...] = out.astype(BF16)


def _layout_a_w_in(wt, tc=256):
    n, D = wt.shape
    assert D % tc == 0 and n == sum(A_SIZES)
    return pl.pallas_call(
        _layout_a_w_in_body,
        grid=(D // tc,),
        in_specs=[pl.BlockSpec((n, tc), lambda i: (0, i))],
        out_specs=pl.BlockSpec((NA, tc), lambda i: (0, i)),
        out_shape=jax.ShapeDtypeStruct((NA, D), BF16),
        compiler_params=_params(("arbitrary",)),
    )(wt)


def _cmp_to_sel():
    c_start = np.arange(NCP)[:, None] * D_CMP
    s_start = np.arange(NSP)[None, :] * L_SEL
    overlap = np.clip(np.minimum(c_start + L_CMP, s_start + L_SEL) - np.maximum(c_start, s_start), 0, None)
    return jnp.asarray(overlap / D_CMP, dtype=BF16)


def kernel(x_prompt, x_sample, cache_kv, cache_win, state_gla, state_hgrn, page_table, a_norm, a_w_in, a_gla_w2,
           a_gla_b, a_gla_gn, a_cmp_pe, a_cmp_w1, a_cmp_b1, a_cmp_w2, a_w_out, c_norm, c_w_in, c_lb_logits, c_gn,
           c_w_out, final_norm):
    B, T, D = x_prompt.shape
    NB, TS, _ = x_sample.shape
    n_pool, page = cache_kv.shape[1], cache_kv.shape[2]
    wb = cache_win.shape[2]
    assert a_norm.shape[0] == 1 and c_norm.shape[0] == 1 and c_lb_logits.shape[0] == 2

    wa = _layout_a_w_in(a_w_in[0].T)
    wc = c_w_in[0].astype(BF16)
    wao = a_w_out[0].astype(BF16)
    wco = c_w_out[0].astype(BF16)
    w1 = a_cmp_w1[0].astype(BF16)
    w2 = a_cmp_w2[0].astype(BF16)
    pe = a_cmp_pe[0].reshape(2, 1, L_CMP * DH)
    b1 = a_cmp_b1[0].reshape(2, 1, CMP_HID)
    w2p = jnp.zeros((LANES, H_B * DK_B), F32).at[MISC_LR:MISC_LR + GLA_LR, :].set(a_gla_w2[0])
    gb = a_gla_b[0].reshape(1, H_B * DK_B)
    c2s = _cmp_to_sel()
    a_nw = a_norm[0].reshape(1, D)
    c_nw = c_norm[0].reshape(1, D)
    f_nw = final_norm.reshape(1, D)
    gla_gn = a_gla_gn[0].reshape(1, DV_B)
    hg_gn = c_gn[0].reshape(1, DV_C)

    xp = x_prompt.reshape(B * T, D)
    xs = jnp.pad(x_sample, ((0, 0), (0, TP - TS), (0, 0))).reshape(NB * TP, D)

    cb = _cmp_bias(pe, w1, b1)

    rows_w = N_KV_SLOTS * HKV * DH
    Pp, kv_rows_p = _norm_matmul(xp, a_nw, wa, tm=1024, tn=rows_w, copy_tile=A_KV // rows_w, w_rows=True)
    kcv = _compress_prompt(Pp, w1, cb, w2, B, T)
    koh = jnp.asarray(np.arange(T)[:, None] // L_SEL == np.arange(NSP)[None, :], dtype=BF16)
    oa_p = _nsa_prompt(Pp, kcv, c2s, koh, B, T, tq=256, tk=512)
    ob_p, gla_p = _recurrence(Pp, (A_QB, A_KB, A_VB, A_ZB), (w2p, gb), gla_gn, None, variant="gla", B=B, T=T,
                              H=H_B, K=DK_B, V=DV_B, C=64, TB=512, nh=4, last_row=63, out_dtype=BF16)
    y1p = _out_proj(oa_p, 0, ob_p, 0, wao, xp, a_nw, tm=512, final_norm=False)

    Ps = _norm_matmul(xs, a_nw, wa, tm=NB * TP, tn=1024, w_rows=True)
    cache2 = cache_kv[0].reshape(n_pool, page * N_KV_SLOTS * HKV, DH)
    win2 = cache_win[0].reshape(NB, wb * 2 * HKV, DH)
    oa_s, win_out = _nsa_sample(Ps, cache2, win2, page_table, w1, cb, w2, c2s, TS)
    ob_s, gla_s = _recurrence(Ps, (A_QB, A_KB, A_VB, A_ZB), (w2p, gb), gla_gn, state_gla, variant="gla", B=NB, T=TP,
                              H=H_B, K=DK_B, V=DV_B, C=TP, TB=TP, nh=H_B, last_row=TS - 1, out_dtype=F32,
                              SB=math.gcd(NB, 8))
    y1s = _out_proj(oa_s, 0, ob_s, 0, wao, xs, a_nw, tm=min(512, NB * TP), final_norm=False)

    Pc = _norm_matmul(y1p, c_nw, wc, tm=1024, tn=1024)
    hc = H_C * DK_C
    oc_p, hg_p = _recurrence(Pc, (0, hc, 2 * hc, 3 * hc), (c_lb_logits,), hg_gn, None, variant="hgrn", B=B, T=T,
                             H=H_C, K=DK_C, V=DV_C, C=64, TB=512, nh=8, last_row=63, out_dtype=BF16)
    y_prompt = _out_proj(oc_p, 0, oc_p, 1, wco, y1p, f_nw, tm=512, final_norm=True)

    Pcs = _norm_matmul(y1s, c_nw, wc, tm=NB * TP, tn=1024)
    oc_s, hg_s = _recurrence(Pcs, (0, hc, 2 * hc, 3 * hc), (c_lb_logits,), hg_gn, state_hgrn, variant="hgrn", B=NB,
                             T=TP, H=H_C, K=DK_C, V=DV_C, C=TP, TB=TP, nh=H_C, last_row=TS - 1, out_dtype=F32,
                             SB=math.gcd(NB, 4))
    y_sample = _out_proj(oc_s, 0, oc_s, 1, wco, y1s, f_nw, tm=min(512, NB * TP), final_norm=True)

    kvw = 6 * HKV * DH
    nrw = N_KV_SLOTS * HKV * DH
    Pp3 = Pp.reshape(B, T, NA)
    Ps3 = Ps.reshape(NB, TP, NA)
    kv_p = kv_rows_p.reshape(1, B, T, N_KV_SLOTS, HKV, DH)
    kv_s = Ps3[:, :TS, A_KV:A_KV + nrw].reshape(1, NB, TS, N_KV_SLOTS, HKV, DH)
    keep = min(WINDOW, T)
    win_p = Pp3[:, T - keep:, A_KV + nrw:A_KV + kvw].reshape(1, B, keep, 2, HKV, DH)
    win_s = win_out.reshape(1, NB, wb, 2, HKV, DH)
    return (y_prompt.reshape(B, T, D), y_sample.reshape(NB, TP, D)[:, :TS], kv_p, kv_s, win_p, win_s,
            gla_p, gla_s, hg_p, hg_s)
```

```python
import functools
import math
from typing import NamedTuple

import numpy as np
import jax
import jax.numpy as jnp
from jax import lax
from jax.experimental import pallas as pl
from jax.experimental.pallas import tpu as pltpu

F32 = jnp.float32
BF16 = jnp.bfloat16

D_MODEL = 2048
DH = 128
H_A = 8
HKV = 2
GRP = H_A // HKV
L_CMP = 32
D_CMP = 16
CMP_R = L_CMP // D_CMP
CMP_HID = 2 * DH
L_SEL = 64
N_TOP = 16
WINDOW = 512
N_KV_SLOTS = 4
H_B = 4
DK_B = 128
DV_B = 256
GLA_LR = 16
GLA_GATE_NORM = 16.0
DK_C = 128
H_C = 16
DV_C = 128
EPS = 1e-6
FORCE_SCORE = 1e9
A_SIZES = (H_A * DH, 6 * HKV * DH, 3 * H_A, H_A * DH, H_B * DK_B, H_B * DK_B, H_B * DV_B, GLA_LR, H_B * DV_B)

LANES = 128
SUBLANES = 8
VMEM_LIMIT = 56 * 1024 * 1024

A_Q = 0
A_VB = 1024
A_ZB = 2048
A_ZA = 3072
A_QB = 4096
A_KB = 4608
A_KV = 5120
A_MISC = 6656
NA = 7168
MISC_LR = GRP * 3

NCP = 128
NSP = 128
TP = 8
NEG = -1e30
LOG2E = 1.4426950408889634


def _dot(a, b):
    return jnp.dot(a, b, preferred_element_type=F32)


def _dot_nt(a, b):
    return lax.dot_general(a, b, (((1,), (1,)), ((), ())), preferred_element_type=F32)


def _split3(a):
    hi = a.astype(BF16)
    r = a - hi.astype(F32)
    mid = r.astype(BF16)
    lo = (r - mid.astype(F32)).astype(BF16)
    return hi, mid, lo


def _dot_exact_rhs(a, b_exact):
    hi, mid, lo = _split3(a)
    return _dot(hi, b_exact) + _dot(mid, b_exact) + _dot(lo, b_exact)


def _dot_exact_lhs(a_exact, b):
    hi, mid, lo = _split3(b)
    return _dot(a_exact, hi) + _dot(a_exact, mid) + _dot(a_exact, lo)


def _sigmoid(x):
    return 1.0 / (1.0 + jnp.exp(-x))


def _silu(x):
    h = 0.5 * x
    return h + h * jnp.tanh(h)


def _log_sigmoid(x):
    return jnp.minimum(x, 0.0) - jnp.log(1.0 + jnp.exp(-jnp.abs(x)))


def _gelu_tanh(x):
    return 0.5 * x * (1.0 + jnp.tanh(np.sqrt(2.0 / np.pi) * (x + 0.044715 * (x * x * x))))


def _masked_softmax_pre(sm):
    m = jnp.max(sm, axis=-1, keepdims=True)
    e = jnp.where(sm > 0.5 * NEG, jnp.exp(sm - m), 0.0)
    l = jnp.sum(e, axis=-1, keepdims=True)
    return e / jnp.maximum(l, 1e-30)


def _params(sem):
    return pltpu.CompilerParams(dimension_semantics=sem, vmem_limit_bytes=VMEM_LIMIT)


def _norm_matmul_body(x_ref, nw_ref, w_ref, o_ref, *rest, copy_tile, w_rows):
    h_ref = rest[-1]

    @pl.when(pl.program_id(1) == 0)
    def _():
        x = x_ref[...]
        ms = jnp.mean(x * x, axis=-1, keepdims=True)
        h_ref[...] = (x * lax.rsqrt(ms + EPS) * nw_ref[...]).astype(BF16)

    res = _dot_nt(h_ref[...], w_ref[...]) if w_rows else _dot(h_ref[...], w_ref[...])
    o_ref[...] = res
    if copy_tile is not None:
        @pl.when(pl.program_id(1) == copy_tile)
        def _():
            rest[0][...] = res


def _norm_matmul(x, nw, w, tm, tn, copy_tile=None, w_rows=False):
    M, D = x.shape
    N = w.shape[0] if w_rows else w.shape[1]
    assert M % tm == 0 and N % tn == 0
    w_spec = pl.BlockSpec((tn, D), lambda i, j: (j, 0)) if w_rows else pl.BlockSpec((D, tn), lambda i, j: (0, j))
    out_specs = [pl.BlockSpec((tm, tn), lambda i, j: (i, j))]
    out_shape = [jax.ShapeDtypeStruct((M, N), F32)]
    if copy_tile is not None:
        out_specs.append(pl.BlockSpec((tm, tn), lambda i, j: (i, 0)))
        out_shape.append(jax.ShapeDtypeStruct((M, tn), F32))
    outs = pl.pallas_call(
        functools.partial(_norm_matmul_body, copy_tile=copy_tile, w_rows=w_rows),
        grid=(M // tm, N // tn),
        in_specs=[pl.BlockSpec((tm, D), lambda i, j: (i, 0)),
                  pl.BlockSpec((1, D), lambda i, j: (0, 0)),
                  w_spec],
        out_specs=out_specs,
        out_shape=out_shape,
        scratch_shapes=[pltpu.VMEM((tm, D), BF16)],
        compiler_params=_params(("arbitrary", "arbitrary")),
    )(x, nw, w)
    return outs[0] if copy_tile is None else outs


def _out_proj_body(a1_ref, a2_ref, w1_ref, w2_ref, x_ref, nw_ref, y_ref, *, final_norm):
    y = x_ref[...] + _dot(a1_ref[...].astype(BF16), w1_ref[...]) + _dot(a2_ref[...].astype(BF16), w2_ref[...])
    if final_norm:
        ms = jnp.mean(y * y, axis=-1, keepdims=True)
        y = y * lax.rsqrt(ms + EPS) * nw_ref[...]
    y_ref[...] = y


def _out_proj(a1, c1, a2, c2, w, x, nw, tm, final_norm):
    M, D = x.shape
    KH = w.shape[0] // 2
    assert M % tm == 0
    return pl.pallas_call(
        functools.partial(_out_proj_body, final_norm=final_norm),
        grid=(M // tm,),
        in_specs=[pl.BlockSpec((tm, KH), lambda i: (i, c1)),
                  pl.BlockSpec((tm, KH), lambda i: (i, c2)),
                  pl.BlockSpec((KH, D), lambda i: (0, 0)),
                  pl.BlockSpec((KH, D), lambda i: (1, 0)),
                  pl.BlockSpec((tm, D), lambda i: (i, 0)),
                  pl.BlockSpec((1, D), lambda i: (0, 0))],
        out_specs=pl.BlockSpec((tm, D), lambda i: (i, 0)),
        out_shape=jax.ShapeDtypeStruct((M, D), F32),
        compiler_params=_params(("arbitrary",)),
    )(a1, a2, w, w, x, nw)


def _cmp_bias_body(pe_ref, w1_ref, b1_ref, o_ref):
    pe = jnp.broadcast_to(pe_ref[...], (SUBLANES, pe_ref.shape[-1])).astype(BF16)
    o_ref[...] = b1_ref[...] + _dot(pe, w1_ref[...])[0:1, :]


def _cmp_bias(pe, w1, b1):
    S, KF, HID = w1.shape
    return pl.pallas_call(
        _cmp_bias_body,
        grid=(S,),
        in_specs=[pl.BlockSpec((None, 1, KF), lambda s: (s, 0, 0)),
                  pl.BlockSpec((None, KF, HID), lambda s: (s, 0, 0)),
                  pl.BlockSpec((None, 1, HID), lambda s: (s, 0, 0))],
        out_specs=pl.BlockSpec((None, 1, HID), lambda s: (s, 0, 0)),
        out_shape=jax.ShapeDtypeStruct((S, 1, HID), F32),
        compiler_params=_params(("arbitrary",)),
    )(pe, w1, b1)


def _compress_hidden(u0, u1, bias):
    return u0 + pltpu.roll(u1, NCP - 1, 0) + bias


def _compress_prompt_body(x_ref, w1_ref, cb_ref, w2_ref, o_ref):
    u0 = jnp.zeros((NCP, CMP_HID), F32)
    u1 = jnp.zeros((NCP, CMP_HID), F32)
    for pp in range(D_CMP // 2):
        xa = x_ref[pl.ds(2 * pp, NCP, stride=D_CMP), :]
        xb = x_ref[pl.ds(2 * pp + 1, NCP, stride=D_CMP), :]
        xp = jnp.concatenate([xa, xb], axis=1).astype(BF16)
        u0 = u0 + _dot(xp, w1_ref[pl.ds(pp * 2 * DH, 2 * DH), :])
        u1 = u1 + _dot(xp, w1_ref[pl.ds(D_CMP * DH + pp * 2 * DH, 2 * DH), :])
    h = _compress_hidden(u0, u1, cb_ref[...])
    o_ref[...] = _dot(_gelu_tanh(h).astype(BF16), w2_ref[...])


def _compress_prompt(P, w1, cb, w2, B, T):
    assert T // D_CMP == NCP
    kv0 = A_KV // DH
    return pl.pallas_call(
        _compress_prompt_body,
        grid=(B, 2, HKV),
        in_specs=[pl.BlockSpec((T, DH), lambda b, s, g: (b, kv0 + HKV * s + g)),
                  pl.BlockSpec((None, L_CMP * DH, CMP_HID), lambda b, s, g: (s, 0, 0)),
                  pl.BlockSpec((None, 1, CMP_HID), lambda b, s, g: (s, 0, 0)),
                  pl.BlockSpec((None, CMP_HID, DH), lambda b, s, g: (s, 0, 0))],
        out_specs=pl.BlockSpec((None, None, None, NCP, DH), lambda b, s, g: (b, s, g, 0, 0)),
        out_shape=jax.ShapeDtypeStruct((B, 2, HKV, NCP, DH), F32),
        compiler_params=_params(("arbitrary", "arbitrary", "arbitrary")),
    )(P, w1, cb, w2)


def _select_blocks(imp, tpos_tok, ns):
    blk = lax.broadcasted_iota(jnp.int32, (1, NSP), 1)
    cur = tpos_tok // L_SEL
    valid = blk <= cur
    forced = (blk == 0) | (blk == cur) | (blk == cur - 1)
    score = jnp.where(valid, jnp.where(forced, FORCE_SCORE, imp), -jnp.inf)
    k_top = float(min(N_TOP, ns))
    if score.shape[0] % LANES == 0:
        nb = -(-ns // SUBLANES) * SUBLANES
        rblk = lax.broadcasted_iota(jnp.int32, (nb, 1), 0)
        tops = []
        for t in range(score.shape[0] // LANES):
            st = score[t * LANES:(t + 1) * LANES].T[0:nb]
            rank = jnp.zeros(st.shape, F32)
            for j in range(ns):
                sj = st[j:j + 1, :]
                rank = rank + jnp.where((sj > st) | ((sj == st) & (rblk > j)), 1.0, 0.0)
            top = jnp.where(rank < k_top, 1.0, 0.0)
            tops.append(jnp.concatenate([top, jnp.zeros((NSP - nb, LANES), F32)], axis=0).T)
        top = tops[0] if len(tops) == 1 else jnp.concatenate(tops, axis=0)
        return jnp.where(valid, top, 0.0)
    rank = jnp.zeros(score.shape, F32)
    for j in range(ns):
        sj = score[:, j:j + 1]
        beats = (sj > score) | ((sj == score) & (blk > j))
        rank = rank + jnp.where(beats, 1.0, 0.0)
    return jnp.where(valid & (rank < float(min(N_TOP, ns))), 1.0, 0.0)


def _compressed_probs(qb, kc, tpos, nc, scale):
    s = _dot_nt(qb, kc.astype(BF16)) * scale
    cidx = lax.broadcasted_iota(jnp.int32, (1, NCP), 1)
    cmask = (cidx < nc) & (cidx * D_CMP + (L_CMP - 1) <= tpos)
    return _masked_softmax_pre(jnp.where(cmask, s, NEG))


def _compressed_branch(qb, kc, vc, tpos, nc, scale):
    p = _compressed_probs(qb, kc, tpos, nc, scale)
    return p, _dot(p.astype(BF16), vc.astype(BF16))


def _keys_with_block_onehot(k_ref, koh_ref, k0, width):
    return jnp.concatenate([k_ref[pl.ds(k0, width), :].astype(BF16), koh_ref[pl.ds(k0, width), :]], axis=1)


def _nsa_prompt_body(q_ref, kc_ref, vc_ref, ks_ref, vs_ref, kw_ref, vw_ref, misc_ref, z_ref, c2s_ref, koh_ref,
                     o_ref, sc_ref, mx_ref, l_ref, acc_ref, *, tq, tk, T):
    qi = pl.program_id(1)
    t0 = pl.multiple_of(qi * tq, tq)
    R = GRP * tq
    scale = DH ** -0.5
    c1 = scale * LOG2E
    ns = -(-T // L_SEL)
    nc = T // D_CMP - CMP_R + 1
    n_top = min(N_TOP, ns)
    heads = range(HKV)

    def hd(g):
        return slice(g * DH, (g + 1) * DH)

    qbs = [jnp.concatenate([q_ref[:, (g * GRP + r) * DH:(g * GRP + r + 1) * DH] for r in range(GRP)],
                           axis=0).astype(BF16) for g in heads]
    tpos = t0 + (lax.broadcasted_iota(jnp.int32, (R, 1), 0) & (tq - 1))
    tpos_tok = t0 + lax.broadcasted_iota(jnp.int32, (tq, 1), 0)
    blk = lax.broadcasted_iota(jnp.int32, (1, NSP), 1)

    nwt = WINDOW // tq + 1
    tiles, starts = [[] for _ in heads], []
    for i in range(nwt):
        k0 = t0 - WINDOW + i * tq
        k0c = pl.multiple_of(jnp.maximum(k0, 0), tq)
        kpos = k0 + lax.broadcasted_iota(jnp.int32, (1, tq), 1)
        for g in heads:
            s = _dot_nt(qbs[g], kw_ref[pl.ds(k0c, tq), hd(g)].astype(BF16))
            if i == 0:
                s = jnp.where((kpos > tpos - WINDOW) & (kpos >= 0), s, NEG)
            elif i == nwt - 1:
                s = jnp.where(kpos <= tpos, s, NEG)
            else:
                s = jnp.where(k0 >= 0, s, NEG)
            tiles[g].append(s)
        starts.append(k0c)
    es_w = []
    for g in heads:
        s = jnp.concatenate(tiles[g], axis=1)
        es_w.append(jnp.exp2((s - jnp.max(s, axis=-1, keepdims=True)) * c1))
    o_ws = [None for _ in heads]
    for i in range(nwt):
        for g in heads:
            pv = _dot(es_w[g][:, i * tq:(i + 1) * tq].astype(BF16), vw_ref[pl.ds(starts[i], tq), hd(g)].astype(BF16))
            o_ws[g] = pv if i == 0 else o_ws[g] + pv
    o_ws = [o_ws[g] / jnp.sum(es_w[g], axis=-1, keepdims=True) for g in heads]

    comp = [_compressed_branch(qbs[g], kc_ref[g], vc_ref[g], tpos, nc, scale) for g in heads]
    imps = []
    for g in heads:
        p_c = comp[g][0]
        psum = p_c[0:tq]
        for r in range(1, GRP):
            psum = psum + p_c[r * tq:(r + 1) * tq]
        imps.append(_dot_exact_rhs(psum, c2s_ref[...]))

    def all_valid():
        v = jnp.where(blk <= tpos_tok // L_SEL, 1.0, 0.0)
        return tuple(v for _ in heads)

    sels = lax.cond(t0 + tq <= n_top * L_SEL, all_valid,
                    lambda: tuple(_select_blocks(imps[g], tpos_tok, ns) for g in heads))

    q_augs, q_earlys = [], []
    for g in heads:
        sel_bias = (sels[g] - 1.0) * (-NEG)
        early_bias = jnp.where(blk < lax.div(t0, L_SEL), sel_bias, NEG)
        q_augs.append(jnp.concatenate([qbs[g], jnp.concatenate([sel_bias.astype(BF16)] * GRP, axis=0)], axis=1))
        q_earlys.append(jnp.concatenate([qbs[g], jnp.concatenate([early_bias.astype(BF16)] * GRP, axis=0)], axis=1))

    def keys_aug(k0, width, g):
        return jnp.concatenate([ks_ref[pl.ds(k0, width), hd(g)].astype(BF16), koh_ref[pl.ds(k0, width), :]], axis=1)

    kpos_d = t0 + lax.broadcasted_iota(jnp.int32, (1, tq), 1)
    s_diags = [jnp.where(kpos_d <= tpos, _dot_nt(q_augs[g], keys_aug(t0, tq, g)), NEG) for g in heads]
    n_early = lax.div(t0 + tk - 1, tk)
    for g in heads:
        mx_ref[g] = functools.reduce(jnp.maximum, [s_diags[g][:, c * LANES:(c + 1) * LANES]
                                                   for c in range(tq // LANES)])

    def pass1(kt, carry):
        k0 = pl.multiple_of(kt * tk, tk)
        ss = [_dot_nt(q_earlys[g], keys_aug(k0, tk, g)) for g in heads]
        old = [mx_ref[g] for g in heads]
        for g in heads:
            mx = ss[g][:, 0:LANES]
            for c in range(1, tk // LANES):
                mx = jnp.maximum(mx, ss[g][:, c * LANES:(c + 1) * LANES])
            sc_ref[g, kt] = ss[g]
            mx_ref[g] = jnp.maximum(old[g], mx)
        return carry

    lax.fori_loop(0, n_early, pass1, 0)
    m2s = [jnp.broadcast_to(jnp.max(mx_ref[g], axis=-1, keepdims=True) * c1, (R, LANES)) for g in heads]
    e_ds = [jnp.concatenate([jnp.exp2(s_diags[g][:, c * LANES:(c + 1) * LANES] * c1 - m2s[g])
                             for c in range(tq // LANES)], axis=1) for g in heads]
    pv_ds = [_dot(e_ds[g].astype(BF16), vs_ref[pl.ds(t0, tq), hd(g)].astype(BF16)) for g in heads]
    for g in heads:
        mx_ref[g] = m2s[g]
        l_ref[g] = sum(e_ds[g][:, c * LANES:(c + 1) * LANES] for c in range(tq // LANES))
        acc_ref[g] = pv_ds[g]

    def pass2(kt, carry):
        k0 = pl.multiple_of(kt * tk, tk)
        ss = [sc_ref[g, kt] for g in heads]
        m2 = [mx_ref[g] for g in heads]
        l_old = [l_ref[g] for g in heads]
        acc_old = [acc_ref[g] for g in heads]
        ebs, lsums = [], []
        for g in heads:
            es = [jnp.exp2(ss[g][:, c * LANES:(c + 1) * LANES] * c1 - m2[g]) for c in range(tk // LANES)]
            lsum = es[0]
            for ec in es[1:]:
                lsum = lsum + ec
            lsums.append(lsum)
            ebs.append(jnp.concatenate(es, axis=1).astype(BF16))
        pvs = [_dot(ebs[g], vs_ref[pl.ds(k0, tk), hd(g)].astype(BF16)) for g in heads]
        for g in heads:
            l_ref[g] = l_old[g] + lsums[g]
            acc_ref[g] = acc_old[g] + pvs[g]
        return carry

    lax.fori_loop(0, n_early, pass2, 0)

    outs = []
    for g in heads:
        o_s = acc_ref[g] / jnp.sum(l_ref[g], axis=-1, keepdims=True)
        gm = _sigmoid(misc_ref[:, g * LANES:g * LANES + GRP * 3])

        def gate(j):
            return jnp.concatenate([gm[:, r * 3 + j:r * 3 + j + 1] for r in range(GRP)], axis=0)

        outs.append(gate(0) * comp[g][1] + gate(1) * o_s + gate(2) * o_ws[g])
    for g in heads:
        for r in range(GRP):
            c = (g * GRP + r) * DH
            o_ref[:, c:c + DH] = (outs[g][r * tq:(r + 1) * tq] * _silu(z_ref[:, c:c + DH])).astype(BF16)


def _nsa_prompt(P, kcv, c2s, koh, B, T, tq, tk):
    assert T % tq == 0 and tq & (tq - 1) == 0 and tk % tq == 0 and T % tk == 0 and WINDOW % tq == 0
    nq = T // tq
    qw = H_A * DH
    kw = HKV * DH
    R = GRP * tq

    def kvspec(slot):
        return pl.BlockSpec((T, kw), lambda b, i: (b, A_KV // kw + slot))

    return pl.pallas_call(
        functools.partial(_nsa_prompt_body, tq=tq, tk=tk, T=T),
        grid=(B, nq),
        in_specs=[pl.BlockSpec((tq, qw), lambda b, i: (b * nq + i, A_Q // qw)),
                  pl.BlockSpec((None, None, HKV, NCP, DH), lambda b, i: (b, 0, 0, 0, 0)),
                  pl.BlockSpec((None, None, HKV, NCP, DH), lambda b, i: (b, 1, 0, 0, 0)),
                  kvspec(2), kvspec(3), kvspec(4), kvspec(5),
                  pl.BlockSpec((tq, HKV * LANES), lambda b, i: (b * nq + i, A_MISC // (HKV * LANES))),
                  pl.BlockSpec((tq, qw), lambda b, i: (b * nq + i, A_ZA // qw)),
                  pl.BlockSpec((NCP, NSP), lambda b, i: (0, 0)),
                  pl.BlockSpec((T, NSP), lambda b, i: (0, 0))],
        out_specs=pl.BlockSpec((tq, qw), lambda b, i: (b * nq + i, 0)),
        out_shape=jax.ShapeDtypeStruct((B * T, qw), BF16),
        scratch_shapes=[pltpu.VMEM((HKV, T // tk, R, tk), F32), pltpu.VMEM((HKV, R, LANES), F32),
                        pltpu.VMEM((HKV, R, LANES), F32), pltpu.VMEM((HKV, R, DH), F32)],
        compiler_params=_params(("arbitrary", "arbitrary")),
    )(P, kcv, kcv, P, P, P, P, P, P, c2s, koh)


def _pad_rows(x, rows):
    return jnp.concatenate([x, jnp.zeros((rows - x.shape[0], x.shape[1]), x.dtype)], axis=0)


def _nsa_sample_body(pt_ref, cache_ref, win_ref, q_ref, kvr_ref, kvw_ref, misc_ref, z_ref, w1_ref, cb_ref, w2_ref,
                     c2s_ref, koh_ref, o_ref, wout_ref, pbuf, sem, *, n_pages, page, past_len, dec_seq, wb):
    b = pl.program_id(0)
    scale = DH ** -0.5
    c1 = scale * LOG2E
    L = past_len + dec_seq
    ns = -(-L // L_SEL)
    nc = L // D_CMP - CMP_R + 1
    R = GRP * TP
    per_page = page // D_CMP
    nch = N_KV_SLOTS * HKV
    grp_rows = D_CMP * nch
    pitch = grp_rows + 1

    def page_copies(seq, slot):
        cps = []
        for j in range(n_pages):
            pid = pt_ref[seq * n_pages + j]
            for n in range(per_page):
                cps.append(pltpu.make_async_copy(
                    cache_ref.at[pid, pl.ds(n * grp_rows, grp_rows), :],
                    pbuf.at[slot, pl.ds((j * per_page + n) * pitch, grp_rows), :], sem.at[slot]))
        return cps

    slot = lax.rem(b, 2)

    @pl.when(b == 0)
    def _():
        for i, cp in enumerate(page_copies(0, 0)):
            cp.start(priority=i % 2)

    for cp in page_copies(b, slot):
        cp.wait()

    last = pl.num_programs(0) - 1
    next_copies = page_copies(jnp.minimum(b + 1, last), 1 - slot)
    copies_per_group = len(next_copies) // (2 * (D_CMP // 2))

    def gather(p, ch):
        return pbuf[slot, pl.ds(p * nch + ch, NCP, stride=pitch), :]

    trow = lax.broadcasted_iota(jnp.int32, (R, 1), 0) & (TP - 1)
    tpos = past_len + trow
    tpos_tok = past_len + lax.broadcasted_iota(jnp.int32, (TP, 1), 0)
    lane = lax.broadcasted_iota(jnp.int32, (1, LANES), 1)
    new_pos = past_len + lane
    new_ok = (new_pos <= tpos) & (new_pos < L)
    new_blk = (past_len + lax.broadcasted_iota(jnp.int32, (LANES, 1), 0)) // L_SEL
    new_oh = jnp.where(lax.broadcasted_iota(jnp.int32, (LANES, NSP), 1) == new_blk, 1.0, 0.0).astype(BF16)
    wpos = (past_len - wb) + lax.broadcasted_iota(jnp.int32, (1, wb), 1)
    wch = 2 * HKV

    heads = range(HKV)
    n_iter = D_CMP // 2
    qbs = [jnp.concatenate([q_ref[:, (g * GRP + r) * DH:(g * GRP + r + 1) * DH] for r in range(GRP)],
                           axis=0).astype(BF16) for g in heads]
    raw = [[] for _ in heads]
    win = {}

    def stage_window_scores():
        es = []
        for g in heads:
            s_w = _dot_nt(qbs[g], win_ref[pl.ds(g, wb, stride=wch), :].astype(BF16))
            s_w = jnp.where((wpos <= tpos) & (wpos > tpos - WINDOW), s_w, NEG)
            s_n = _dot_nt(qbs[g], _pad_rows(kvw_ref[:, g * DH:(g + 1) * DH], LANES).astype(BF16))
            s_n = jnp.where(new_ok & (new_pos > tpos - WINDOW), s_n, NEG)
            s = jnp.concatenate([s_w, s_n], axis=1)
            es.append(jnp.exp2((s - jnp.max(s, axis=-1, keepdims=True)) * c1))
        win["e"] = es

    def stage_window_out():
        outs_w = []
        for g in heads:
            e = win["e"][g]
            o_w = (_dot(e[:, :wb].astype(BF16), win_ref[pl.ds(HKV + g, wb, stride=wch), :].astype(BF16))
                   + _dot(e[:, wb:].astype(BF16),
                          _pad_rows(kvw_ref[:, (HKV + g) * DH:(HKV + g + 1) * DH], LANES).astype(BF16)))
            outs_w.append(o_w / jnp.sum(e, axis=-1, keepdims=True))
        win["o"] = outs_w

    def stage_raw_scores(p_lo, p_hi):
        for p in range(p_lo, p_hi):
            for g in heads:
                raw[g].append(_dot_nt(qbs[g], gather(p, 2 * HKV + g).astype(BF16)))

    def stage_raw_new():
        for g in heads:
            ksel_col = (2 * HKV + g) * DH
            raw[g].append(_dot_nt(qbs[g], _pad_rows(kvr_ref[:, ksel_col:ksel_col + DH], LANES).astype(BF16)))


    def compress_step(s, pp, u0, u1):
        xs = [jnp.concatenate([gather(2 * pp, s * HKV + g), gather(2 * pp + 1, s * HKV + g)], axis=1) for g in heads]
        xp = jnp.concatenate(xs, axis=0).astype(BF16)
        u0 = u0 + _dot(xp, w1_ref[s, pl.ds(pp * 2 * DH, 2 * DH), :])
        u1 = u1 + _dot(xp, w1_ref[s, pl.ds(D_CMP * DH + pp * 2 * DH, 2 * DH), :])
        grp = s * n_iter + pp
        for i, cp in enumerate(next_copies[grp * copies_per_group:(grp + 1) * copies_per_group]):
            cp.start(priority=i % 2)
        return u0, u1

    def compress_out(s, u0, u1):
        return [_dot(_gelu_tanh(_compress_hidden(u0[g * NCP:(g + 1) * NCP], u1[g * NCP:(g + 1) * NCP],
                                                 cb_ref[s])).astype(BF16), w2_ref[s]) for g in heads]

    quarter = D_CMP // 4
    early = {0: stage_window_scores, 1: lambda: stage_raw_scores(0, quarter),
             2: lambda: stage_raw_scores(quarter, 2 * quarter), 3: stage_window_out,
             4: lambda: stage_raw_scores(2 * quarter, 3 * quarter),
             5: lambda: stage_raw_scores(3 * quarter, D_CMP), 6: stage_raw_new}
    u0 = u1 = jnp.zeros((HKV * NCP, CMP_HID), F32)
    for pp in range(n_iter):
        u0, u1 = compress_step(0, pp, u0, u1)
        if pp in early:
            early[pp]()
    kc = compress_out(0, u0, u1)

    st = {}

    def stage_probs():
        st["p_c"] = [_compressed_probs(qbs[g], kc[g], tpos, nc, scale) for g in heads]

    def stage_select():
        imps = []
        for g in heads:
            p_c = st["p_c"][g]
            psum = p_c[0:TP]
            for r in range(1, GRP):
                psum = psum + p_c[r * TP:(r + 1) * TP]
            imps.append(_dot_exact_rhs(psum, c2s_ref[...]))
        st["sel"] = [_select_blocks(imps[g], tpos_tok, ns) for g in heads]

    def stage_scores():
        biases = [jnp.concatenate([((st["sel"][g] - 1.0) * (-NEG)).astype(BF16)] * GRP, axis=0) for g in heads]
        bias_past = [_dot_nt(biases[g], koh_ref[...]) for g in heads]
        bias_new = [jnp.where(new_ok, _dot_nt(biases[g], new_oh), NEG) for g in heads]
        es = []
        for g in heads:
            s = jnp.concatenate([pc + bias_past[g] for pc in raw[g][:D_CMP]] + [raw[g][D_CMP] + bias_new[g]], axis=1)
            es.append(jnp.exp2((s - jnp.max(s, axis=-1, keepdims=True)) * c1))
        st["e"] = es
        st["o_s"] = [_dot(es[g][:, D_CMP * NCP:].astype(BF16),
                          _pad_rows(kvr_ref[:, (3 * HKV + g) * DH:(3 * HKV + g + 1) * DH], LANES).astype(BF16))
                     for g in heads]

    def stage_pv(p_lo, p_hi):
        for p in range(p_lo, p_hi):
            for g in heads:
                st["o_s"][g] = st["o_s"][g] + _dot(st["e"][g][:, p * NCP:(p + 1) * NCP].astype(BF16),
                                                  gather(p, 3 * HKV + g).astype(BF16))

    after = {0: stage_probs, 3: stage_select, 6: stage_scores, 7: lambda: stage_pv(0, D_CMP // 2)}
    u0 = u1 = jnp.zeros((HKV * NCP, CMP_HID), F32)
    for pp in range(n_iter):
        u0, u1 = compress_step(1, pp, u0, u1)
        if pp in after:
            after[pp]()
    vc = compress_out(1, u0, u1)
    stage_pv(D_CMP // 2, D_CMP)

    outs = []
    for g in heads:
        o_c = _dot(st["p_c"][g].astype(BF16), vc[g].astype(BF16))
        o_s = st["o_s"][g] / jnp.sum(st["e"][g], axis=-1, keepdims=True)
        gm = _sigmoid(misc_ref[:, g * LANES:g * LANES + GRP * 3])

        def gate(j):
            return jnp.concatenate([gm[:, r * 3 + j:r * 3 + j + 1] for r in range(GRP)], axis=0)

        outs.append(gate(0) * o_c + gate(1) * o_s + gate(2) * win["o"][g])

    for g in range(HKV):
        for r in range(GRP):
            c = (g * GRP + r) * DH
            o_ref[:, c:c + DH] = outs[g][r * TP:(r + 1) * TP] * _silu(z_ref[:, c:c + DH])

    wout_ref[pl.ds(0, (wb - dec_seq) * wch), :] = win_ref[pl.ds(dec_seq * wch, (wb - dec_seq) * wch), :]
    for t in range(dec_seq):
        for ch in range(wch):
            wout_ref[pl.ds((wb - dec_seq + t) * wch + ch, 1), :] = kvw_ref[t:t + 1, ch * DH:(ch + 1) * DH]

    @pl.when(b == last)
    def _():
        for cp in next_copies:
            cp.wait()


def _nsa_sample(Ps, cache, cache_win, page_table, w1, cb, w2, c2s, dec_seq):
    NB, n_pages = page_table.shape
    n_pool, prow, _ = cache.shape
    page = prow // (N_KV_SLOTS * HKV)
    wb = cache_win.shape[1] // (2 * HKV)
    past_len = n_pages * page
    L = past_len + dec_seq
    assert wb == WINDOW and (L // D_CMP) == NCP and NCP * D_CMP == past_len and dec_seq <= TP
    assert -(-L // L_SEL) < NSP and page % D_CMP == 0 and (dec_seq * 2 * HKV) % SUBLANES == 0
    qw = H_A * DH
    nch = N_KV_SLOTS * HKV
    koh = jnp.asarray(np.arange(NCP)[:, None] * D_CMP // L_SEL == np.arange(NSP)[None, :], dtype=BF16)

    grid_spec = pltpu.PrefetchScalarGridSpec(
        num_scalar_prefetch=1,
        grid=(NB,),
        in_specs=[
            pl.BlockSpec(memory_space=pl.ANY),
            pl.BlockSpec((None, wb * 2 * HKV, DH), lambda b, pt: (b, 0, 0)),
            pl.BlockSpec((TP, qw), lambda b, pt: (b, A_Q // qw)),
            pl.BlockSpec((TP, 4 * HKV * DH), lambda b, pt: (b, A_KV // (4 * HKV * DH))),
            pl.BlockSpec((TP, 2 * HKV * DH), lambda b, pt: (b, (A_KV + 4 * HKV * DH) // (2 * HKV * DH))),
            pl.BlockSpec((TP, HKV * LANES), lambda b, pt: (b, A_MISC // (HKV * LANES))),
            pl.BlockSpec((TP, qw), lambda b, pt: (b, A_ZA // qw)),
            pl.BlockSpec((2, L_CMP * DH, CMP_HID), lambda b, pt: (0, 0, 0)),
            pl.BlockSpec((2, 1, CMP_HID), lambda b, pt: (0, 0, 0)),
            pl.BlockSpec((2, CMP_HID, DH), lambda b, pt: (0, 0, 0)),
            pl.BlockSpec((NCP, NSP), lambda b, pt: (0, 0)),
            pl.BlockSpec((NCP, NSP), lambda b, pt: (0, 0))],
        out_specs=[pl.BlockSpec((TP, qw), lambda b, pt: (b, 0)),
                   pl.BlockSpec((None, wb * 2 * HKV, DH), lambda b, pt: (b, 0, 0))],
        scratch_shapes=[pltpu.VMEM((2, NCP * (D_CMP * nch + 1), DH), F32), pltpu.SemaphoreType.DMA((2,))],
    )
    return pl.pallas_call(
        functools.partial(_nsa_sample_body, n_pages=n_pages, page=page, past_len=past_len, dec_seq=dec_seq, wb=wb),
        grid_spec=grid_spec,
        out_shape=[jax.ShapeDtypeStruct((NB * TP, qw), F32),
                   jax.ShapeDtypeStruct((NB, wb * 2 * HKV, DH), F32)],
        compiler_params=_params(("arbitrary",)),
    )(page_table.reshape(-1), cache, cache_win, Ps, Ps, Ps, Ps, Ps, w1, cb, w2, c2s, koh)


def _rec_chunk(q, k, v, la, gate, S, C, last_row, nh, K, V):
    sb = min(16, C)
    ri = lax.broadcasted_iota(jnp.int32, (C, C), 0)
    cj = lax.broadcasted_iota(jnp.int32, (C, C), 1)
    causal = cj <= ri
    tri = jnp.where(causal, 1.0, 0.0).astype(BF16)
    b = _dot_exact_lhs(tri, la)
    qe = (q * jnp.exp(b)).astype(BF16)
    qis, kis = [], []
    k_ref_prev, c_prev = None, None
    for i in range(C // sb):
        r0 = i * sb
        ci = b[r0 + sb // 2:r0 + sb // 2 + 1, :]
        qis.append((q[r0:r0 + sb] * jnp.exp(b[r0:r0 + sb] - ci)).astype(BF16))
        k_new = k[r0:r0 + sb] * jnp.exp(jnp.minimum(ci - b[r0:r0 + sb], 80.0))
        k_scaled = k_new if i == 0 else jnp.concatenate([k_ref_prev * jnp.exp(ci - c_prev), k_new], axis=0)
        k_ref_prev, c_prev = k_scaled, ci
        rest = C - r0 - sb
        kis.append((k_scaled if rest == 0 else
                    jnp.concatenate([k_scaled, jnp.zeros((rest, k.shape[1]), F32)], axis=0)).astype(BF16))
    b_last = b[last_row:last_row + 1, :]
    rowi = lax.broadcasted_iota(jnp.int32, (C, 1), 0)
    kd = jnp.where(rowi <= last_row, k * jnp.exp(jnp.minimum(b_last - b, 0.0)), 0.0)
    stack = jnp.concatenate([kd, jnp.broadcast_to(b_last, (SUBLANES, nh * K)),
                             jnp.zeros((LANES - C - SUBLANES, nh * K), F32)], axis=0)
    vb = v.astype(BF16)
    v_pad = jnp.concatenate([vb, jnp.zeros((LANES - C, nh * V), BF16)], axis=0)
    outs, s_new = [], []
    for h in range(nh):
        ks = slice(h * K, (h + 1) * K)
        vs = slice(h * V, (h + 1) * V)
        o = _dot(qe[:, ks], S[h].astype(BF16))
        rows = [_dot_nt(qi[:, ks], ki[:, ks]) for qi, ki in zip(qis, kis)]
        att = rows[0] if len(rows) == 1 else jnp.concatenate(rows, axis=0)
        att = jnp.where(causal, att, 0.0)
        o = o + _dot(att.astype(BF16), vb[:, vs])
        stack_t = stack[:, ks].T
        a_col = jnp.exp(stack_t[:, C:C + 1])
        s_new.append(a_col * S[h] + _dot(stack_t.astype(BF16), v_pad[:, vs]))
        ms = jnp.mean(o * o, axis=-1, keepdims=True)
        outs.append(o * lax.rsqrt(ms + EPS) * gate[:, vs])
    return outs, s_new


def _rec_body(*refs, variant, nh, K, V, C, TB, SB, last_row, has_s0, layer_idx):
    refs = list(refs)
    q_ref, k_ref, v_ref, z_ref = refs[:4]
    pos = 4
    if variant == "gla":
        misc_ref, w2_ref, gb_ref = refs[pos:pos + 3]
        pos += 3
    else:
        lb_ref = refs[pos]
        pos += 1
    gn_ref = refs[pos]
    pos += 1
    if has_s0:
        s0_ref = refs[pos]
        pos += 1
    o_ref, s_ref = refs[pos:pos + 2]
    n_chunks = TB // C
    single = n_chunks == 1 and has_s0

    if not single:
        @pl.when(pl.program_id(2) == 0)
        def _():
            if has_s0:
                s_ref[...] = s0_ref[...]
            else:
                s_ref[...] = jnp.zeros(s_ref.shape, F32)

    if variant == "hgrn":
        lg = lb_ref[...]
        e = jnp.exp(lg - jnp.max(lg, axis=0, keepdims=True))
        prob = e / jnp.sum(e, axis=0, keepdims=True)
        lb = prob[1:2]
        for i in range(2, layer_idx + 1):
            lb = lb + prob[i:i + 1]
        if SB > 1:
            lb = jnp.concatenate([lb] * SB, axis=1)
    gn_all = jnp.concatenate([gn_ref[...]] * (nh * SB), axis=1)

    def chunk(c0):
        def rows(s):
            return pl.ds(s * TB + c0, C)

        def cat(f):
            return f(0) if SB == 1 else jnp.concatenate([f(s) for s in range(SB)], axis=1)

        qr = cat(lambda s: q_ref[rows(s), :])
        kr = cat(lambda s: k_ref[rows(s), :])
        v = cat(lambda s: v_ref[rows(s), :])
        gate = _silu(cat(lambda s: z_ref[rows(s), :])) * gn_all
        if variant == "gla":
            w2 = w2_ref[...].astype(BF16)
            zg = cat(lambda s: _dot(misc_ref[rows(s), :].astype(BF16), w2) + gb_ref[...])
            q, k, la = qr * (K ** -0.5), kr, _log_sigmoid(zg) / GLA_GATE_NORM
        else:
            t = jnp.exp(-jnp.abs(kr))
            r = 1.0 / (1.0 + t)
            tr = t * r
            nonneg = kr >= 0.0
            sig = jnp.where(nonneg, r, tr)
            nsig = jnp.where(nonneg, tr, r)
            q, k, la = _silu(qr), (1.0 - lb) * nsig, jnp.log(lb + (1.0 - lb) * sig)
        src = s0_ref if single else s_ref
        states = [src[s, h] for s in range(SB) for h in range(nh)]
        outs, s_new = _rec_chunk(q, k, v, la, gate, states, C, last_row, SB * nh, K, V)
        for s in range(SB):
            for h in range(nh):
                o_ref[rows(s), h * V:(h + 1) * V] = outs[s * nh + h].astype(o_ref.dtype)
                s_ref[s, h] = s_new[s * nh + h]

    if n_chunks == 1:
        chunk(0)
    else:
        def body(ci, carry):
            chunk(pl.multiple_of(ci * C, C))
            return carry

        lax.fori_loop(0, n_chunks, body, 0, unroll=8)


def _recurrence(P, cols, extra, gn, s0, *, variant, B, T, H, K, V, C, TB, nh, last_row, out_dtype, SB=1,
                layer_idx=1):
    cq, ck, cv, cz = cols
    assert H % nh == 0 and T % TB == 0 and TB % C == 0 and B % SB == 0
    nt = T // TB
    assert SB == 1 or nt == 1

    def colspec(c0, w):
        assert c0 % (nh * w) == 0
        return pl.BlockSpec((SB * TB, nh * w), lambda b, j, t: (b * nt + t, c0 // (nh * w) + j))

    in_specs = [colspec(cq, K), colspec(ck, K), colspec(cv, V), colspec(cz, V)]
    args = [P, P, P, P]
    if variant == "gla":
        w2p, gb = extra
        in_specs += [pl.BlockSpec((SB * TB, LANES), lambda b, j, t: (b * nt + t, A_MISC // LANES)),
                     pl.BlockSpec((LANES, nh * K), lambda b, j, t: (0, j)),
                     pl.BlockSpec((1, nh * K), lambda b, j, t: (0, j))]
        args += [P, w2p, gb]
    else:
        (lb_logits,) = extra
        in_specs += [pl.BlockSpec((lb_logits.shape[0], nh * K), lambda b, j, t: (0, j))]
        args += [lb_logits]
    in_specs += [pl.BlockSpec((1, V), lambda b, j, t: (0, 0))]
    args += [gn]
    st_spec = pl.BlockSpec((None, SB, nh, K, V), lambda b, j, t: (0, b, j, 0, 0))
    if s0 is not None:
        in_specs += [st_spec]
        args += [s0]
    return pl.pallas_call(
        functools.partial(_rec_body, variant=variant, nh=nh, K=K, V=V, C=C, TB=TB, SB=SB, last_row=last_row,
                          has_s0=s0 is not None, layer_idx=layer_idx),
        grid=(B // SB, H // nh, nt),
        in_specs=in_specs,
        out_specs=[pl.BlockSpec((SB * TB, nh * V), lambda b, j, t: (b * nt + t, j)), st_spec],
        out_shape=[jax.ShapeDtypeStruct((B * T, H * V), out_dtype),
                   jax.ShapeDtypeStruct((1, B, H, K, V), F32)],
        compiler_params=_params(("arbitrary", "arbitrary", "arbitrary")),
    )(*args)


def _layout_a_w_in_body(w_ref, o_ref):
    off = [int(v) for v in np.concatenate([[0], np.cumsum(A_SIZES)])]
    q, kv, gbr, za, qb, kb, vb, lr, zb = [w_ref[off[i]:off[i + 1], :] for i in range(len(A_SIZES))]
    cols = w_ref.shape[1]
    z = lambda n: jnp.zeros((n, cols), F32)
    misc0 = jnp.concatenate([gbr[:GRP * 3], lr, z(LANES - GRP * 3 - GLA_LR)], axis=0)
    misc1 = jnp.concatenate([gbr[GRP * 3:], z(LANES - GRP * 3)], axis=0)
    out = jnp.concatenate([q, vb, zb, za, qb, kb, kv, misc0, misc1, z(NA - A_MISC - HKV * LANES)], axis=0)
    o_ref[...] = out.astype(BF16)


def _layout_a_w_in(wt, tc=256):
    n, D = wt.shape
    assert D % tc == 0 and n == sum(A_SIZES)
    return pl.pallas_call(
        _layout_a_w_in_body,
        grid=(D // tc,),
        in_specs=[pl.BlockSpec((n, tc), lambda i: (0, i))],
        out_specs=pl.BlockSpec((NA, tc), lambda i: (0, i)),
        out_shape=jax.ShapeDtypeStruct((NA, D), BF16),
        compiler_params=_params(("arbitrary",)),
    )(wt)


def _cmp_to_sel():
    c_start = np.arange(NCP)[:, None] * D_CMP
    s_start = np.arange(NSP)[None, :] * L_SEL
    overlap = np.clip(np.minimum(c_start + L_CMP, s_start + L_SEL) - np.maximum(c_start, s_start), 0, None)
    return jnp.asarray(overlap / D_CMP, dtype=BF16)


class _Tiles(NamedTuple):
    proj_rows: int
    proj_cols: int
    out_rows: int
    nsa_q: int
    nsa_k: int
    rec_chunk: int
    rec_block: int
    gla_heads: int
    hgrn_heads: int
    gla_seqs: int
    hgrn_seqs: int


def _tiles(n_decode):
    return _Tiles(proj_rows=1024, proj_cols=N_KV_SLOTS * HKV * DH, out_rows=512, nsa_q=256, nsa_k=512,
                  rec_chunk=64, rec_block=512, gla_heads=H_B, hgrn_heads=8,
                  gla_seqs=math.gcd(n_decode, 8), hgrn_seqs=math.gcd(n_decode, 4))


def kernel(x_prompt, x_sample, cache_kv, cache_win, state_gla, state_hgrn, page_table, a_norm, a_w_in, a_gla_w2,
           a_gla_b, a_gla_gn, a_cmp_pe, a_cmp_w1, a_cmp_b1, a_cmp_w2, a_w_out, c_norm, c_w_in, c_lb_logits, c_gn,
           c_w_out, final_norm):
    B, T, D = x_prompt.shape
    NB, TS, _ = x_sample.shape
    n_pool, page = cache_kv.shape[1], cache_kv.shape[2]
    wb = cache_win.shape[2]
    assert a_norm.shape[0] == 1 and c_norm.shape[0] == 1 and c_lb_logits.shape[0] == 2

    wa = _layout_a_w_in(a_w_in[0].T)
    wc = c_w_in[0].astype(BF16)
    wao = a_w_out[0].astype(BF16)
    wco = c_w_out[0].astype(BF16)
    w1 = a_cmp_w1[0].astype(BF16)
    w2 = a_cmp_w2[0].astype(BF16)
    pe = a_cmp_pe[0].reshape(2, 1, L_CMP * DH)
    b1 = a_cmp_b1[0].reshape(2, 1, CMP_HID)
    w2p = jnp.zeros((LANES, H_B * DK_B), F32).at[MISC_LR:MISC_LR + GLA_LR, :].set(a_gla_w2[0])
    gb = a_gla_b[0].reshape(1, H_B * DK_B)
    c2s = _cmp_to_sel()
    a_nw = a_norm[0].reshape(1, D)
    c_nw = c_norm[0].reshape(1, D)
    f_nw = final_norm.reshape(1, D)
    gla_gn = a_gla_gn[0].reshape(1, DV_B)
    hg_gn = c_gn[0].reshape(1, DV_C)

    xp = x_prompt.reshape(B * T, D)
    xs = jnp.pad(x_sample, ((0, 0), (0, TP - TS), (0, 0))).reshape(NB * TP, D)

    cb = _cmp_bias(pe, w1, b1)

    tl = _tiles(NB)

    Pp, kv_rows_p = _norm_matmul(xp, a_nw, wa, tm=tl.proj_rows, tn=tl.proj_cols, copy_tile=A_KV // tl.proj_cols,
                                 w_rows=True)
    kcv = _compress_prompt(Pp, w1, cb, w2, B, T)
    koh = jnp.asarray(np.arange(T)[:, None] // L_SEL == np.arange(NSP)[None, :], dtype=BF16)
    oa_p = _nsa_prompt(Pp, kcv, c2s, koh, B, T, tq=tl.nsa_q, tk=tl.nsa_k)
    ob_p, gla_p = _recurrence(Pp, (A_QB, A_KB, A_VB, A_ZB), (w2p, gb), gla_gn, None, variant="gla", B=B, T=T,
                              H=H_B, K=DK_B, V=DV_B, C=tl.rec_chunk, TB=tl.rec_block, nh=tl.gla_heads,
                              last_row=tl.rec_chunk - 1, out_dtype=BF16)
    y1p = _out_proj(oa_p, 0, ob_p, 0, wao, xp, a_nw, tm=tl.out_rows, final_norm=False)

    Ps = _norm_matmul(xs, a_nw, wa, tm=NB * TP, tn=tl.proj_cols, w_rows=True)
    cache2 = cache_kv[0].reshape(n_pool, page * N_KV_SLOTS * HKV, DH)
    win2 = cache_win[0].reshape(NB, wb * 2 * HKV, DH)
    oa_s, win_out = _nsa_sample(Ps, cache2, win2, page_table, w1, cb, w2, c2s, TS)
    ob_s, gla_s = _recurrence(Ps, (A_QB, A_KB, A_VB, A_ZB), (w2p, gb), gla_gn, state_gla, variant="gla", B=NB, T=TP,
                              H=H_B, K=DK_B, V=DV_B, C=TP, TB=TP, nh=H_B, last_row=TS - 1, out_dtype=F32,
                              SB=tl.gla_seqs)
    y1s = _out_proj(oa_s, 0, ob_s, 0, wao, xs, a_nw, tm=min(tl.out_rows, NB * TP), final_norm=False)

    Pc = _norm_matmul(y1p, c_nw, wc, tm=tl.proj_rows, tn=tl.proj_cols)
    hc = H_C * DK_C
    oc_p, hg_p = _recurrence(Pc, (0, hc, 2 * hc, 3 * hc), (c_lb_logits,), hg_gn, None, variant="hgrn", B=B, T=T,
                             H=H_C, K=DK_C, V=DV_C, C=tl.rec_chunk, TB=tl.rec_block, nh=tl.hgrn_heads,
                             last_row=tl.rec_chunk - 1, out_dtype=BF16)
    y_prompt = _out_proj(oc_p, 0, oc_p, 1, wco, y1p, f_nw, tm=tl.out_rows, final_norm=True)

    Pcs = _norm_matmul(y1s, c_nw, wc, tm=NB * TP, tn=tl.proj_cols)
    oc_s, hg_s = _recurrence(Pcs, (0, hc, 2 * hc, 3 * hc), (c_lb_logits,), hg_gn, state_hgrn, variant="hgrn", B=NB,
                             T=TP, H=H_C, K=DK_C, V=DV_C, C=TP, TB=TP, nh=H_C, last_row=TS - 1, out_dtype=F32,
                             SB=tl.hgrn_seqs)
    y_sample = _out_proj(oc_s, 0, oc_s, 1, wco, y1s, f_nw, tm=min(tl.out_rows, NB * TP), final_norm=True)

    kvw = 6 * HKV * DH
    nrw = N_KV_SLOTS * HKV * DH
    Pp3 = Pp.reshape(B, T, NA)
    Ps3 = Ps.reshape(NB, TP, NA)
    kv_p = kv_rows_p.reshape(1, B, T, N_KV_SLOTS, HKV, DH)
    kv_s = Ps3[:, :TS, A_KV:A_KV + nrw].reshape(1, NB, TS, N_KV_SLOTS, HKV, DH)
    keep = min(WINDOW, T)
    win_p = Pp3[:, T - keep:, A_KV + nrw:A_KV + kvw].reshape(1, B, keep, 2, HKV, DH)
    win_s = win_out.reshape(1, NB, wb, 2, HKV, DH)
    return (y_prompt.reshape(B, T, D), y_sample.reshape(NB, TP, D)[:, :TS], kv_p, kv_s, win_p, win_s,
            gla_p, gla_s, hg_p, hg_s)
```

```python
import functools
import math
from typing import NamedTuple

import numpy as np
import jax
import jax.numpy as jnp
from jax import lax
from jax.experimental import pallas as pl
from jax.experimental.pallas import tpu as pltpu

F32 = jnp.float32
BF16 = jnp.bfloat16

D_MODEL = 2048
DH = 128
H_A = 8
HKV = 2
GRP = H_A // HKV
L_CMP = 32
D_CMP = 16
CMP_R = L_CMP // D_CMP
CMP_HID = 2 * DH
L_SEL = 64
N_TOP = 16
WINDOW = 512
N_KV_SLOTS = 4
H_B = 4
DK_B = 128
DV_B = 256
GLA_LR = 16
GLA_GATE_NORM = 16.0
DK_C = 128
H_C = 16
DV_C = 128
EPS = 1e-6
FORCE_SCORE = 1e9
A_SIZES = (H_A * DH, 6 * HKV * DH, 3 * H_A, H_A * DH, H_B * DK_B, H_B * DK_B, H_B * DV_B, GLA_LR, H_B * DV_B)

LANES = 128
SUBLANES = 8
VMEM_LIMIT = 56 * 1024 * 1024

A_Q = 0
A_VB = 1024
A_ZB = 2048
A_ZA = 3072
A_QB = 4096
A_KB = 4608
A_KV = 5120
A_MISC = 6656
NA = 7168
MISC_LR = GRP * 3

NCP = 128
NSP = 128
TP = 8
NEG = -1e30
LOG2E = 1.4426950408889634


def _dot(a, b):
    return jnp.dot(a, b, preferred_element_type=F32)


def _dot_nt(a, b):
    return lax.dot_general(a, b, (((1,), (1,)), ((), ())), preferred_element_type=F32)


def _split3(a):
    hi = a.astype(BF16)
    r = a - hi.astype(F32)
    mid = r.astype(BF16)
    lo = (r - mid.astype(F32)).astype(BF16)
    return hi, mid, lo


def _dot_exact_rhs(a, b_exact):
    hi, mid, lo = _split3(a)
    return _dot(hi, b_exact) + _dot(mid, b_exact) + _dot(lo, b_exact)


def _dot_exact_lhs(a_exact, b):
    hi, mid, lo = _split3(b)
    return _dot(a_exact, hi) + _dot(a_exact, mid) + _dot(a_exact, lo)


def _sigmoid(x):
    return 1.0 / (1.0 + jnp.exp(-x))


def _silu(x):
    h = 0.5 * x
    return h + h * jnp.tanh(h)


def _log_sigmoid(x):
    return jnp.minimum(x, 0.0) - jnp.log(1.0 + jnp.exp(-jnp.abs(x)))


def _gelu_tanh(x):
    return 0.5 * x * (1.0 + jnp.tanh(np.sqrt(2.0 / np.pi) * (x + 0.044715 * (x * x * x))))


def _masked_softmax_pre(sm):
    m = jnp.max(sm, axis=-1, keepdims=True)
    e = jnp.where(sm > 0.5 * NEG, jnp.exp(sm - m), 0.0)
    l = jnp.sum(e, axis=-1, keepdims=True)
    return e / jnp.maximum(l, 1e-30)


def _params(sem):
    return pltpu.CompilerParams(dimension_semantics=sem, vmem_limit_bytes=VMEM_LIMIT)


def _norm_matmul_body(x_ref, nw_ref, w_ref, o_ref, *rest, copy_tile, w_rows):
    h_ref = rest[-1]

    @pl.when(pl.program_id(1) == 0)
    def _():
        x = x_ref[...]
        ms = jnp.mean(x * x, axis=-1, keepdims=True)
        h_ref[...] = (x * lax.rsqrt(ms + EPS) * nw_ref[...]).astype(BF16)

    res = _dot_nt(h_ref[...], w_ref[...]) if w_rows else _dot(h_ref[...], w_ref[...])
    o_ref[...] = res
    if copy_tile is not None:
        @pl.when(pl.program_id(1) == copy_tile)
        def _():
            rest[0][...] = res


def _norm_matmul(x, nw, w, tm, tn, copy_tile=None, w_rows=False):
    M, D = x.shape
    N = w.shape[0] if w_rows else w.shape[1]
    assert M % tm == 0 and N % tn == 0
    w_spec = pl.BlockSpec((tn, D), lambda i, j: (j, 0)) if w_rows else pl.BlockSpec((D, tn), lambda i, j: (0, j))
    out_specs = [pl.BlockSpec((tm, tn), lambda i, j: (i, j))]
    out_shape = [jax.ShapeDtypeStruct((M, N), F32)]
    if copy_tile is not None:
        out_specs.append(pl.BlockSpec((tm, tn), lambda i, j: (i, 0)))
        out_shape.append(jax.ShapeDtypeStruct((M, tn), F32))
    outs = pl.pallas_call(
        functools.partial(_norm_matmul_body, copy_tile=copy_tile, w_rows=w_rows),
        grid=(M // tm, N // tn),
        in_specs=[pl.BlockSpec((tm, D), lambda i, j: (i, 0)),
                  pl.BlockSpec((1, D), lambda i, j: (0, 0)),
                  w_spec],
        out_specs=out_specs,
        out_shape=out_shape,
        scratch_shapes=[pltpu.VMEM((tm, D), BF16)],
        compiler_params=_params(("arbitrary", "arbitrary")),
    )(x, nw, w)
    return outs[0] if copy_tile is None else outs


def _out_proj_body(a1_ref, a2_ref, w1_ref, w2_ref, x_ref, nw_ref, y_ref, *, final_norm):
    y = x_ref[...] + _dot(a1_ref[...].astype(BF16), w1_ref[...]) + _dot(a2_ref[...].astype(BF16), w2_ref[...])
    if final_norm:
        ms = jnp.mean(y * y, axis=-1, keepdims=True)
        y = y * lax.rsqrt(ms + EPS) * nw_ref[...]
    y_ref[...] = y


def _out_proj(a1, c1, a2, c2, w, x, nw, tm, final_norm):
    M, D = x.shape
    KH = w.shape[0] // 2
    assert M % tm == 0
    return pl.pallas_call(
        functools.partial(_out_proj_body, final_norm=final_norm),
        grid=(M // tm,),
        in_specs=[pl.BlockSpec((tm, KH), lambda i: (i, c1)),
                  pl.BlockSpec((tm, KH), lambda i: (i, c2)),
                  pl.BlockSpec((KH, D), lambda i: (0, 0)),
                  pl.BlockSpec((KH, D), lambda i: (1, 0)),
                  pl.BlockSpec((tm, D), lambda i: (i, 0)),
                  pl.BlockSpec((1, D), lambda i: (0, 0))],
        out_specs=pl.BlockSpec((tm, D), lambda i: (i, 0)),
        out_shape=jax.ShapeDtypeStruct((M, D), F32),
        compiler_params=_params(("arbitrary",)),
    )(a1, a2, w, w, x, nw)


def _cmp_bias_body(pe_ref, w1_ref, b1_ref, o_ref):
    pe = jnp.broadcast_to(pe_ref[...], (SUBLANES, pe_ref.shape[-1])).astype(BF16)
    o_ref[...] = b1_ref[...] + _dot(pe, w1_ref[...])[0:1, :]


def _cmp_bias(pe, w1, b1):
    S, KF, HID = w1.shape
    return pl.pallas_call(
        _cmp_bias_body,
        grid=(S,),
        in_specs=[pl.BlockSpec((None, 1, KF), lambda s: (s, 0, 0)),
                  pl.BlockSpec((None, KF, HID), lambda s: (s, 0, 0)),
                  pl.BlockSpec((None, 1, HID), lambda s: (s, 0, 0))],
        out_specs=pl.BlockSpec((None, 1, HID), lambda s: (s, 0, 0)),
        out_shape=jax.ShapeDtypeStruct((S, 1, HID), F32),
        compiler_params=_params(("arbitrary",)),
    )(pe, w1, b1)


def _compress_hidden(u0, u1, bias):
    return u0 + pltpu.roll(u1, NCP - 1, 0) + bias


def _compress_prompt_body(x_ref, w1_ref, cb_ref, w2_ref, o_ref):
    u0 = jnp.zeros((NCP, CMP_HID), F32)
    u1 = jnp.zeros((NCP, CMP_HID), F32)
    for pp in range(D_CMP // 2):
        xa = x_ref[pl.ds(2 * pp, NCP, stride=D_CMP), :]
        xb = x_ref[pl.ds(2 * pp + 1, NCP, stride=D_CMP), :]
        xp = jnp.concatenate([xa, xb], axis=1).astype(BF16)
        u0 = u0 + _dot(xp, w1_ref[pl.ds(pp * 2 * DH, 2 * DH), :])
        u1 = u1 + _dot(xp, w1_ref[pl.ds(D_CMP * DH + pp * 2 * DH, 2 * DH), :])
    h = _compress_hidden(u0, u1, cb_ref[...])
    o_ref[...] = _dot(_gelu_tanh(h).astype(BF16), w2_ref[...])


def _compress_prompt(P, w1, cb, w2, B, T):
    assert T // D_CMP == NCP
    kv0 = A_KV // DH
    return pl.pallas_call(
        _compress_prompt_body,
        grid=(B, 2, HKV),
        in_specs=[pl.BlockSpec((T, DH), lambda b, s, g: (b, kv0 + HKV * s + g)),
                  pl.BlockSpec((None, L_CMP * DH, CMP_HID), lambda b, s, g: (s, 0, 0)),
                  pl.BlockSpec((None, 1, CMP_HID), lambda b, s, g: (s, 0, 0)),
                  pl.BlockSpec((None, CMP_HID, DH), lambda b, s, g: (s, 0, 0))],
        out_specs=pl.BlockSpec((None, None, None, NCP, DH), lambda b, s, g: (b, s, g, 0, 0)),
        out_shape=jax.ShapeDtypeStruct((B, 2, HKV, NCP, DH), F32),
        compiler_params=_params(("arbitrary", "arbitrary", "arbitrary")),
    )(P, w1, cb, w2)


def _select_blocks(imp, tpos_tok, ns):
    blk = lax.broadcasted_iota(jnp.int32, (1, NSP), 1)
    cur = tpos_tok // L_SEL
    valid = blk <= cur
    forced = (blk == 0) | (blk == cur) | (blk == cur - 1)
    score = jnp.where(valid, jnp.where(forced, FORCE_SCORE, imp), -jnp.inf)
    k_top = float(min(N_TOP, ns))
    if score.shape[0] % LANES == 0:
        nb = -(-ns // SUBLANES) * SUBLANES
        rblk = lax.broadcasted_iota(jnp.int32, (nb, 1), 0)
        tops = []
        for t in range(score.shape[0] // LANES):
            st = score[t * LANES:(t + 1) * LANES].T[0:nb]
            rank = jnp.zeros(st.shape, F32)
            for j in range(ns):
                sj = st[j:j + 1, :]
                rank = rank + jnp.where((sj > st) | ((sj == st) & (rblk > j)), 1.0, 0.0)
            top = jnp.where(rank < k_top, 1.0, 0.0)
            tops.append(jnp.concatenate([top, jnp.zeros((NSP - nb, LANES), F32)], axis=0).T)
        top = tops[0] if len(tops) == 1 else jnp.concatenate(tops, axis=0)
        return jnp.where(valid, top, 0.0)
    rank = jnp.zeros(score.shape, F32)
    for j in range(ns):
        sj = score[:, j:j + 1]
        beats = (sj > score) | ((sj == score) & (blk > j))
        rank = rank + jnp.where(beats, 1.0, 0.0)
    return jnp.where(valid & (rank < float(min(N_TOP, ns))), 1.0, 0.0)


def _compressed_probs(qb, kc, tpos, nc, scale):
    s = _dot_nt(qb, kc.astype(BF16)) * scale
    cidx = lax.broadcasted_iota(jnp.int32, (1, NCP), 1)
    cmask = (cidx < nc) & (cidx * D_CMP + (L_CMP - 1) <= tpos)
    return _masked_softmax_pre(jnp.where(cmask, s, NEG))


def _compressed_branch(qb, kc, vc, tpos, nc, scale):
    p = _compressed_probs(qb, kc, tpos, nc, scale)
    return p, _dot(p.astype(BF16), vc.astype(BF16))


def _keys_with_block_onehot(k_ref, koh_ref, k0, width):
    return jnp.concatenate([k_ref[pl.ds(k0, width), :].astype(BF16), koh_ref[pl.ds(k0, width), :]], axis=1)


def _nsa_prompt_body(q_ref, kc_ref, vc_ref, ks_ref, vs_ref, kw_ref, vw_ref, misc_ref, z_ref, c2s_ref, koh_ref,
                     o_ref, sc_ref, mx_ref, l_ref, acc_ref, *, tq, tk, T):
    qi = pl.program_id(1)
    t0 = pl.multiple_of(qi * tq, tq)
    R = GRP * tq
    scale = DH ** -0.5
    c1 = scale * LOG2E
    ns = -(-T // L_SEL)
    nc = T // D_CMP - CMP_R + 1
    n_top = min(N_TOP, ns)
    heads = range(HKV)

    def hd(g):
        return slice(g * DH, (g + 1) * DH)

    qbs = [jnp.concatenate([q_ref[:, (g * GRP + r) * DH:(g * GRP + r + 1) * DH] for r in range(GRP)],
                           axis=0).astype(BF16) for g in heads]
    tpos = t0 + (lax.broadcasted_iota(jnp.int32, (R, 1), 0) & (tq - 1))
    tpos_tok = t0 + lax.broadcasted_iota(jnp.int32, (tq, 1), 0)
    blk = lax.broadcasted_iota(jnp.int32, (1, NSP), 1)

    nwt = WINDOW // tq + 1
    tiles, starts = [[] for _ in heads], []
    for i in range(nwt):
        k0 = t0 - WINDOW + i * tq
        k0c = pl.multiple_of(jnp.maximum(k0, 0), tq)
        kpos = k0 + lax.broadcasted_iota(jnp.int32, (1, tq), 1)
        for g in heads:
            s = _dot_nt(qbs[g], kw_ref[pl.ds(k0c, tq), hd(g)].astype(BF16))
            if i == 0:
                s = jnp.where((kpos > tpos - WINDOW) & (kpos >= 0), s, NEG)
            elif i == nwt - 1:
                s = jnp.where(kpos <= tpos, s, NEG)
            else:
                s = jnp.where(k0 >= 0, s, NEG)
            tiles[g].append(s)
        starts.append(k0c)
    es_w = []
    for g in heads:
        s = jnp.concatenate(tiles[g], axis=1)
        es_w.append(jnp.exp2((s - jnp.max(s, axis=-1, keepdims=True)) * c1))
    o_ws = [None for _ in heads]
    for i in range(nwt):
        for g in heads:
            pv = _dot(es_w[g][:, i * tq:(i + 1) * tq].astype(BF16), vw_ref[pl.ds(starts[i], tq), hd(g)].astype(BF16))
            o_ws[g] = pv if i == 0 else o_ws[g] + pv
    o_ws = [o_ws[g] / jnp.sum(es_w[g], axis=-1, keepdims=True) for g in heads]

    comp = [_compressed_branch(qbs[g], kc_ref[g], vc_ref[g], tpos, nc, scale) for g in heads]
    imps = []
    for g in heads:
        p_c = comp[g][0]
        psum = p_c[0:tq]
        for r in range(1, GRP):
            psum = psum + p_c[r * tq:(r + 1) * tq]
        imps.append(_dot_exact_rhs(psum, c2s_ref[...]))

    def all_valid():
        v = jnp.where(blk <= tpos_tok // L_SEL, 1.0, 0.0)
        return tuple(v for _ in heads)

    sels = lax.cond(t0 + tq <= n_top * L_SEL, all_valid,
                    lambda: tuple(_select_blocks(imps[g], tpos_tok, ns) for g in heads))

    q_augs, q_earlys = [], []
    for g in heads:
        sel_bias = (sels[g] - 1.0) * (-NEG)
        early_bias = jnp.where(blk < lax.div(t0, L_SEL), sel_bias, NEG)
        q_augs.append(jnp.concatenate([qbs[g], jnp.concatenate([sel_bias.astype(BF16)] * GRP, axis=0)], axis=1))
        q_earlys.append(jnp.concatenate([qbs[g], jnp.concatenate([early_bias.astype(BF16)] * GRP, axis=0)], axis=1))

    def keys_aug(k0, width, g):
        return jnp.concatenate([ks_ref[pl.ds(k0, width), hd(g)].astype(BF16), koh_ref[pl.ds(k0, width), :]], axis=1)

    kpos_d = t0 + lax.broadcasted_iota(jnp.int32, (1, tq), 1)
    s_diags = [jnp.where(kpos_d <= tpos, _dot_nt(q_augs[g], keys_aug(t0, tq, g)), NEG) for g in heads]
    n_early = lax.div(t0 + tk - 1, tk)
    for g in heads:
        mx_ref[g] = functools.reduce(jnp.maximum, [s_diags[g][:, c * LANES:(c + 1) * LANES]
                                                   for c in range(tq // LANES)])

    def pass1(kt, carry):
        k0 = pl.multiple_of(kt * tk, tk)
        ss = [_dot_nt(q_earlys[g], keys_aug(k0, tk, g)) for g in heads]
        old = [mx_ref[g] for g in heads]
        for g in heads:
            mx = ss[g][:, 0:LANES]
            for c in range(1, tk // LANES):
                mx = jnp.maximum(mx, ss[g][:, c * LANES:(c + 1) * LANES])
            sc_ref[g, kt] = ss[g]
            mx_ref[g] = jnp.maximum(old[g], mx)
        return carry

    lax.fori_loop(0, n_early, pass1, 0)
    m2s = [jnp.broadcast_to(jnp.max(mx_ref[g], axis=-1, keepdims=True) * c1, (R, LANES)) for g in heads]
    e_ds = [jnp.concatenate([jnp.exp2(s_diags[g][:, c * LANES:(c + 1) * LANES] * c1 - m2s[g])
                             for c in range(tq // LANES)], axis=1) for g in heads]
    pv_ds = [_dot(e_ds[g].astype(BF16), vs_ref[pl.ds(t0, tq), hd(g)].astype(BF16)) for g in heads]
    for g in heads:
        mx_ref[g] = m2s[g]
        l_ref[g] = sum(e_ds[g][:, c * LANES:(c + 1) * LANES] for c in range(tq // LANES))
        acc_ref[g] = pv_ds[g]

    def pass2(kt, carry):
        k0 = pl.multiple_of(kt * tk, tk)
        ss = [sc_ref[g, kt] for g in heads]
        m2 = [mx_ref[g] for g in heads]
        l_old = [l_ref[g] for g in heads]
        acc_old = [acc_ref[g] for g in heads]
        ebs, lsums = [], []
        for g in heads:
            es = [jnp.exp2(ss[g][:, c * LANES:(c + 1) * LANES] * c1 - m2[g]) for c in range(tk // LANES)]
            lsum = es[0]
            for ec in es[1:]:
                lsum = lsum + ec
            lsums.append(lsum)
            ebs.append(jnp.concatenate(es, axis=1).astype(BF16))
        pvs = [_dot(ebs[g], vs_ref[pl.ds(k0, tk), hd(g)].astype(BF16)) for g in heads]
        for g in heads:
            l_ref[g] = l_old[g] + lsums[g]
            acc_ref[g] = acc_old[g] + pvs[g]
        return carry

    lax.fori_loop(0, n_early, pass2, 0)

    outs = []
    for g in heads:
        o_s = acc_ref[g] / jnp.sum(l_ref[g], axis=-1, keepdims=True)
        gm = _sigmoid(misc_ref[:, g * LANES:g * LANES + GRP * 3])

        def gate(j):
            return jnp.concatenate([gm[:, r * 3 + j:r * 3 + j + 1] for r in range(GRP)], axis=0)

        outs.append(gate(0) * comp[g][1] + gate(1) * o_s + gate(2) * o_ws[g])
    for g in heads:
        for r in range(GRP):
            c = (g * GRP + r) * DH
            o_ref[:, c:c + DH] = (outs[g][r * tq:(r + 1) * tq] * _silu(z_ref[:, c:c + DH])).astype(BF16)


def _nsa_prompt(P, kcv, c2s, koh, B, T, tq, tk):
    assert T % tq == 0 and tq & (tq - 1) == 0 and tk % tq == 0 and T % tk == 0 and WINDOW % tq == 0
    nq = T // tq
    qw = H_A * DH
    kw = HKV * DH
    R = GRP * tq

    def kvspec(slot):
        return pl.BlockSpec((T, kw), lambda b, i: (b, A_KV // kw + slot))

    return pl.pallas_call(
        functools.partial(_nsa_prompt_body, tq=tq, tk=tk, T=T),
        grid=(B, nq),
        in_specs=[pl.BlockSpec((tq, qw), lambda b, i: (b * nq + i, A_Q // qw)),
                  pl.BlockSpec((None, None, HKV, NCP, DH), lambda b, i: (b, 0, 0, 0, 0)),
                  pl.BlockSpec((None, None, HKV, NCP, DH), lambda b, i: (b, 1, 0, 0, 0)),
                  kvspec(2), kvspec(3), kvspec(4), kvspec(5),
                  pl.BlockSpec((tq, HKV * LANES), lambda b, i: (b * nq + i, A_MISC // (HKV * LANES))),
                  pl.BlockSpec((tq, qw), lambda b, i: (b * nq + i, A_ZA // qw)),
                  pl.BlockSpec((NCP, NSP), lambda b, i: (0, 0)),
                  pl.BlockSpec((T, NSP), lambda b, i: (0, 0))],
        out_specs=pl.BlockSpec((tq, qw), lambda b, i: (b * nq + i, 0)),
        out_shape=jax.ShapeDtypeStruct((B * T, qw), BF16),
        scratch_shapes=[pltpu.VMEM((HKV, T // tk, R, tk), F32), pltpu.VMEM((HKV, R, LANES), F32),
                        pltpu.VMEM((HKV, R, LANES), F32), pltpu.VMEM((HKV, R, DH), F32)],
        compiler_params=_params(("arbitrary", "arbitrary")),
    )(P, kcv, kcv, P, P, P, P, P, P, c2s, koh)


def _pad_rows(x, rows):
    return jnp.concatenate([x, jnp.zeros((rows - x.shape[0], x.shape[1]), x.dtype)], axis=0)


def _nsa_sample_body(pt_ref, cache_ref, win_ref, q_ref, kvr_ref, kvw_ref, misc_ref, z_ref, w1_ref, cb_ref, w2_ref,
                     c2s_ref, koh_ref, o_ref, wout_ref, pbuf, sem, *, n_pages, page, past_len, dec_seq, wb):
    b = pl.program_id(0)
    scale = DH ** -0.5
    c1 = scale * LOG2E
    L = past_len + dec_seq
    ns = -(-L // L_SEL)
    nc = L // D_CMP - CMP_R + 1
    R = GRP * TP
    per_page = page // D_CMP
    nch = N_KV_SLOTS * HKV
    grp_rows = D_CMP * nch
    pitch = grp_rows + 1

    def page_copies(seq, slot):
        cps = []
        for j in range(n_pages):
            pid = pt_ref[seq * n_pages + j]
            for n in range(per_page):
                cps.append(pltpu.make_async_copy(
                    cache_ref.at[pid, pl.ds(n * grp_rows, grp_rows), :],
                    pbuf.at[slot, pl.ds((j * per_page + n) * pitch, grp_rows), :], sem.at[slot]))
        return cps

    slot = lax.rem(b, 2)

    @pl.when(b == 0)
    def _():
        for cp in page_copies(0, 0):
            cp.start()

    for cp in page_copies(b, slot):
        cp.wait()

    last = pl.num_programs(0) - 1
    next_copies = page_copies(jnp.minimum(b + 1, last), 1 - slot)
    copies_per_group = len(next_copies) // (2 * (D_CMP // 2))

    def gather(p, ch):
        return pbuf[slot, pl.ds(p * nch + ch, NCP, stride=pitch), :]

    trow = lax.broadcasted_iota(jnp.int32, (R, 1), 0) & (TP - 1)
    tpos = past_len + trow
    tpos_tok = past_len + lax.broadcasted_iota(jnp.int32, (TP, 1), 0)
    lane = lax.broadcasted_iota(jnp.int32, (1, LANES), 1)
    new_pos = past_len + lane
    new_ok = (new_pos <= tpos) & (new_pos < L)
    new_blk = (past_len + lax.broadcasted_iota(jnp.int32, (LANES, 1), 0)) // L_SEL
    new_oh = jnp.where(lax.broadcasted_iota(jnp.int32, (LANES, NSP), 1) == new_blk, 1.0, 0.0).astype(BF16)
    wpos = (past_len - wb) + lax.broadcasted_iota(jnp.int32, (1, wb), 1)
    wch = 2 * HKV

    heads = range(HKV)
    n_iter = D_CMP // 2
    qbs = [jnp.concatenate([q_ref[:, (g * GRP + r) * DH:(g * GRP + r + 1) * DH] for r in range(GRP)],
                           axis=0).astype(BF16) for g in heads]
    raw = [[] for _ in heads]
    win = {}

    def stage_window_scores():
        es = []
        for g in heads:
            s_w = _dot_nt(qbs[g], win_ref[pl.ds(g, wb, stride=wch), :].astype(BF16))
            s_w = jnp.where((wpos <= tpos) & (wpos > tpos - WINDOW), s_w, NEG)
            s_n = _dot_nt(qbs[g], _pad_rows(kvw_ref[:, g * DH:(g + 1) * DH], LANES).astype(BF16))
            s_n = jnp.where(new_ok & (new_pos > tpos - WINDOW), s_n, NEG)
            s = jnp.concatenate([s_w, s_n], axis=1)
            es.append(jnp.exp2((s - jnp.max(s, axis=-1, keepdims=True)) * c1))
        win["e"] = es

    def stage_window_out():
        outs_w = []
        for g in heads:
            e = win["e"][g]
            o_w = (_dot(e[:, :wb].astype(BF16), win_ref[pl.ds(HKV + g, wb, stride=wch), :].astype(BF16))
                   + _dot(e[:, wb:].astype(BF16),
                          _pad_rows(kvw_ref[:, (HKV + g) * DH:(HKV + g + 1) * DH], LANES).astype(BF16)))
            outs_w.append(o_w / jnp.sum(e, axis=-1, keepdims=True))
        win["o"] = outs_w

    def stage_raw_scores(p_lo, p_hi):
        for p in range(p_lo, p_hi):
            for g in heads:
                raw[g].append(_dot_nt(qbs[g], gather(p, 2 * HKV + g).astype(BF16)))

    def stage_raw_new():
        for g in heads:
            ksel_col = (2 * HKV + g) * DH
            raw[g].append(_dot_nt(qbs[g], _pad_rows(kvr_ref[:, ksel_col:ksel_col + DH], LANES).astype(BF16)))


    def compress_step(s, pp, u0, u1):
        xs = [jnp.concatenate([gather(2 * pp, s * HKV + g), gather(2 * pp + 1, s * HKV + g)], axis=1) for g in heads]
        xp = jnp.concatenate(xs, axis=0).astype(BF16)
        u0 = u0 + _dot(xp, w1_ref[s, pl.ds(pp * 2 * DH, 2 * DH), :])
        u1 = u1 + _dot(xp, w1_ref[s, pl.ds(D_CMP * DH + pp * 2 * DH, 2 * DH), :])
        grp = s * n_iter + pp
        for cp in next_copies[grp * copies_per_group:(grp + 1) * copies_per_group]:
            cp.start()
        return u0, u1

    def compress_out(s, u0, u1):
        return [_dot(_gelu_tanh(_compress_hidden(u0[g * NCP:(g + 1) * NCP], u1[g * NCP:(g + 1) * NCP],
                                                 cb_ref[s])).astype(BF16), w2_ref[s]) for g in heads]

    quarter = D_CMP // 4
    early = {0: stage_window_scores, 1: lambda: stage_raw_scores(0, quarter),
             2: lambda: stage_raw_scores(quarter, 2 * quarter), 3: stage_window_out,
             4: lambda: stage_raw_scores(2 * quarter, 3 * quarter),
             5: lambda: stage_raw_scores(3 * quarter, D_CMP), 6: stage_raw_new}
    u0 = u1 = jnp.zeros((HKV * NCP, CMP_HID), F32)
    for pp in range(n_iter):
        u0, u1 = compress_step(0, pp, u0, u1)
        if pp in early:
            early[pp]()
    kc = compress_out(0, u0, u1)

    st = {}

    def stage_probs():
        st["p_c"] = [_compressed_probs(qbs[g], kc[g], tpos, nc, scale) for g in heads]

    def stage_select():
        imps = []
        for g in heads:
            p_c = st["p_c"][g]
            psum = p_c[0:TP]
            for r in range(1, GRP):
                psum = psum + p_c[r * TP:(r + 1) * TP]
            imps.append(_dot_exact_rhs(psum, c2s_ref[...]))
        st["sel"] = [_select_blocks(imps[g], tpos_tok, ns) for g in heads]

    def stage_scores():
        biases = [jnp.concatenate([((st["sel"][g] - 1.0) * (-NEG)).astype(BF16)] * GRP, axis=0) for g in heads]
        bias_past = [_dot_nt(biases[g], koh_ref[...]) for g in heads]
        bias_new = [jnp.where(new_ok, _dot_nt(biases[g], new_oh), NEG) for g in heads]
        es = []
        for g in heads:
            s = jnp.concatenate([pc + bias_past[g] for pc in raw[g][:D_CMP]] + [raw[g][D_CMP] + bias_new[g]], axis=1)
            es.append(jnp.exp2((s - jnp.max(s, axis=-1, keepdims=True)) * c1))
        st["e"] = es
        st["o_s"] = [_dot(es[g][:, D_CMP * NCP:].astype(BF16),
                          _pad_rows(kvr_ref[:, (3 * HKV + g) * DH:(3 * HKV + g + 1) * DH], LANES).astype(BF16))
                     for g in heads]

    def stage_pv(p_lo, p_hi):
        for p in range(p_lo, p_hi):
            for g in heads:
                st["o_s"][g] = st["o_s"][g] + _dot(st["e"][g][:, p * NCP:(p + 1) * NCP].astype(BF16),
                                                  gather(p, 3 * HKV + g).astype(BF16))

    after = {0: stage_probs, 3: stage_select, 6: stage_scores, 7: lambda: stage_pv(0, D_CMP // 2)}
    u0 = u1 = jnp.zeros((HKV * NCP, CMP_HID), F32)
    for pp in range(n_iter):
        u0, u1 = compress_step(1, pp, u0, u1)
        if pp in after:
            after[pp]()
    vc = compress_out(1, u0, u1)
    stage_pv(D_CMP // 2, D_CMP)

    outs = []
    for g in heads:
        o_c = _dot(st["p_c"][g].astype(BF16), vc[g].astype(BF16))
        o_s = st["o_s"][g] / jnp.sum(st["e"][g], axis=-1, keepdims=True)
        gm = _sigmoid(misc_ref[:, g * LANES:g * LANES + GRP * 3])

        def gate(j):
            return jnp.concatenate([gm[:, r * 3 + j:r * 3 + j + 1] for r in range(GRP)], axis=0)

        outs.append(gate(0) * o_c + gate(1) * o_s + gate(2) * win["o"][g])

    for g in range(HKV):
        for r in range(GRP):
            c = (g * GRP + r) * DH
            o_ref[:, c:c + DH] = outs[g][r * TP:(r + 1) * TP] * _silu(z_ref[:, c:c + DH])

    wout_ref[pl.ds(0, (wb - dec_seq) * wch), :] = win_ref[pl.ds(dec_seq * wch, (wb - dec_seq) * wch), :]
    for t in range(dec_seq):
        for ch in range(wch):
            wout_ref[pl.ds((wb - dec_seq + t) * wch + ch, 1), :] = kvw_ref[t:t + 1, ch * DH:(ch + 1) * DH]

    @pl.when(b == last)
    def _():
        for cp in next_copies:
            cp.wait()


def _nsa_sample(Ps, cache, cache_win, page_table, w1, cb, w2, c2s, dec_seq):
    NB, n_pages = page_table.shape
    n_pool, prow, _ = cache.shape
    page = prow // (N_KV_SLOTS * HKV)
    wb = cache_win.shape[1] // (2 * HKV)
    past_len = n_pages * page
    L = past_len + dec_seq
    assert wb == WINDOW and (L // D_CMP) == NCP and NCP * D_CMP == past_len and dec_seq <= TP
    assert -(-L // L_SEL) < NSP and page % D_CMP == 0 and (dec_seq * 2 * HKV) % SUBLANES == 0
    qw = H_A * DH
    nch = N_KV_SLOTS * HKV
    koh = jnp.asarray(np.arange(NCP)[:, None] * D_CMP // L_SEL == np.arange(NSP)[None, :], dtype=BF16)

    grid_spec = pltpu.PrefetchScalarGridSpec(
        num_scalar_prefetch=1,
        grid=(NB,),
        in_specs=[
            pl.BlockSpec(memory_space=pl.ANY),
            pl.BlockSpec((None, wb * 2 * HKV, DH), lambda b, pt: (b, 0, 0)),
            pl.BlockSpec((TP, qw), lambda b, pt: (b, A_Q // qw)),
            pl.BlockSpec((TP, 4 * HKV * DH), lambda b, pt: (b, A_KV // (4 * HKV * DH))),
            pl.BlockSpec((TP, 2 * HKV * DH), lambda b, pt: (b, (A_KV + 4 * HKV * DH) // (2 * HKV * DH))),
            pl.BlockSpec((TP, HKV * LANES), lambda b, pt: (b, A_MISC // (HKV * LANES))),
            pl.BlockSpec((TP, qw), lambda b, pt: (b, A_ZA // qw)),
            pl.BlockSpec((2, L_CMP * DH, CMP_HID), lambda b, pt: (0, 0, 0)),
            pl.BlockSpec((2, 1, CMP_HID), lambda b, pt: (0, 0, 0)),
            pl.BlockSpec((2, CMP_HID, DH), lambda b, pt: (0, 0, 0)),
            pl.BlockSpec((NCP, NSP), lambda b, pt: (0, 0)),
            pl.BlockSpec((NCP, NSP), lambda b, pt: (0, 0))],
        out_specs=[pl.BlockSpec((TP, qw), lambda b, pt: (b, 0)),
                   pl.BlockSpec((None, wb * 2 * HKV, DH), lambda b, pt: (b, 0, 0))],
        scratch_shapes=[pltpu.VMEM((2, NCP * (D_CMP * nch + 1), DH), F32), pltpu.SemaphoreType.DMA((2,))],
    )
    return pl.pallas_call(
        functools.partial(_nsa_sample_body, n_pages=n_pages, page=page, past_len=past_len, dec_seq=dec_seq, wb=wb),
        grid_spec=grid_spec,
        out_shape=[jax.ShapeDtypeStruct((NB * TP, qw), F32),
                   jax.ShapeDtypeStruct((NB, wb * 2 * HKV, DH), F32)],
        compiler_params=_params(("arbitrary",)),
    )(page_table.reshape(-1), cache, cache_win, Ps, Ps, Ps, Ps, Ps, w1, cb, w2, c2s, koh)


def _rec_chunk(q, k, v, la, gate, S, C, last_row, nh, K, V):
    sb = min(16, C)
    ri = lax.broadcasted_iota(jnp.int32, (C, C), 0)
    cj = lax.broadcasted_iota(jnp.int32, (C, C), 1)
    causal = cj <= ri
    tri = jnp.where(causal, 1.0, 0.0).astype(BF16)
    b = _dot_exact_lhs(tri, la)
    qe = (q * jnp.exp(b)).astype(BF16)
    qis, kis = [], []
    k_ref_prev, c_prev = None, None
    for i in range(C // sb):
        r0 = i * sb
        ci = b[r0 + sb // 2:r0 + sb // 2 + 1, :]
        qis.append((q[r0:r0 + sb] * jnp.exp(b[r0:r0 + sb] - ci)).astype(BF16))
        k_new = k[r0:r0 + sb] * jnp.exp(jnp.minimum(ci - b[r0:r0 + sb], 80.0))
        k_scaled = k_new if i == 0 else jnp.concatenate([k_ref_prev * jnp.exp(ci - c_prev), k_new], axis=0)
        k_ref_prev, c_prev = k_scaled, ci
        rest = C - r0 - sb
        kis.append((k_scaled if rest == 0 else
                    jnp.concatenate([k_scaled, jnp.zeros((rest, k.shape[1]), F32)], axis=0)).astype(BF16))
    b_last = b[last_row:last_row + 1, :]
    rowi = lax.broadcasted_iota(jnp.int32, (C, 1), 0)
    kd = jnp.where(rowi <= last_row, k * jnp.exp(jnp.minimum(b_last - b, 0.0)), 0.0)
    stack = jnp.concatenate([kd, jnp.broadcast_to(b_last, (SUBLANES, nh * K)),
                             jnp.zeros((LANES - C - SUBLANES, nh * K), F32)], axis=0)
    vb = v.astype(BF16)
    v_pad = jnp.concatenate([vb, jnp.zeros((LANES - C, nh * V), BF16)], axis=0)
    outs, s_new = [], []
    for h in range(nh):
        ks = slice(h * K, (h + 1) * K)
        vs = slice(h * V, (h + 1) * V)
        o = _dot(qe[:, ks], S[h].astype(BF16))
        rows = [_dot_nt(qi[:, ks], ki[:, ks]) for qi, ki in zip(qis, kis)]
        att = rows[0] if len(rows) == 1 else jnp.concatenate(rows, axis=0)
        att = jnp.where(causal, att, 0.0)
        o = o + _dot(att.astype(BF16), vb[:, vs])
        stack_t = stack[:, ks].T
        a_col = jnp.exp(stack_t[:, C:C + 1])
        s_new.append(a_col * S[h] + _dot(stack_t.astype(BF16), v_pad[:, vs]))
        ms = jnp.mean(o * o, axis=-1, keepdims=True)
        outs.append(o * lax.rsqrt(ms + EPS) * gate[:, vs])
    return outs, s_new


def _rec_body(*refs, variant, nh, K, V, C, TB, SB, last_row, has_s0, layer_idx):
    refs = list(refs)
    q_ref, k_ref, v_ref, z_ref = refs[:4]
    pos = 4
    if variant == "gla":
        misc_ref, w2_ref, gb_ref = refs[pos:pos + 3]
        pos += 3
    else:
        lb_ref = refs[pos]
        pos += 1
    gn_ref = refs[pos]
    pos += 1
    if has_s0:
        s0_ref = refs[pos]
        pos += 1
    o_ref, s_ref = refs[pos:pos + 2]
    n_chunks = TB // C
    single = n_chunks == 1 and has_s0

    if not single:
        @pl.when(pl.program_id(2) == 0)
        def _():
            if has_s0:
                s_ref[...] = s0_ref[...]
            else:
                s_ref[...] = jnp.zeros(s_ref.shape, F32)

    if variant == "hgrn":
        lg = lb_ref[...]
        e = jnp.exp(lg - jnp.max(lg, axis=0, keepdims=True))
        prob = e / jnp.sum(e, axis=0, keepdims=True)
        lb = prob[1:2]
        for i in range(2, layer_idx + 1):
            lb = lb + prob[i:i + 1]
        if SB > 1:
            lb = jnp.concatenate([lb] * SB, axis=1)
    gn_all = jnp.concatenate([gn_ref[...]] * (nh * SB), axis=1)

    def chunk(c0):
        def rows(s):
            return pl.ds(s * TB + c0, C)

        def cat(f):
            return f(0) if SB == 1 else jnp.concatenate([f(s) for s in range(SB)], axis=1)

        qr = cat(lambda s: q_ref[rows(s), :])
        kr = cat(lambda s: k_ref[rows(s), :])
        v = cat(lambda s: v_ref[rows(s), :])
        gate = _silu(cat(lambda s: z_ref[rows(s), :])) * gn_all
        if variant == "gla":
            w2 = w2_ref[...].astype(BF16)
            zg = cat(lambda s: _dot(misc_ref[rows(s), :].astype(BF16), w2) + gb_ref[...])
            q, k, la = qr * (K ** -0.5), kr, _log_sigmoid(zg) / GLA_GATE_NORM
        else:
            t = jnp.exp(-jnp.abs(kr))
            r = 1.0 / (1.0 + t)
            tr = t * r
            nonneg = kr >= 0.0
            sig = jnp.where(nonneg, r, tr)
            nsig = jnp.where(nonneg, tr, r)
            q, k, la = _silu(qr), (1.0 - lb) * nsig, jnp.log(lb + (1.0 - lb) * sig)
        src = s0_ref if single else s_ref
        states = [src[s, h] for s in range(SB) for h in range(nh)]
        outs, s_new = _rec_chunk(q, k, v, la, gate, states, C, last_row, SB * nh, K, V)
        for s in range(SB):
            for h in range(nh):
                o_ref[rows(s), h * V:(h + 1) * V] = outs[s * nh + h].astype(o_ref.dtype)
                s_ref[s, h] = s_new[s * nh + h]

    if n_chunks == 1:
        chunk(0)
    else:
        def body(ci, carry):
            chunk(pl.multiple_of(ci * C, C))
            return carry

        lax.fori_loop(0, n_chunks, body, 0, unroll=8)


def _recurrence(P, cols, extra, gn, s0, *, variant, B, T, H, K, V, C, TB, nh, last_row, out_dtype, SB=1,
                layer_idx=1):
    cq, ck, cv, cz = cols
    assert H % nh == 0 and T % TB == 0 and TB % C == 0 and B % SB == 0
    nt = T // TB
    assert SB == 1 or nt == 1

    def colspec(c0, w):
        assert c0 % (nh * w) == 0
        return pl.BlockSpec((SB * TB, nh * w), lambda b, j, t: (b * nt + t, c0 // (nh * w) + j))

    in_specs = [colspec(cq, K), colspec(ck, K), colspec(cv, V), colspec(cz, V)]
    args = [P, P, P, P]
    if variant == "gla":
        w2p, gb = extra
        in_specs += [pl.BlockSpec((SB * TB, LANES), lambda b, j, t: (b * nt + t, A_MISC // LANES)),
                     pl.BlockSpec((LANES, nh * K), lambda b, j, t: (0, j)),
                     pl.BlockSpec((1, nh * K), lambda b, j, t: (0, j))]
        args += [P, w2p, gb]
    else:
        (lb_logits,) = extra
        in_specs += [pl.BlockSpec((lb_logits.shape[0], nh * K), lambda b, j, t: (0, j))]
        args += [lb_logits]
    in_specs += [pl.BlockSpec((1, V), lambda b, j, t: (0, 0))]
    args += [gn]
    st_spec = pl.BlockSpec((None, SB, nh, K, V), lambda b, j, t: (0, b, j, 0, 0))
    if s0 is not None:
        in_specs += [st_spec]
        args += [s0]
    return pl.pallas_call(
        functools.partial(_rec_body, variant=variant, nh=nh, K=K, V=V, C=C, TB=TB, SB=SB, last_row=last_row,
                          has_s0=s0 is not None, layer_idx=layer_idx),
        grid=(B // SB, H // nh, nt),
        in_specs=in_specs,
        out_specs=[pl.BlockSpec((SB * TB, nh * V), lambda b, j, t: (b * nt + t, j)), st_spec],
        out_shape=[jax.ShapeDtypeStruct((B * T, H * V), out_dtype),
                   jax.ShapeDtypeStruct((1, B, H, K, V), F32)],
        compiler_params=_params(("arbitrary", "arbitrary", "arbitrary")),
    )(*args)


def _layout_a_w_in_body(w_ref, o_ref):
    off = [int(v) for v in np.concatenate([[0], np.cumsum(A_SIZES)])]
    q, kv, gbr, za, qb, kb, vb, lr, zb = [w_ref[off[i]:off[i + 1], :] for i in range(len(A_SIZES))]
    cols = w_ref.shape[1]
    z = lambda n: jnp.zeros((n, cols), F32)
    misc0 = jnp.concatenate([gbr[:GRP * 3], lr, z(LANES - GRP * 3 - GLA_LR)], axis=0)
    misc1 = jnp.concatenate([gbr[GRP * 3:], z(LANES - GRP * 3)], axis=0)
    out = jnp.concatenate([q, vb, zb, za, qb, kb, kv, misc0, misc1, z(NA - A_MISC - HKV * LANES)], axis=0)
    o_ref[...] = out.astype(BF16)


def _layout_a_w_in(wt, tc=256):
    n, D = wt.shape
    assert D % tc == 0 and n == sum(A_SIZES)
    return pl.pallas_call(
        _layout_a_w_in_body,
        grid=(D // tc,),
        in_specs=[pl.BlockSpec((n, tc), lambda i: (0, i))],
        out_specs=pl.BlockSpec((NA, tc), lambda i: (0, i)),
        out_shape=jax.ShapeDtypeStruct((NA, D), BF16),
        compiler_params=_params(("arbitrary",)),
    )(wt)


def _cmp_to_sel():
    c_start = np.arange(NCP)[:, None] * D_CMP
    s_start = np.arange(NSP)[None, :] * L_SEL
    overlap = np.clip(np.minimum(c_start + L_CMP, s_start + L_SEL) - np.maximum(c_start, s_start), 0, None)
    return jnp.asarray(overlap / D_CMP, dtype=BF16)


class _Tiles(NamedTuple):
    proj_rows: int
    proj_cols: int
    out_rows: int
    nsa_q: int
    nsa_k: int
    rec_chunk: int
    rec_block: int
    gla_heads: int
    hgrn_heads: int
    gla_seqs: int
    hgrn_seqs: int


def _tiles(n_decode):
    return _Tiles(proj_rows=1024, proj_cols=N_KV_SLOTS * HKV * DH, out_rows=512, nsa_q=256, nsa_k=512,
                  rec_chunk=64, rec_block=512, gla_heads=H_B, hgrn_heads=8,
                  gla_seqs=math.gcd(n_decode, 16), hgrn_seqs=math.gcd(n_decode, 8))


def kernel(x_prompt, x_sample, cache_kv, cache_win, state_gla, state_hgrn, page_table, a_norm, a_w_in, a_gla_w2,
           a_gla_b, a_gla_gn, a_cmp_pe, a_cmp_w1, a_cmp_b1, a_cmp_w2, a_w_out, c_norm, c_w_in, c_lb_logits, c_gn,
           c_w_out, final_norm):
    B, T, D = x_prompt.shape
    NB, TS, _ = x_sample.shape
    n_pool, page = cache_kv.shape[1], cache_kv.shape[2]
    wb = cache_win.shape[2]
    assert a_norm.shape[0] == 1 and c_norm.shape[0] == 1 and c_lb_logits.shape[0] == 2

    wa = _layout_a_w_in(a_w_in[0].T)
    wc = c_w_in[0].astype(BF16)
    wao = a_w_out[0].astype(BF16)
    wco = c_w_out[0].astype(BF16)
    w1 = a_cmp_w1[0].astype(BF16)
    w2 = a_cmp_w2[0].astype(BF16)
    pe = a_cmp_pe[0].reshape(2, 1, L_CMP * DH)
    b1 = a_cmp_b1[0].reshape(2, 1, CMP_HID)
    w2p = jnp.zeros((LANES, H_B * DK_B), F32).at[MISC_LR:MISC_LR + GLA_LR, :].set(a_gla_w2[0])
    gb = a_gla_b[0].reshape(1, H_B * DK_B)
    c2s = _cmp_to_sel()
    a_nw = a_norm[0].reshape(1, D)
    c_nw = c_norm[0].reshape(1, D)
    f_nw = final_norm.reshape(1, D)
    gla_gn = a_gla_gn[0].reshape(1, DV_B)
    hg_gn = c_gn[0].reshape(1, DV_C)

    xp = x_prompt.reshape(B * T, D)
    xs = jnp.pad(x_sample, ((0, 0), (0, TP - TS), (0, 0))).reshape(NB * TP, D)

    cb = _cmp_bias(pe, w1, b1)

    tl = _tiles(NB)

    Pp, kv_rows_p = _norm_matmul(xp, a_nw, wa, tm=tl.proj_rows, tn=tl.proj_cols, copy_tile=A_KV // tl.proj_cols,
                                 w_rows=True)
    kcv = _compress_prompt(Pp, w1, cb, w2, B, T)
    koh = jnp.asarray(np.arange(T)[:, None] // L_SEL == np.arange(NSP)[None, :], dtype=BF16)
    oa_p = _nsa_prompt(Pp, kcv, c2s, koh, B, T, tq=tl.nsa_q, tk=tl.nsa_k)
    ob_p, gla_p = _recurrence(Pp, (A_QB, A_KB, A_VB, A_ZB), (w2p, gb), gla_gn, None, variant="gla", B=B, T=T,
                              H=H_B, K=DK_B, V=DV_B, C=tl.rec_chunk, TB=tl.rec_block, nh=tl.gla_heads,
                              last_row=tl.rec_chunk - 1, out_dtype=BF16)
    y1p = _out_proj(oa_p, 0, ob_p, 0, wao, xp, a_nw, tm=tl.out_rows, final_norm=False)

    Ps = _norm_matmul(xs, a_nw, wa, tm=NB * TP, tn=tl.proj_cols, w_rows=True)
    cache2 = cache_kv[0].reshape(n_pool, page * N_KV_SLOTS * HKV, DH)
    win2 = cache_win[0].reshape(NB, wb * 2 * HKV, DH)
    oa_s, win_out = _nsa_sample(Ps, cache2, win2, page_table, w1, cb, w2, c2s, TS)
    ob_s, gla_s = _recurrence(Ps, (A_QB, A_KB, A_VB, A_ZB), (w2p, gb), gla_gn, state_gla, variant="gla", B=NB, T=TP,
                              H=H_B, K=DK_B, V=DV_B, C=TP, TB=TP, nh=H_B, last_row=TS - 1, out_dtype=F32,
                              SB=tl.gla_seqs)
    y1s = _out_proj(oa_s, 0, ob_s, 0, wao, xs, a_nw, tm=min(tl.out_rows, NB * TP), final_norm=False)

    Pc = _norm_matmul(y1p, c_nw, wc, tm=tl.proj_rows, tn=tl.proj_cols)
    hc = H_C * DK_C
    oc_p, hg_p = _recurrence(Pc, (0, hc, 2 * hc, 3 * hc), (c_lb_logits,), hg_gn, None, variant="hgrn", B=B, T=T,
                             H=H_C, K=DK_C, V=DV_C, C=tl.rec_chunk, TB=tl.rec_block, nh=tl.hgrn_heads,
                             last_row=tl.rec_chunk - 1, out_dtype=BF16)
    y_prompt = _out_proj(oc_p, 0, oc_p, 1, wco, y1p, f_nw, tm=tl.out_rows, final_norm=True)

    Pcs = _norm_matmul(y1s, c_nw, wc, tm=NB * TP, tn=tl.proj_cols)
    oc_s, hg_s = _recurrence(Pcs, (0, hc, 2 * hc, 3 * hc), (c_lb_logits,), hg_gn, state_hgrn, variant="hgrn", B=NB,
                             T=TP, H=H_C, K=DK_C, V=DV_C, C=TP, TB=TP, nh=H_C, last_row=TS - 1, out_dtype=F32,
                             SB=tl.hgrn_seqs)
    y_sample = _out_proj(oc_s, 0, oc_s, 1, wco, y1s, f_nw, tm=min(tl.out_rows, NB * TP), final_norm=True)

    kvw = 6 * HKV * DH
    nrw = N_KV_SLOTS * HKV * DH
    Pp3 = Pp.reshape(B, T, NA)
    Ps3 = Ps.reshape(NB, TP, NA)
    kv_p = kv_rows_p.reshape(1, B, T, N_KV_SLOTS, HKV, DH)
    kv_s = Ps3[:, :TS, A_KV:A_KV + nrw].reshape(1, NB, TS, N_KV_SLOTS, HKV, DH)
    keep = min(WINDOW, T)
    win_p = Pp3[:, T - keep:, A_KV + nrw:A_KV + kvw].reshape(1, B, keep, 2, HKV, DH)
    win_s = win_out.reshape(1, NB, wb, 2, HKV, DH)
    return (y_prompt.reshape(B, T, D), y_sample.reshape(NB, TP, D)[:, :TS], kv_p, kv_s, win_p, win_s,
            gla_p, gla_s, hg_p, hg_s)
```

```python
import functools
import math
from typing import NamedTuple

import numpy as np
import jax
import jax.numpy as jnp
from jax import lax
from jax.experimental import pallas as pl
from jax.experimental.pallas import tpu as pltpu

F32 = jnp.float32
BF16 = jnp.bfloat16

D_MODEL = 2048
DH = 128
H_A = 8
HKV = 2
GRP = H_A // HKV
L_CMP = 32
D_CMP = 16
CMP_R = L_CMP // D_CMP
CMP_HID = 2 * DH
L_SEL = 64
N_TOP = 16
WINDOW = 512
N_KV_SLOTS = 4
H_B = 4
DK_B = 128
DV_B = 256
GLA_LR = 16
GLA_GATE_NORM = 16.0
DK_C = 128
H_C = 16
DV_C = 128
EPS = 1e-6
FORCE_SCORE = 1e9
A_SIZES = (H_A * DH, 6 * HKV * DH, 3 * H_A, H_A * DH, H_B * DK_B, H_B * DK_B, H_B * DV_B, GLA_LR, H_B * DV_B)

LANES = 128
SUBLANES = 8
VMEM_LIMIT = 56 * 1024 * 1024

A_Q = 0
A_VB = 1024
A_ZB = 2048
A_ZA = 3072
A_QB = 4096
A_KB = 4608
A_KV = 5120
A_MISC = 6656
NA = 7168
MISC_LR = GRP * 3

NCP = 128
NSP = 128
TP = 8
NEG = -1e30
LOG2E = 1.4426950408889634


def _dot(a, b):
    return jnp.dot(a, b, preferred_element_type=F32)


def _dot_nt(a, b):
    return lax.dot_general(a, b, (((1,), (1,)), ((), ())), preferred_element_type=F32)


def _split3(a):
    hi = a.astype(BF16)
    r = a - hi.astype(F32)
    mid = r.astype(BF16)
    lo = (r - mid.astype(F32)).astype(BF16)
    return hi, mid, lo


def _dot_exact_rhs(a, b_exact):
    hi, mid, lo = _split3(a)
    return _dot(hi, b_exact) + _dot(mid, b_exact) + _dot(lo, b_exact)


def _dot_exact_lhs(a_exact, b):
    hi, mid, lo = _split3(b)
    return _dot(a_exact, hi) + _dot(a_exact, mid) + _dot(a_exact, lo)


def _sigmoid(x):
    return 1.0 / (1.0 + jnp.exp(-x))


def _silu(x):
    h = 0.5 * x
    return h + h * jnp.tanh(h)


def _log_sigmoid(x):
    return jnp.minimum(x, 0.0) - jnp.log(1.0 + jnp.exp(-jnp.abs(x)))


def _gelu_tanh(x):
    return 0.5 * x * (1.0 + jnp.tanh(np.sqrt(2.0 / np.pi) * (x + 0.044715 * (x * x * x))))


def _masked_softmax_pre(sm):
    m = jnp.max(sm, axis=-1, keepdims=True)
    e = jnp.where(sm > 0.5 * NEG, jnp.exp(sm - m), 0.0)
    l = jnp.sum(e, axis=-1, keepdims=True)
    return e / jnp.maximum(l, 1e-30)


def _params(sem):
    return pltpu.CompilerParams(dimension_semantics=sem, vmem_limit_bytes=VMEM_LIMIT)


def _norm_matmul_body(x_ref, nw_ref, w_ref, o_ref, *rest, copy_tile, w_rows):
    h_ref = rest[-1]

    @pl.when(pl.program_id(1) == 0)
    def _():
        x = x_ref[...]
        ms = jnp.mean(x * x, axis=-1, keepdims=True)
        h_ref[...] = (x * lax.rsqrt(ms + EPS) * nw_ref[...]).astype(BF16)

    res = _dot_nt(h_ref[...], w_ref[...]) if w_rows else _dot(h_ref[...], w_ref[...])
    o_ref[...] = res
    if copy_tile is not None:
        @pl.when(pl.program_id(1) == copy_tile)
        def _():
            n_lt = res.shape[1] // LANES
            for c in range(n_lt):
                rest[0][pl.ds(c, res.shape[0], stride=n_lt), :] = res[:, c * LANES:(c + 1) * LANES]


def _norm_matmul(x, nw, w, tm, tn, copy_tile=None, w_rows=False):
    M, D = x.shape
    N = w.shape[0] if w_rows else w.shape[1]
    assert M % tm == 0 and N % tn == 0
    w_spec = pl.BlockSpec((tn, D), lambda i, j: (j, 0)) if w_rows else pl.BlockSpec((D, tn), lambda i, j: (0, j))
    out_specs = [pl.BlockSpec((tm, tn), lambda i, j: (i, j))]
    out_shape = [jax.ShapeDtypeStruct((M, N), F32)]
    if copy_tile is not None:
        out_specs.append(pl.BlockSpec((tm * (tn // LANES), LANES), lambda i, j: (i, 0)))
        out_shape.append(jax.ShapeDtypeStruct((M * (tn // LANES), LANES), F32))
    outs = pl.pallas_call(
        functools.partial(_norm_matmul_body, copy_tile=copy_tile, w_rows=w_rows),
        grid=(M // tm, N // tn),
        in_specs=[pl.BlockSpec((tm, D), lambda i, j: (i, 0)),
                  pl.BlockSpec((1, D), lambda i, j: (0, 0)),
                  w_spec],
        out_specs=out_specs,
        out_shape=out_shape,
        scratch_shapes=[pltpu.VMEM((tm, D), BF16)],
        compiler_params=_params(("arbitrary", "arbitrary")),
    )(x, nw, w)
    return outs[0] if copy_tile is None else outs


def _out_proj_body(a1_ref, a2_ref, w1_ref, w2_ref, x_ref, nw_ref, y_ref, *, final_norm):
    y = x_ref[...] + _dot(a1_ref[...].astype(BF16), w1_ref[...]) + _dot(a2_ref[...].astype(BF16), w2_ref[...])
    if final_norm:
        ms = jnp.mean(y * y, axis=-1, keepdims=True)
        y = y * lax.rsqrt(ms + EPS) * nw_ref[...]
    y_ref[...] = y


def _out_proj(a1, c1, a2, c2, w, x, nw, tm, final_norm):
    M, D = x.shape
    KH = w.shape[0] // 2
    assert M % tm == 0
    return pl.pallas_call(
        functools.partial(_out_proj_body, final_norm=final_norm),
        grid=(M // tm,),
        in_specs=[pl.BlockSpec((tm, KH), lambda i: (i, c1)),
                  pl.BlockSpec((tm, KH), lambda i: (i, c2)),
                  pl.BlockSpec((KH, D), lambda i: (0, 0)),
                  pl.BlockSpec((KH, D), lambda i: (1, 0)),
                  pl.BlockSpec((tm, D), lambda i: (i, 0)),
                  pl.BlockSpec((1, D), lambda i: (0, 0))],
        out_specs=pl.BlockSpec((tm, D), lambda i: (i, 0)),
        out_shape=jax.ShapeDtypeStruct((M, D), F32),
        compiler_params=_params(("arbitrary",)),
    )(a1, a2, w, w, x, nw)


def _cmp_bias_body(pe_ref, w1_ref, b1_ref, o_ref):
    pe = jnp.broadcast_to(pe_ref[...], (SUBLANES, pe_ref.shape[-1])).astype(BF16)
    o_ref[...] = b1_ref[...] + _dot(pe, w1_ref[...])[0:1, :]


def _cmp_bias(pe, w1, b1):
    S, KF, HID = w1.shape
    return pl.pallas_call(
        _cmp_bias_body,
        grid=(S,),
        in_specs=[pl.BlockSpec((None, 1, KF), lambda s: (s, 0, 0)),
                  pl.BlockSpec((None, KF, HID), lambda s: (s, 0, 0)),
                  pl.BlockSpec((None, 1, HID), lambda s: (s, 0, 0))],
        out_specs=pl.BlockSpec((None, 1, HID), lambda s: (s, 0, 0)),
        out_shape=jax.ShapeDtypeStruct((S, 1, HID), F32),
        compiler_params=_params(("arbitrary",)),
    )(pe, w1, b1)


def _compress_hidden(u0, u1, bias):
    return u0 + pltpu.roll(u1, NCP - 1, 0) + bias


def _compress_prompt_body(x_ref, w1_ref, cb_ref, w2_ref, o_ref):
    u0 = jnp.zeros((NCP, CMP_HID), F32)
    u1 = jnp.zeros((NCP, CMP_HID), F32)
    for pp in range(D_CMP // 2):
        xa = x_ref[pl.ds(2 * pp, NCP, stride=D_CMP), :]
        xb = x_ref[pl.ds(2 * pp + 1, NCP, stride=D_CMP), :]
        xp = jnp.concatenate([xa, xb], axis=1).astype(BF16)
        u0 = u0 + _dot(xp, w1_ref[pl.ds(pp * 2 * DH, 2 * DH), :])
        u1 = u1 + _dot(xp, w1_ref[pl.ds(D_CMP * DH + pp * 2 * DH, 2 * DH), :])
    h = _compress_hidden(u0, u1, cb_ref[...])
    o_ref[...] = _dot(_gelu_tanh(h).astype(BF16), w2_ref[...])


def _compress_prompt(P, w1, cb, w2, B, T):
    assert T // D_CMP == NCP
    kv0 = A_KV // DH
    return pl.pallas_call(
        _compress_prompt_body,
        grid=(B, 2, HKV),
        in_specs=[pl.BlockSpec((T, DH), lambda b, s, g: (b, kv0 + HKV * s + g)),
                  pl.BlockSpec((None, L_CMP * DH, CMP_HID), lambda b, s, g: (s, 0, 0)),
                  pl.BlockSpec((None, 1, CMP_HID), lambda b, s, g: (s, 0, 0)),
                  pl.BlockSpec((None, CMP_HID, DH), lambda b, s, g: (s, 0, 0))],
        out_specs=pl.BlockSpec((None, None, None, NCP, DH), lambda b, s, g: (b, s, g, 0, 0)),
        out_shape=jax.ShapeDtypeStruct((B, 2, HKV, NCP, DH), F32),
        compiler_params=_params(("arbitrary", "arbitrary", "arbitrary")),
    )(P, w1, cb, w2)


def _select_blocks(imp, tpos_tok, ns):
    blk = lax.broadcasted_iota(jnp.int32, (1, NSP), 1)
    cur = tpos_tok // L_SEL
    valid = blk <= cur
    forced = (blk == 0) | (blk == cur) | (blk == cur - 1)
    score = jnp.where(valid, jnp.where(forced, FORCE_SCORE, imp), -jnp.inf)
    k_top = float(min(N_TOP, ns))
    if score.shape[0] % LANES == 0:
        nb = -(-ns // SUBLANES) * SUBLANES
        rblk = lax.broadcasted_iota(jnp.int32, (nb, 1), 0)
        tops = []
        for t in range(score.shape[0] // LANES):
            st = score[t * LANES:(t + 1) * LANES].T[0:nb]
            rank = jnp.zeros(st.shape, F32)
            for j in range(ns):
                sj = st[j:j + 1, :]
                rank = rank + jnp.where((sj > st) | ((sj == st) & (rblk > j)), 1.0, 0.0)
            top = jnp.where(rank < k_top, 1.0, 0.0)
            tops.append(jnp.concatenate([top, jnp.zeros((NSP - nb, LANES), F32)], axis=0).T)
        top = tops[0] if len(tops) == 1 else jnp.concatenate(tops, axis=0)
        return jnp.where(valid, top, 0.0)
    rank = jnp.zeros(score.shape, F32)
    for j in range(ns):
        sj = score[:, j:j + 1]
        beats = (sj > score) | ((sj == score) & (blk > j))
        rank = rank + jnp.where(beats, 1.0, 0.0)
    return jnp.where(valid & (rank < float(min(N_TOP, ns))), 1.0, 0.0)


def _compressed_probs(qb, kc, tpos, nc, scale):
    s = _dot_nt(qb, kc.astype(BF16)) * scale
    cidx = lax.broadcasted_iota(jnp.int32, (1, NCP), 1)
    cmask = (cidx < nc) & (cidx * D_CMP + (L_CMP - 1) <= tpos)
    return _masked_softmax_pre(jnp.where(cmask, s, NEG))


def _compressed_branch(qb, kc, vc, tpos, nc, scale):
    p = _compressed_probs(qb, kc, tpos, nc, scale)
    return p, _dot(p.astype(BF16), vc.astype(BF16))


def _keys_with_block_onehot(k_ref, koh_ref, k0, width):
    return jnp.concatenate([k_ref[pl.ds(k0, width), :].astype(BF16), koh_ref[pl.ds(k0, width), :]], axis=1)


def _nsa_prompt_body(q_ref, kc_ref, vc_ref, ks_ref, vs_ref, kw_ref, vw_ref, misc_ref, z_ref, c2s_ref, koh_ref,
                     o_ref, sc_ref, mx_ref, l_ref, acc_ref, *, tq, tk, T):
    qi = pl.program_id(1)
    t0 = pl.multiple_of(qi * tq, tq)
    R = GRP * tq
    scale = DH ** -0.5
    c1 = scale * LOG2E
    ns = -(-T // L_SEL)
    nc = T // D_CMP - CMP_R + 1
    n_top = min(N_TOP, ns)
    heads = range(HKV)

    def hd(g):
        return slice(g * DH, (g + 1) * DH)

    qbs = [jnp.concatenate([q_ref[:, (g * GRP + r) * DH:(g * GRP + r + 1) * DH] for r in range(GRP)],
                           axis=0).astype(BF16) for g in heads]
    tpos = t0 + (lax.broadcasted_iota(jnp.int32, (R, 1), 0) & (tq - 1))
    tpos_tok = t0 + lax.broadcasted_iota(jnp.int32, (tq, 1), 0)
    blk = lax.broadcasted_iota(jnp.int32, (1, NSP), 1)

    nwt = WINDOW // tq + 1
    tiles, starts = [[] for _ in heads], []
    for i in range(nwt):
        k0 = t0 - WINDOW + i * tq
        k0c = pl.multiple_of(jnp.maximum(k0, 0), tq)
        kpos = k0 + lax.broadcasted_iota(jnp.int32, (1, tq), 1)
        for g in heads:
            s = _dot_nt(qbs[g], kw_ref[pl.ds(k0c, tq), hd(g)].astype(BF16))
            if i == 0:
                s = jnp.where((kpos > tpos - WINDOW) & (kpos >= 0), s, NEG)
            elif i == nwt - 1:
                s = jnp.where(kpos <= tpos, s, NEG)
            else:
                s = jnp.where(k0 >= 0, s, NEG)
            tiles[g].append(s)
        starts.append(k0c)
    es_w = []
    for g in heads:
        s = jnp.concatenate(tiles[g], axis=1)
        es_w.append(jnp.exp2((s - jnp.max(s, axis=-1, keepdims=True)) * c1))
    o_ws = [None for _ in heads]
    for i in range(nwt):
        for g in heads:
            pv = _dot(es_w[g][:, i * tq:(i + 1) * tq].astype(BF16), vw_ref[pl.ds(starts[i], tq), hd(g)].astype(BF16))
            o_ws[g] = pv if i == 0 else o_ws[g] + pv
    o_ws = [o_ws[g] / jnp.sum(es_w[g], axis=-1, keepdims=True) for g in heads]

    comp = [_compressed_branch(qbs[g], kc_ref[g], vc_ref[g], tpos, nc, scale) for g in heads]
    imps = []
    for g in heads:
        p_c = comp[g][0]
        psum = p_c[0:tq]
        for r in range(1, GRP):
            psum = psum + p_c[r * tq:(r + 1) * tq]
        imps.append(_dot_exact_rhs(psum, c2s_ref[...]))

    def all_valid():
        v = jnp.where(blk <= tpos_tok // L_SEL, 1.0, 0.0)
        return tuple(v for _ in heads)

    sels = lax.cond(t0 + tq <= n_top * L_SEL, all_valid,
                    lambda: tuple(_select_blocks(imps[g], tpos_tok, ns) for g in heads))

    q_augs, q_earlys = [], []
    for g in heads:
        sel_bias = (sels[g] - 1.0) * (-NEG)
        early_bias = jnp.where(blk < lax.div(t0, L_SEL), sel_bias, NEG)
        q_augs.append(jnp.concatenate([qbs[g], jnp.concatenate([sel_bias.astype(BF16)] * GRP, axis=0)], axis=1))
        q_earlys.append(jnp.concatenate([qbs[g], jnp.concatenate([early_bias.astype(BF16)] * GRP, axis=0)], axis=1))

    def keys_aug(k0, width, g):
        return jnp.concatenate([ks_ref[pl.ds(k0, width), hd(g)].astype(BF16), koh_ref[pl.ds(k0, width), :]], axis=1)

    kpos_d = t0 + lax.broadcasted_iota(jnp.int32, (1, tq), 1)
    s_diags = [jnp.where(kpos_d <= tpos, _dot_nt(q_augs[g], keys_aug(t0, tq, g)), NEG) for g in heads]
    n_early = lax.div(t0 + tk - 1, tk)
    for g in heads:
        mx_ref[g] = functools.reduce(jnp.maximum, [s_diags[g][:, c * LANES:(c + 1) * LANES]
                                                   for c in range(tq // LANES)])

    def pass1(kt, carry):
        k0 = pl.multiple_of(kt * tk, tk)
        ss = [_dot_nt(q_earlys[g], keys_aug(k0, tk, g)) for g in heads]
        old = [mx_ref[g] for g in heads]
        for g in heads:
            mx = ss[g][:, 0:LANES]
            for c in range(1, tk // LANES):
                mx = jnp.maximum(mx, ss[g][:, c * LANES:(c + 1) * LANES])
            sc_ref[g, kt] = ss[g]
            mx_ref[g] = jnp.maximum(old[g], mx)
        return carry

    lax.fori_loop(0, n_early, pass1, 0)
    m2s = [jnp.broadcast_to(jnp.max(mx_ref[g], axis=-1, keepdims=True) * c1, (R, LANES)) for g in heads]
    e_ds = [jnp.concatenate([jnp.exp2(s_diags[g][:, c * LANES:(c + 1) * LANES] * c1 - m2s[g])
                             for c in range(tq // LANES)], axis=1) for g in heads]
    pv_ds = [_dot(e_ds[g].astype(BF16), vs_ref[pl.ds(t0, tq), hd(g)].astype(BF16)) for g in heads]
    for g in heads:
        mx_ref[g] = m2s[g]
        l_ref[g] = sum(e_ds[g][:, c * LANES:(c + 1) * LANES] for c in range(tq // LANES))
        acc_ref[g] = pv_ds[g]

    def pass2(kt, carry):
        k0 = pl.multiple_of(kt * tk, tk)
        ss = [sc_ref[g, kt] for g in heads]
        m2 = [mx_ref[g] for g in heads]
        l_old = [l_ref[g] for g in heads]
        acc_old = [acc_ref[g] for g in heads]
        ebs, lsums = [], []
        for g in heads:
            es = [jnp.exp2(ss[g][:, c * LANES:(c + 1) * LANES] * c1 - m2[g]) for c in range(tk // LANES)]
            lsum = es[0]
            for ec in es[1:]:
                lsum = lsum + ec
            lsums.append(lsum)
            ebs.append(jnp.concatenate(es, axis=1).astype(BF16))
        pvs = [_dot(ebs[g], vs_ref[pl.ds(k0, tk), hd(g)].astype(BF16)) for g in heads]
        for g in heads:
            l_ref[g] = l_old[g] + lsums[g]
            acc_ref[g] = acc_old[g] + pvs[g]
        return carry

    lax.fori_loop(0, n_early, pass2, 0)

    outs = []
    for g in heads:
        o_s = acc_ref[g] / jnp.sum(l_ref[g], axis=-1, keepdims=True)
        gm = _sigmoid(misc_ref[:, g * LANES:g * LANES + GRP * 3])

        def gate(j):
            return jnp.concatenate([gm[:, r * 3 + j:r * 3 + j + 1] for r in range(GRP)], axis=0)

        outs.append(gate(0) * comp[g][1] + gate(1) * o_s + gate(2) * o_ws[g])
    for g in heads:
        for r in range(GRP):
            c = (g * GRP + r) * DH
            o_ref[:, c:c + DH] = (outs[g][r * tq:(r + 1) * tq] * _silu(z_ref[:, c:c + DH])).astype(BF16)


def _nsa_prompt(P, kcv, c2s, koh, B, T, tq, tk):
    assert T % tq == 0 and tq & (tq - 1) == 0 and tk % tq == 0 and T % tk == 0 and WINDOW % tq == 0
    nq = T // tq
    qw = H_A * DH
    kw = HKV * DH
    R = GRP * tq

    def kvspec(slot):
        return pl.BlockSpec((T, kw), lambda b, i: (b, A_KV // kw + slot))

    return pl.pallas_call(
        functools.partial(_nsa_prompt_body, tq=tq, tk=tk, T=T),
        grid=(B, nq),
        in_specs=[pl.BlockSpec((tq, qw), lambda b, i: (b * nq + i, A_Q // qw)),
                  pl.BlockSpec((None, None, HKV, NCP, DH), lambda b, i: (b, 0, 0, 0, 0)),
                  pl.BlockSpec((None, None, HKV, NCP, DH), lambda b, i: (b, 1, 0, 0, 0)),
                  kvspec(2), kvspec(3), kvspec(4), kvspec(5),
                  pl.BlockSpec((tq, HKV * LANES), lambda b, i: (b * nq + i, A_MISC // (HKV * LANES))),
                  pl.BlockSpec((tq, qw), lambda b, i: (b * nq + i, A_ZA // qw)),
                  pl.BlockSpec((NCP, NSP), lambda b, i: (0, 0)),
                  pl.BlockSpec((T, NSP), lambda b, i: (0, 0))],
        out_specs=pl.BlockSpec((tq, qw), lambda b, i: (b * nq + i, 0)),
        out_shape=jax.ShapeDtypeStruct((B * T, qw), BF16),
        scratch_shapes=[pltpu.VMEM((HKV, T // tk, R, tk), F32), pltpu.VMEM((HKV, R, LANES), F32),
                        pltpu.VMEM((HKV, R, LANES), F32), pltpu.VMEM((HKV, R, DH), F32)],
        compiler_params=_params(("arbitrary", "arbitrary")),
    )(P, kcv, kcv, P, P, P, P, P, P, c2s, koh)


def _pad_rows(x, rows):
    return jnp.concatenate([x, jnp.zeros((rows - x.shape[0], x.shape[1]), x.dtype)], axis=0)


def _nsa_sample_body(pt_ref, cache_ref, win_ref, q_ref, kvr_ref, kvw_ref, misc_ref, z_ref, w1_ref, cb_ref, w2_ref,
                     c2s_ref, koh_ref, o_ref, wout_ref, pbuf, sem, *, n_pages, page, past_len, dec_seq, wb):
    b = pl.program_id(0)
    scale = DH ** -0.5
    c1 = scale * LOG2E
    L = past_len + dec_seq
    ns = -(-L // L_SEL)
    nc = L // D_CMP - CMP_R + 1
    R = GRP * TP
    per_page = page // D_CMP
    nch = N_KV_SLOTS * HKV
    grp_rows = D_CMP * nch
    pitch = grp_rows + 1

    def page_copies(seq, slot):
        cps = []
        for j in range(n_pages):
            pid = pt_ref[seq * n_pages + j]
            for n in range(per_page):
                cps.append(pltpu.make_async_copy(
                    cache_ref.at[pid, pl.ds(n * grp_rows, grp_rows), :],
                    pbuf.at[slot, pl.ds((j * per_page + n) * pitch, grp_rows), :], sem.at[slot]))
        return cps

    slot = lax.rem(b, 2)

    @pl.when(b == 0)
    def _():
        for cp in page_copies(0, 0):
            cp.start()

    for cp in page_copies(b, slot):
        cp.wait()

    last = pl.num_programs(0) - 1
    next_copies = page_copies(jnp.minimum(b + 1, last), 1 - slot)
    copies_per_group = len(next_copies) // (2 * (D_CMP // 2))

    def gather(p, ch):
        return pbuf[slot, pl.ds(p * nch + ch, NCP, stride=pitch), :]

    trow = lax.broadcasted_iota(jnp.int32, (R, 1), 0) & (TP - 1)
    tpos = past_len + trow
    tpos_tok = past_len + lax.broadcasted_iota(jnp.int32, (TP, 1), 0)
    lane = lax.broadcasted_iota(jnp.int32, (1, LANES), 1)
    new_pos = past_len + lane
    new_ok = (new_pos <= tpos) & (new_pos < L)
    new_blk = (past_len + lax.broadcasted_iota(jnp.int32, (LANES, 1), 0)) // L_SEL
    new_oh = jnp.where(lax.broadcasted_iota(jnp.int32, (LANES, NSP), 1) == new_blk, 1.0, 0.0).astype(BF16)
    wpos = (past_len - wb) + lax.broadcasted_iota(jnp.int32, (1, wb), 1)
    wch = 2 * HKV

    heads = range(HKV)
    n_iter = D_CMP // 2
    qbs = [jnp.concatenate([q_ref[:, (g * GRP + r) * DH:(g * GRP + r + 1) * DH] for r in range(GRP)],
                           axis=0).astype(BF16) for g in heads]
    raw = [[] for _ in heads]
    win = {}

    def stage_window_scores():
        es = []
        for g in heads:
            s_w = _dot_nt(qbs[g], win_ref[pl.ds(g, wb, stride=wch), :].astype(BF16))
            s_w = jnp.where((wpos <= tpos) & (wpos > tpos - WINDOW), s_w, NEG)
            s_n = _dot_nt(qbs[g], _pad_rows(kvw_ref[:, g * DH:(g + 1) * DH], LANES).astype(BF16))
            s_n = jnp.where(new_ok & (new_pos > tpos - WINDOW), s_n, NEG)
            s = jnp.concatenate([s_w, s_n], axis=1)
            es.append(jnp.exp2((s - jnp.max(s, axis=-1, keepdims=True)) * c1))
        win["e"] = es

    def stage_window_out():
        outs_w = []
        for g in heads:
            e = win["e"][g]
            o_w = (_dot(e[:, :wb].astype(BF16), win_ref[pl.ds(HKV + g, wb, stride=wch), :].astype(BF16))
                   + _dot(e[:, wb:].astype(BF16),
                          _pad_rows(kvw_ref[:, (HKV + g) * DH:(HKV + g + 1) * DH], LANES).astype(BF16)))
            outs_w.append(o_w / jnp.sum(e, axis=-1, keepdims=True))
        win["o"] = outs_w

    def stage_raw_scores(p_lo, p_hi):
        for p in range(p_lo, p_hi):
            for g in heads:
                raw[g].append(_dot_nt(qbs[g], gather(p, 2 * HKV + g).astype(BF16)))

    def stage_raw_new():
        for g in heads:
            ksel_col = (2 * HKV + g) * DH
            raw[g].append(_dot_nt(qbs[g], _pad_rows(kvr_ref[:, ksel_col:ksel_col + DH], LANES).astype(BF16)))


    def compress_step(s, pp, u0, u1):
        xs = [jnp.concatenate([gather(2 * pp, s * HKV + g), gather(2 * pp + 1, s * HKV + g)], axis=1) for g in heads]
        xp = jnp.concatenate(xs, axis=0).astype(BF16)
        u0 = u0 + _dot(xp, w1_ref[s, pl.ds(pp * 2 * DH, 2 * DH), :])
        u1 = u1 + _dot(xp, w1_ref[s, pl.ds(D_CMP * DH + pp * 2 * DH, 2 * DH), :])
        grp = s * n_iter + pp
        for cp in next_copies[grp * copies_per_group:(grp + 1) * copies_per_group]:
            cp.start()
        return u0, u1

    def compress_out(s, u0, u1):
        return [_dot(_gelu_tanh(_compress_hidden(u0[g * NCP:(g + 1) * NCP], u1[g * NCP:(g + 1) * NCP],
                                                 cb_ref[s])).astype(BF16), w2_ref[s]) for g in heads]

    quarter = D_CMP // 4
    early = {0: stage_window_scores, 1: lambda: stage_raw_scores(0, quarter),
             2: lambda: stage_raw_scores(quarter, 2 * quarter), 3: stage_window_out,
             4: lambda: stage_raw_scores(2 * quarter, 3 * quarter),
             5: lambda: stage_raw_scores(3 * quarter, D_CMP), 6: stage_raw_new}
    u0 = u1 = jnp.zeros((HKV * NCP, CMP_HID), F32)
    for pp in range(n_iter):
        u0, u1 = compress_step(0, pp, u0, u1)
        if pp in early:
            early[pp]()
    kc = compress_out(0, u0, u1)

    st = {}

    def stage_probs():
        st["p_c"] = [_compressed_probs(qbs[g], kc[g], tpos, nc, scale) for g in heads]

    def stage_select():
        imps = []
        for g in heads:
            p_c = st["p_c"][g]
            psum = p_c[0:TP]
            for r in range(1, GRP):
                psum = psum + p_c[r * TP:(r + 1) * TP]
            imps.append(_dot_exact_rhs(psum, c2s_ref[...]))
        st["sel"] = [_select_blocks(imps[g], tpos_tok, ns) for g in heads]

    def stage_scores():
        biases = [jnp.concatenate([((st["sel"][g] - 1.0) * (-NEG)).astype(BF16)] * GRP, axis=0) for g in heads]
        bias_past = [_dot_nt(biases[g], koh_ref[...]) for g in heads]
        bias_new = [jnp.where(new_ok, _dot_nt(biases[g], new_oh), NEG) for g in heads]
        es = []
        for g in heads:
            s = jnp.concatenate([pc + bias_past[g] for pc in raw[g][:D_CMP]] + [raw[g][D_CMP] + bias_new[g]], axis=1)
            es.append(jnp.exp2((s - jnp.max(s, axis=-1, keepdims=True)) * c1))
        st["e"] = es
        st["o_s"] = [_dot(es[g][:, D_CMP * NCP:].astype(BF16),
                          _pad_rows(kvr_ref[:, (3 * HKV + g) * DH:(3 * HKV + g + 1) * DH], LANES).astype(BF16))
                     for g in heads]

    def stage_pv(p_lo, p_hi):
        for p in range(p_lo, p_hi):
            for g in heads:
                st["o_s"][g] = st["o_s"][g] + _dot(st["e"][g][:, p * NCP:(p + 1) * NCP].astype(BF16),
                                                  gather(p, 3 * HKV + g).astype(BF16))

    after = {0: stage_probs, 3: stage_select, 6: stage_scores, 7: lambda: stage_pv(0, D_CMP // 2)}
    u0 = u1 = jnp.zeros((HKV * NCP, CMP_HID), F32)
    for pp in range(n_iter):
        u0, u1 = compress_step(1, pp, u0, u1)
        if pp in after:
            after[pp]()
    vc = compress_out(1, u0, u1)
    stage_pv(D_CMP // 2, D_CMP)

    outs = []
    for g in heads:
        o_c = _dot(st["p_c"][g].astype(BF16), vc[g].astype(BF16))
        o_s = st["o_s"][g] / jnp.sum(st["e"][g], axis=-1, keepdims=True)
        gm = _sigmoid(misc_ref[:, g * LANES:g * LANES + GRP * 3])

        def gate(j):
            return jnp.concatenate([gm[:, r * 3 + j:r * 3 + j + 1] for r in range(GRP)], axis=0)

        outs.append(gate(0) * o_c + gate(1) * o_s + gate(2) * win["o"][g])

    for g in range(HKV):
        for r in range(GRP):
            c = (g * GRP + r) * DH
            o_ref[:, c:c + DH] = outs[g][r * TP:(r + 1) * TP] * _silu(z_ref[:, c:c + DH])

    wout_ref[pl.ds(0, (wb - dec_seq) * wch), :] = win_ref[pl.ds(dec_seq * wch, (wb - dec_seq) * wch), :]
    for t in range(dec_seq):
        for ch in range(wch):
            wout_ref[pl.ds((wb - dec_seq + t) * wch + ch, 1), :] = kvw_ref[t:t + 1, ch * DH:(ch + 1) * DH]

    @pl.when(b == last)
    def _():
        for cp in next_copies:
            cp.wait()


def _nsa_sample(Ps, cache, cache_win, page_table, w1, cb, w2, c2s, dec_seq):
    NB, n_pages = page_table.shape
    n_pool, prow, _ = cache.shape
    page = prow // (N_KV_SLOTS * HKV)
    wb = cache_win.shape[1] // (2 * HKV)
    past_len = n_pages * page
    L = past_len + dec_seq
    assert wb == WINDOW and (L // D_CMP) == NCP and NCP * D_CMP == past_len and dec_seq <= TP
    assert -(-L // L_SEL) < NSP and page % D_CMP == 0 and (dec_seq * 2 * HKV) % SUBLANES == 0
    qw = H_A * DH
    nch = N_KV_SLOTS * HKV
    koh = jnp.asarray(np.arange(NCP)[:, None] * D_CMP // L_SEL == np.arange(NSP)[None, :], dtype=BF16)

    grid_spec = pltpu.PrefetchScalarGridSpec(
        num_scalar_prefetch=1,
        grid=(NB,),
        in_specs=[
            pl.BlockSpec(memory_space=pl.ANY),
            pl.BlockSpec((None, wb * 2 * HKV, DH), lambda b, pt: (b, 0, 0)),
            pl.BlockSpec((TP, qw), lambda b, pt: (b, A_Q // qw)),
            pl.BlockSpec((TP, 4 * HKV * DH), lambda b, pt: (b, A_KV // (4 * HKV * DH))),
            pl.BlockSpec((TP, 2 * HKV * DH), lambda b, pt: (b, (A_KV + 4 * HKV * DH) // (2 * HKV * DH))),
            pl.BlockSpec((TP, HKV * LANES), lambda b, pt: (b, A_MISC // (HKV * LANES))),
            pl.BlockSpec((TP, qw), lambda b, pt: (b, A_ZA // qw)),
            pl.BlockSpec((2, L_CMP * DH, CMP_HID), lambda b, pt: (0, 0, 0)),
            pl.BlockSpec((2, 1, CMP_HID), lambda b, pt: (0, 0, 0)),
            pl.BlockSpec((2, CMP_HID, DH), lambda b, pt: (0, 0, 0)),
            pl.BlockSpec((NCP, NSP), lambda b, pt: (0, 0)),
            pl.BlockSpec((NCP, NSP), lambda b, pt: (0, 0))],
        out_specs=[pl.BlockSpec((TP, qw), lambda b, pt: (b, 0)),
                   pl.BlockSpec((None, wb * 2 * HKV, DH), lambda b, pt: (b, 0, 0))],
        scratch_shapes=[pltpu.VMEM((2, NCP * (D_CMP * nch + 1), DH), F32), pltpu.SemaphoreType.DMA((2,))],
    )
    return pl.pallas_call(
        functools.partial(_nsa_sample_body, n_pages=n_pages, page=page, past_len=past_len, dec_seq=dec_seq, wb=wb),
        grid_spec=grid_spec,
        out_shape=[jax.ShapeDtypeStruct((NB * TP, qw), F32),
                   jax.ShapeDtypeStruct((NB, wb * 2 * HKV, DH), F32)],
        compiler_params=_params(("arbitrary",)),
    )(page_table.reshape(-1), cache, cache_win, Ps, Ps, Ps, Ps, Ps, w1, cb, w2, c2s, koh)


def _rec_chunk(q, k, v, la, gate, S, C, last_row, nh, K, V):
    sb = min(16, C)
    ri = lax.broadcasted_iota(jnp.int32, (C, C), 0)
    cj = lax.broadcasted_iota(jnp.int32, (C, C), 1)
    causal = cj <= ri
    tri = jnp.where(causal, 1.0, 0.0).astype(BF16)
    b = _dot_exact_lhs(tri, la)
    qe = (q * jnp.exp(b)).astype(BF16)
    qis, kis = [], []
    k_ref_prev, c_prev = None, None
    for i in range(C // sb):
        r0 = i * sb
        ci = b[r0 + sb // 2:r0 + sb // 2 + 1, :]
        qis.append((q[r0:r0 + sb] * jnp.exp(b[r0:r0 + sb] - ci)).astype(BF16))
        k_new = k[r0:r0 + sb] * jnp.exp(jnp.minimum(ci - b[r0:r0 + sb], 80.0))
        k_scaled = k_new if i == 0 else jnp.concatenate([k_ref_prev * jnp.exp(ci - c_prev), k_new], axis=0)
        k_ref_prev, c_prev = k_scaled, ci
        rest = C - r0 - sb
        kis.append((k_scaled if rest == 0 else
                    jnp.concatenate([k_scaled, jnp.zeros((rest, k.shape[1]), F32)], axis=0)).astype(BF16))
    b_last = b[last_row:last_row + 1, :]
    rowi = lax.broadcasted_iota(jnp.int32, (C, 1), 0)
    kd = jnp.where(rowi <= last_row, k * jnp.exp(jnp.minimum(b_last - b, 0.0)), 0.0)
    stack = jnp.concatenate([kd, jnp.broadcast_to(b_last, (SUBLANES, nh * K)),
                             jnp.zeros((LANES - C - SUBLANES, nh * K), F32)], axis=0)
    vb = v.astype(BF16)
    v_pad = jnp.concatenate([vb, jnp.zeros((LANES - C, nh * V), BF16)], axis=0)
    outs, s_new = [], []
    for h in range(nh):
        ks = slice(h * K, (h + 1) * K)
        vs = slice(h * V, (h + 1) * V)
        o = _dot(qe[:, ks], S[h].astype(BF16))
        rows = [_dot_nt(qi[:, ks], ki[:, ks]) for qi, ki in zip(qis, kis)]
        att = rows[0] if len(rows) == 1 else jnp.concatenate(rows, axis=0)
        att = jnp.where(causal, att, 0.0)
        o = o + _dot(att.astype(BF16), vb[:, vs])
        stack_t = stack[:, ks].T
        a_col = jnp.exp(stack_t[:, C:C + 1])
        s_new.append(a_col * S[h] + _dot(stack_t.astype(BF16), v_pad[:, vs]))
        ms = jnp.mean(o * o, axis=-1, keepdims=True)
        outs.append(o * lax.rsqrt(ms + EPS) * gate[:, vs])
    return outs, s_new


def _rec_body(*refs, variant, nh, K, V, C, TB, SB, last_row, has_s0, layer_idx):
    refs = list(refs)
    q_ref, k_ref, v_ref, z_ref = refs[:4]
    pos = 4
    if variant == "gla":
        misc_ref, w2_ref, gb_ref = refs[pos:pos + 3]
        pos += 3
    else:
        lb_ref = refs[pos]
        pos += 1
    gn_ref = refs[pos]
    pos += 1
    if has_s0:
        s0_ref = refs[pos]
        pos += 1
    o_ref, s_ref = refs[pos:pos + 2]
    n_chunks = TB // C
    single = n_chunks == 1 and has_s0

    if not single:
        @pl.when(pl.program_id(2) == 0)
        def _():
            if has_s0:
                s_ref[...] = s0_ref[...]
            else:
                s_ref[...] = jnp.zeros(s_ref.shape, F32)

    if variant == "hgrn":
        lg = lb_ref[...]
        e = jnp.exp(lg - jnp.max(lg, axis=0, keepdims=True))
        prob = e / jnp.sum(e, axis=0, keepdims=True)
        lb = prob[1:2]
        for i in range(2, layer_idx + 1):
            lb = lb + prob[i:i + 1]
        if SB > 1:
            lb = jnp.concatenate([lb] * SB, axis=1)
    gn_all = jnp.concatenate([gn_ref[...]] * (nh * SB), axis=1)

    def chunk(c0):
        def rows(s):
            return pl.ds(s * TB + c0, C)

        def cat(f):
            return f(0) if SB == 1 else jnp.concatenate([f(s) for s in range(SB)], axis=1)

        qr = cat(lambda s: q_ref[rows(s), :])
        kr = cat(lambda s: k_ref[rows(s), :])
        v = cat(lambda s: v_ref[rows(s), :])
        gate = _silu(cat(lambda s: z_ref[rows(s), :])) * gn_all
        if variant == "gla":
            w2 = w2_ref[...].astype(BF16)
            zg = cat(lambda s: _dot(misc_ref[rows(s), :].astype(BF16), w2) + gb_ref[...])
            q, k, la = qr * (K ** -0.5), kr, _log_sigmoid(zg) / GLA_GATE_NORM
        else:
            t = jnp.exp(-jnp.abs(kr))
            r = 1.0 / (1.0 + t)
            tr = t * r
            nonneg = kr >= 0.0
            sig = jnp.where(nonneg, r, tr)
            nsig = jnp.where(nonneg, tr, r)
            q, k, la = _silu(qr), (1.0 - lb) * nsig, jnp.log(lb + (1.0 - lb) * sig)
        src = s0_ref if single else s_ref
        states = [src[s, h] for s in range(SB) for h in range(nh)]
        outs, s_new = _rec_chunk(q, k, v, la, gate, states, C, last_row, SB * nh, K, V)
        for s in range(SB):
            for h in range(nh):
                o_ref[rows(s), h * V:(h + 1) * V] = outs[s * nh + h].astype(o_ref.dtype)
                s_ref[s, h] = s_new[s * nh + h]

    if n_chunks == 1:
        chunk(0)
    else:
        def body(ci, carry):
            chunk(pl.multiple_of(ci * C, C))
            return carry

        lax.fori_loop(0, n_chunks, body, 0, unroll=8)


def _recurrence(P, cols, extra, gn, s0, *, variant, B, T, H, K, V, C, TB, nh, last_row, out_dtype, SB=1,
                layer_idx=1):
    cq, ck, cv, cz = cols
    assert H % nh == 0 and T % TB == 0 and TB % C == 0 and B % SB == 0
    nt = T // TB
    assert SB == 1 or nt == 1

    def colspec(c0, w):
        assert c0 % (nh * w) == 0
        return pl.BlockSpec((SB * TB, nh * w), lambda b, j, t: (b * nt + t, c0 // (nh * w) + j))

    in_specs = [colspec(cq, K), colspec(ck, K), colspec(cv, V), colspec(cz, V)]
    args = [P, P, P, P]
    if variant == "gla":
        w2p, gb = extra
        in_specs += [pl.BlockSpec((SB * TB, LANES), lambda b, j, t: (b * nt + t, A_MISC // LANES)),
                     pl.BlockSpec((LANES, nh * K), lambda b, j, t: (0, j)),
                     pl.BlockSpec((1, nh * K), lambda b, j, t: (0, j))]
        args += [P, w2p, gb]
    else:
        (lb_logits,) = extra
        in_specs += [pl.BlockSpec((lb_logits.shape[0], nh * K), lambda b, j, t: (0, j))]
        args += [lb_logits]
    in_specs += [pl.BlockSpec((1, V), lambda b, j, t: (0, 0))]
    args += [gn]
    st_spec = pl.BlockSpec((None, SB, nh, K, V), lambda b, j, t: (0, b, j, 0, 0))
    if s0 is not None:
        in_specs += [st_spec]
        args += [s0]
    return pl.pallas_call(
        functools.partial(_rec_body, variant=variant, nh=nh, K=K, V=V, C=C, TB=TB, SB=SB, last_row=last_row,
                          has_s0=s0 is not None, layer_idx=layer_idx),
        grid=(B // SB, H // nh, nt),
        in_specs=in_specs,
        out_specs=[pl.BlockSpec((SB * TB, nh * V), lambda b, j, t: (b * nt + t, j)), st_spec],
        out_shape=[jax.ShapeDtypeStruct((B * T, H * V), out_dtype),
                   jax.ShapeDtypeStruct((1, B, H, K, V), F32)],
        compiler_params=_params(("arbitrary", "arbitrary", "arbitrary")),
    )(*args)


def _layout_a_w_in_body(w_ref, o_ref):
    off = [int(v) for v in np.concatenate([[0], np.cumsum(A_SIZES)])]
    q, kv, gbr, za, qb, kb, vb, lr, zb = [w_ref[off[i]:off[i + 1], :] for i in range(len(A_SIZES))]
    cols = w_ref.shape[1]
    z = lambda n: jnp.zeros((n, cols), F32)
    misc0 = jnp.concatenate([gbr[:GRP * 3], lr, z(LANES - GRP * 3 - GLA_LR)], axis=0)
    misc1 = jnp.concatenate([gbr[GRP * 3:], z(LANES - GRP * 3)], axis=0)
    out = jnp.concatenate([q, vb, zb, za, qb, kb, kv, misc0, misc1, z(NA - A_MISC - HKV * LANES)], axis=0)
    o_ref[...] = out.astype(BF16)


def _layout_a_w_in(wt, tc=256):
    n, D = wt.shape
    assert D % tc == 0 and n == sum(A_SIZES)
    return pl.pallas_call(
        _layout_a_w_in_body,
        grid=(D // tc,),
        in_specs=[pl.BlockSpec((n, tc), lambda i: (0, i))],
        out_specs=pl.BlockSpec((NA, tc), lambda i: (0, i)),
        out_shape=jax.ShapeDtypeStruct((NA, D), BF16),
        compiler_params=_params(("arbitrary",)),
    )(wt)


def _cmp_to_sel():
    c_start = np.arange(NCP)[:, None] * D_CMP
    s_start = np.arange(NSP)[None, :] * L_SEL
    overlap = np.clip(np.minimum(c_start + L_CMP, s_start + L_SEL) - np.maximum(c_start, s_start), 0, None)
    return jnp.asarray(overlap / D_CMP, dtype=BF16)


class _Tiles(NamedTuple):
    proj_rows: int
    proj_cols: int
    out_rows: int
    nsa_q: int
    nsa_k: int
    rec_chunk: int
    rec_block: int
    gla_heads: int
    hgrn_heads: int
    gla_seqs: int
    hgrn_seqs: int


def _tiles(n_decode):
    return _Tiles(proj_rows=1024, proj_cols=N_KV_SLOTS * HKV * DH, out_rows=512, nsa_q=256, nsa_k=512,
                  rec_chunk=64, rec_block=512, gla_heads=H_B, hgrn_heads=8,
                  gla_seqs=math.gcd(n_decode, 16), hgrn_seqs=math.gcd(n_decode, 8))


def kernel(x_prompt, x_sample, cache_kv, cache_win, state_gla, state_hgrn, page_table, a_norm, a_w_in, a_gla_w2,
           a_gla_b, a_gla_gn, a_cmp_pe, a_cmp_w1, a_cmp_b1, a_cmp_w2, a_w_out, c_norm, c_w_in, c_lb_logits, c_gn,
           c_w_out, final_norm):
    B, T, D = x_prompt.shape
    NB, TS, _ = x_sample.shape
    n_pool, page = cache_kv.shape[1], cache_kv.shape[2]
    wb = cache_win.shape[2]
    assert a_norm.shape[0] == 1 and c_norm.shape[0] == 1 and c_lb_logits.shape[0] == 2

    wa = _layout_a_w_in(a_w_in[0].T)
    wc = c_w_in[0].astype(BF16)
    wao = a_w_out[0].astype(BF16)
    wco = c_w_out[0].astype(BF16)
    w1 = a_cmp_w1[0].astype(BF16)
    w2 = a_cmp_w2[0].astype(BF16)
    pe = a_cmp_pe[0].reshape(2, 1, L_CMP * DH)
    b1 = a_cmp_b1[0].reshape(2, 1, CMP_HID)
    w2p = jnp.zeros((LANES, H_B * DK_B), F32).at[MISC_LR:MISC_LR + GLA_LR, :].set(a_gla_w2[0])
    gb = a_gla_b[0].reshape(1, H_B * DK_B)
    c2s = _cmp_to_sel()
    a_nw = a_norm[0].reshape(1, D)
    c_nw = c_norm[0].reshape(1, D)
    f_nw = final_norm.reshape(1, D)
    gla_gn = a_gla_gn[0].reshape(1, DV_B)
    hg_gn = c_gn[0].reshape(1, DV_C)

    xp = x_prompt.reshape(B * T, D)
    xs = jnp.pad(x_sample, ((0, 0), (0, TP - TS), (0, 0))).reshape(NB * TP, D)

    cb = _cmp_bias(pe, w1, b1)

    tl = _tiles(NB)

    Pp, kv_rows_p = _norm_matmul(xp, a_nw, wa, tm=tl.proj_rows, tn=tl.proj_cols, copy_tile=A_KV // tl.proj_cols,
                                 w_rows=True)
    kcv = _compress_prompt(Pp, w1, cb, w2, B, T)
    koh = jnp.asarray(np.arange(T)[:, None] // L_SEL == np.arange(NSP)[None, :], dtype=BF16)
    oa_p = _nsa_prompt(Pp, kcv, c2s, koh, B, T, tq=tl.nsa_q, tk=tl.nsa_k)
    ob_p, gla_p = _recurrence(Pp, (A_QB, A_KB, A_VB, A_ZB), (w2p, gb), gla_gn, None, variant="gla", B=B, T=T,
                              H=H_B, K=DK_B, V=DV_B, C=tl.rec_chunk, TB=tl.rec_block, nh=tl.gla_heads,
                              last_row=tl.rec_chunk - 1, out_dtype=BF16)
    y1p = _out_proj(oa_p, 0, ob_p, 0, wao, xp, a_nw, tm=tl.out_rows, final_norm=False)

    Ps = _norm_matmul(xs, a_nw, wa, tm=NB * TP, tn=tl.proj_cols, w_rows=True)
    cache2 = cache_kv[0].reshape(n_pool, page * N_KV_SLOTS * HKV, DH)
    win2 = cache_win[0].reshape(NB, wb * 2 * HKV, DH)
    oa_s, win_out = _nsa_sample(Ps, cache2, win2, page_table, w1, cb, w2, c2s, TS)
    ob_s, gla_s = _recurrence(Ps, (A_QB, A_KB, A_VB, A_ZB), (w2p, gb), gla_gn, state_gla, variant="gla", B=NB, T=TP,
                              H=H_B, K=DK_B, V=DV_B, C=TP, TB=TP, nh=H_B, last_row=TS - 1, out_dtype=F32,
                              SB=tl.gla_seqs)
    y1s = _out_proj(oa_s, 0, ob_s, 0, wao, xs, a_nw, tm=min(tl.out_rows, NB * TP), final_norm=False)

    Pc = _norm_matmul(y1p, c_nw, wc, tm=tl.proj_rows, tn=tl.proj_cols)
    hc = H_C * DK_C
    oc_p, hg_p = _recurrence(Pc, (0, hc, 2 * hc, 3 * hc), (c_lb_logits,), hg_gn, None, variant="hgrn", B=B, T=T,
                             H=H_C, K=DK_C, V=DV_C, C=tl.rec_chunk, TB=tl.rec_block, nh=tl.hgrn_heads,
                             last_row=tl.rec_chunk - 1, out_dtype=BF16)
    y_prompt = _out_proj(oc_p, 0, oc_p, 1, wco, y1p, f_nw, tm=tl.out_rows, final_norm=True)

    Pcs = _norm_matmul(y1s, c_nw, wc, tm=NB * TP, tn=tl.proj_cols)
    oc_s, hg_s = _recurrence(Pcs, (0, hc, 2 * hc, 3 * hc), (c_lb_logits,), hg_gn, state_hgrn, variant="hgrn", B=NB,
                             T=TP, H=H_C, K=DK_C, V=DV_C, C=TP, TB=TP, nh=H_C, last_row=TS - 1, out_dtype=F32,
                             SB=tl.hgrn_seqs)
    y_sample = _out_proj(oc_s, 0, oc_s, 1, wco, y1s, f_nw, tm=min(tl.out_rows, NB * TP), final_norm=True)

    kvw = 6 * HKV * DH
    nrw = N_KV_SLOTS * HKV * DH
    Pp3 = Pp.reshape(B, T, NA)
    Ps3 = Ps.reshape(NB, TP, NA)
    kv_p = kv_rows_p.reshape(1, B, T, N_KV_SLOTS, HKV, DH)
    kv_s = Ps3[:, :TS, A_KV:A_KV + nrw].reshape(1, NB, TS, N_KV_SLOTS, HKV, DH)
    keep = min(WINDOW, T)
    win_p = Pp3[:, T - keep:, A_KV + nrw:A_KV + kvw].reshape(1, B, keep, 2, HKV, DH)
    win_s = win_out.reshape(1, NB, wb, 2, HKV, DH)
    return (y_prompt.reshape(B, T, D), y_sample.reshape(NB, TP, D)[:, :TS], kv_p, kv_s, win_p, win_s,
            gla_p, gla_s, hg_p, hg_s)
```

```python
import functools
import math
from typing import NamedTuple

import numpy as np
import jax
import jax.numpy as jnp
from jax import lax
from jax.experimental import pallas as pl
from jax.experimental.pallas import tpu as pltpu

F32 = jnp.float32
BF16 = jnp.bfloat16

D_MODEL = 2048
DH = 128
H_A = 8
HKV = 2
GRP = H_A // HKV
L_CMP = 32
D_CMP = 16
CMP_R = L_CMP // D_CMP
CMP_HID = 2 * DH
L_SEL = 64
N_TOP = 16
WINDOW = 512
N_KV_SLOTS = 4
H_B = 4
DK_B = 128
DV_B = 256
GLA_LR = 16
GLA_GATE_NORM = 16.0
DK_C = 128
H_C = 16
DV_C = 128
EPS = 1e-6
FORCE_SCORE = 1e9
A_SIZES = (H_A * DH, 6 * HKV * DH, 3 * H_A, H_A * DH, H_B * DK_B, H_B * DK_B, H_B * DV_B, GLA_LR, H_B * DV_B)

LANES = 128
SUBLANES = 8
VMEM_LIMIT = 56 * 1024 * 1024

A_Q = 0
A_VB = 1024
A_ZB = 2048
A_ZA = 3072
A_QB = 4096
A_KB = 4608
A_KV = 5120
A_MISC = 6656
NA = 7168
MISC_LR = GRP * 3

NCP = 128
NSP = 128
TP = 8
NEG = -1e30
LOG2E = 1.4426950408889634


def _dot(a, b):
    return jnp.dot(a, b, preferred_element_type=F32)


def _dot_nt(a, b):
    return lax.dot_general(a, b, (((1,), (1,)), ((), ())), preferred_element_type=F32)


def _split3(a):
    hi = a.astype(BF16)
    r = a - hi.astype(F32)
    mid = r.astype(BF16)
    lo = (r - mid.astype(F32)).astype(BF16)
    return hi, mid, lo


def _dot_exact_rhs(a, b_exact):
    hi, mid, lo = _split3(a)
    return _dot(hi, b_exact) + _dot(mid, b_exact) + _dot(lo, b_exact)


def _dot_exact_lhs(a_exact, b):
    hi, mid, lo = _split3(b)
    return _dot(a_exact, hi) + _dot(a_exact, mid) + _dot(a_exact, lo)


def _sigmoid(x):
    return 1.0 / (1.0 + jnp.exp(-x))


def _silu(x):
    h = 0.5 * x
    return h + h * jnp.tanh(h)


def _log_sigmoid(x):
    return jnp.minimum(x, 0.0) - jnp.log(1.0 + jnp.exp(-jnp.abs(x)))


def _gelu_tanh(x):
    return 0.5 * x * (1.0 + jnp.tanh(np.sqrt(2.0 / np.pi) * (x + 0.044715 * (x * x * x))))


def _masked_softmax_pre(sm):
    m = jnp.max(sm, axis=-1, keepdims=True)
    e = jnp.where(sm > 0.5 * NEG, jnp.exp(sm - m), 0.0)
    l = jnp.sum(e, axis=-1, keepdims=True)
    return e / jnp.maximum(l, 1e-30)


def _params(sem):
    return pltpu.CompilerParams(dimension_semantics=sem, vmem_limit_bytes=VMEM_LIMIT)


def _norm_matmul_body(x_ref, nw_ref, w_ref, o_ref, *rest, copy_tile, w_rows):
    h_ref = rest[-1]

    @pl.when(pl.program_id(1) == 0)
    def _():
        x = x_ref[...]
        ms = jnp.mean(x * x, axis=-1, keepdims=True)
        h_ref[...] = (x * lax.rsqrt(ms + EPS) * nw_ref[...]).astype(BF16)

    res = _dot_nt(h_ref[...], w_ref[...]) if w_rows else _dot(h_ref[...], w_ref[...])
    o_ref[...] = res
    if copy_tile is not None:
        @pl.when(pl.program_id(1) == copy_tile)
        def _():
            n_lt = res.shape[1] // LANES
            for c in range(n_lt):
                rest[0][pl.ds(c, res.shape[0], stride=n_lt), :] = res[:, c * LANES:(c + 1) * LANES]


def _norm_matmul(x, nw, w, tm, tn, copy_tile=None, w_rows=False):
    M, D = x.shape
    N = w.shape[0] if w_rows else w.shape[1]
    assert M % tm == 0 and N % tn == 0
    w_spec = pl.BlockSpec((tn, D), lambda i, j: (j, 0)) if w_rows else pl.BlockSpec((D, tn), lambda i, j: (0, j))
    out_specs = [pl.BlockSpec((tm, tn), lambda i, j: (i, j))]
    out_shape = [jax.ShapeDtypeStruct((M, N), F32)]
    if copy_tile is not None:
        out_specs.append(pl.BlockSpec((tm * (tn // LANES), LANES), lambda i, j: (i, 0)))
        out_shape.append(jax.ShapeDtypeStruct((M * (tn // LANES), LANES), F32))
    outs = pl.pallas_call(
        functools.partial(_norm_matmul_body, copy_tile=copy_tile, w_rows=w_rows),
        grid=(M // tm, N // tn),
        in_specs=[pl.BlockSpec((tm, D), lambda i, j: (i, 0)),
                  pl.BlockSpec((1, D), lambda i, j: (0, 0)),
                  w_spec],
        out_specs=out_specs,
        out_shape=out_shape,
        scratch_shapes=[pltpu.VMEM((tm, D), BF16)],
        compiler_params=_params(("arbitrary", "arbitrary")),
    )(x, nw, w)
    return outs[0] if copy_tile is None else outs


def _out_proj_body(a1_ref, a2_ref, w1_ref, w2_ref, x_ref, nw_ref, y_ref, *, final_norm):
    y = x_ref[...] + _dot(a1_ref[...].astype(BF16), w1_ref[...]) + _dot(a2_ref[...].astype(BF16), w2_ref[...])
    if final_norm:
        ms = jnp.mean(y * y, axis=-1, keepdims=True)
        y = y * lax.rsqrt(ms + EPS) * nw_ref[...]
    y_ref[...] = y


def _out_proj(a1, c1, a2, c2, w, x, nw, tm, final_norm):
    M, D = x.shape
    KH = w.shape[0] // 2
    assert M % tm == 0
    return pl.pallas_call(
        functools.partial(_out_proj_body, final_norm=final_norm),
        grid=(M // tm,),
        in_specs=[pl.BlockSpec((tm, KH), lambda i: (i, c1)),
                  pl.BlockSpec((tm, KH), lambda i: (i, c2)),
                  pl.BlockSpec((KH, D), lambda i: (0, 0)),
                  pl.BlockSpec((KH, D), lambda i: (1, 0)),
                  pl.BlockSpec((tm, D), lambda i: (i, 0)),
                  pl.BlockSpec((1, D), lambda i: (0, 0))],
        out_specs=pl.BlockSpec((tm, D), lambda i: (i, 0)),
        out_shape=jax.ShapeDtypeStruct((M, D), F32),
        compiler_params=_params(("arbitrary",)),
    )(a1, a2, w, w, x, nw)


def _cmp_bias_body(pe_ref, w1_ref, b1_ref, o_ref):
    pe = jnp.broadcast_to(pe_ref[...], (SUBLANES, pe_ref.shape[-1])).astype(BF16)
    o_ref[...] = b1_ref[...] + _dot(pe, w1_ref[...])[0:1, :]


def _cmp_bias(pe, w1, b1):
    S, KF, HID = w1.shape
    return pl.pallas_call(
        _cmp_bias_body,
        grid=(S,),
        in_specs=[pl.BlockSpec((None, 1, KF), lambda s: (s, 0, 0)),
                  pl.BlockSpec((None, KF, HID), lambda s: (s, 0, 0)),
                  pl.BlockSpec((None, 1, HID), lambda s: (s, 0, 0))],
        out_specs=pl.BlockSpec((None, 1, HID), lambda s: (s, 0, 0)),
        out_shape=jax.ShapeDtypeStruct((S, 1, HID), F32),
        compiler_params=_params(("arbitrary",)),
    )(pe, w1, b1)


def _compress_hidden(u0, u1, bias):
    return u0 + pltpu.roll(u1, NCP - 1, 0) + bias


def _compress_prompt_body(x_ref, w1_ref, cb_ref, w2_ref, o_ref):
    u0 = jnp.zeros((NCP, CMP_HID), F32)
    u1 = jnp.zeros((NCP, CMP_HID), F32)
    for pp in range(D_CMP // 2):
        xa = x_ref[pl.ds(2 * pp, NCP, stride=D_CMP), :]
        xb = x_ref[pl.ds(2 * pp + 1, NCP, stride=D_CMP), :]
        xp = jnp.concatenate([xa, xb], axis=1).astype(BF16)
        u0 = u0 + _dot(xp, w1_ref[pl.ds(pp * 2 * DH, 2 * DH), :])
        u1 = u1 + _dot(xp, w1_ref[pl.ds(D_CMP * DH + pp * 2 * DH, 2 * DH), :])
    h = _compress_hidden(u0, u1, cb_ref[...])
    o_ref[...] = _dot(_gelu_tanh(h).astype(BF16), w2_ref[...])


def _compress_prompt(P, w1, cb, w2, B, T):
    assert T // D_CMP == NCP
    kv0 = A_KV // DH
    return pl.pallas_call(
        _compress_prompt_body,
        grid=(B, 2, HKV),
        in_specs=[pl.BlockSpec((T, DH), lambda b, s, g: (b, kv0 + HKV * s + g)),
                  pl.BlockSpec((None, L_CMP * DH, CMP_HID), lambda b, s, g: (s, 0, 0)),
                  pl.BlockSpec((None, 1, CMP_HID), lambda b, s, g: (s, 0, 0)),
                  pl.BlockSpec((None, CMP_HID, DH), lambda b, s, g: (s, 0, 0))],
        out_specs=pl.BlockSpec((None, None, None, NCP, DH), lambda b, s, g: (b, s, g, 0, 0)),
        out_shape=jax.ShapeDtypeStruct((B, 2, HKV, NCP, DH), F32),
        compiler_params=_params(("arbitrary", "arbitrary", "arbitrary")),
    )(P, w1, cb, w2)


def _select_blocks(imp, tpos_tok, ns):
    blk = lax.broadcasted_iota(jnp.int32, (1, NSP), 1)
    cur = tpos_tok // L_SEL
    valid = blk <= cur
    forced = (blk == 0) | (blk == cur) | (blk == cur - 1)
    score = jnp.where(valid, jnp.where(forced, FORCE_SCORE, imp), -jnp.inf)
    k_top = float(min(N_TOP, ns))
    if score.shape[0] % LANES == 0:
        nb = -(-ns // SUBLANES) * SUBLANES
        rblk = lax.broadcasted_iota(jnp.int32, (nb, 1), 0)
        tops = []
        for t in range(score.shape[0] // LANES):
            st = score[t * LANES:(t + 1) * LANES].T[0:nb]
            rank = jnp.zeros(st.shape, F32)
            for j in range(ns):
                sj = st[j:j + 1, :]
                rank = rank + jnp.where((sj > st) | ((sj == st) & (rblk > j)), 1.0, 0.0)
            top = jnp.where(rank < k_top, 1.0, 0.0)
            tops.append(jnp.concatenate([top, jnp.zeros((NSP - nb, LANES), F32)], axis=0).T)
        top = tops[0] if len(tops) == 1 else jnp.concatenate(tops, axis=0)
        return jnp.where(valid, top, 0.0)
    rank = jnp.zeros(score.shape, F32)
    for j in range(ns):
        sj = score[:, j:j + 1]
        beats = (sj > score) | ((sj == score) & (blk > j))
        rank = rank + jnp.where(beats, 1.0, 0.0)
    return jnp.where(valid & (rank < float(min(N_TOP, ns))), 1.0, 0.0)


def _compressed_probs(qb, kc, tpos, nc, scale):
    s = _dot_nt(qb, kc.astype(BF16)) * scale
    cidx = lax.broadcasted_iota(jnp.int32, (1, NCP), 1)
    cmask = (cidx < nc) & (cidx * D_CMP + (L_CMP - 1) <= tpos)
    return _masked_softmax_pre(jnp.where(cmask, s, NEG))


def _compressed_branch(qb, kc, vc, tpos, nc, scale):
    p = _compressed_probs(qb, kc, tpos, nc, scale)
    return p, _dot(p.astype(BF16), vc.astype(BF16))


def _keys_with_block_onehot(k_ref, koh_ref, k0, width):
    return jnp.concatenate([k_ref[pl.ds(k0, width), :].astype(BF16), koh_ref[pl.ds(k0, width), :]], axis=1)


def _nsa_prompt_body(q_ref, kc_ref, vc_ref, ks_ref, vs_ref, kw_ref, vw_ref, misc_ref, z_ref, c2s_ref, koh_ref,
                     o_ref, sc_ref, mx_ref, l_ref, acc_ref, *, tq, tk, T):
    qi = pl.program_id(1)
    t0 = pl.multiple_of(qi * tq, tq)
    R = GRP * tq
    scale = DH ** -0.5
    c1 = scale * LOG2E
    ns = -(-T // L_SEL)
    nc = T // D_CMP - CMP_R + 1
    n_top = min(N_TOP, ns)
    heads = range(HKV)

    def hd(g):
        return slice(g * DH, (g + 1) * DH)

    qbs = [jnp.concatenate([q_ref[:, (g * GRP + r) * DH:(g * GRP + r + 1) * DH] for r in range(GRP)],
                           axis=0).astype(BF16) for g in heads]
    tpos = t0 + (lax.broadcasted_iota(jnp.int32, (R, 1), 0) & (tq - 1))
    tpos_tok = t0 + lax.broadcasted_iota(jnp.int32, (tq, 1), 0)
    blk = lax.broadcasted_iota(jnp.int32, (1, NSP), 1)

    nwt = WINDOW // tq + 1
    tiles, starts = [[] for _ in heads], []
    for i in range(nwt):
        k0 = t0 - WINDOW + i * tq
        k0c = pl.multiple_of(jnp.maximum(k0, 0), tq)
        kpos = k0 + lax.broadcasted_iota(jnp.int32, (1, tq), 1)
        for g in heads:
            s = _dot_nt(qbs[g], kw_ref[pl.ds(k0c, tq), hd(g)].astype(BF16))
            if i == 0:
                s = jnp.where((kpos > tpos - WINDOW) & (kpos >= 0), s, NEG)
            elif i == nwt - 1:
                s = jnp.where(kpos <= tpos, s, NEG)
            else:
                s = jnp.where(k0 >= 0, s, NEG)
            tiles[g].append(s)
        starts.append(k0c)
    es_w = []
    for g in heads:
        s = jnp.concatenate(tiles[g], axis=1)
        es_w.append(jnp.exp2((s - jnp.max(s, axis=-1, keepdims=True)) * c1))
    o_ws = [None for _ in heads]
    for i in range(nwt):
        for g in heads:
            pv = _dot(es_w[g][:, i * tq:(i + 1) * tq].astype(BF16), vw_ref[pl.ds(starts[i], tq), hd(g)].astype(BF16))
            o_ws[g] = pv if i == 0 else o_ws[g] + pv
    o_ws = [o_ws[g] / jnp.sum(es_w[g], axis=-1, keepdims=True) for g in heads]

    comp = [_compressed_branch(qbs[g], kc_ref[g], vc_ref[g], tpos, nc, scale) for g in heads]
    imps = []
    for g in heads:
        p_c = comp[g][0]
        psum = p_c[0:tq]
        for r in range(1, GRP):
            psum = psum + p_c[r * tq:(r + 1) * tq]
        imps.append(_dot_exact_rhs(psum, c2s_ref[...]))

    def all_valid():
        v = jnp.where(blk <= tpos_tok // L_SEL, 1.0, 0.0)
        return tuple(v for _ in heads)

    sels = lax.cond(t0 + tq <= n_top * L_SEL, all_valid,
                    lambda: tuple(_select_blocks(imps[g], tpos_tok, ns) for g in heads))

    q_augs, q_earlys = [], []
    for g in heads:
        sel_bias = (sels[g] - 1.0) * (-NEG)
        early_bias = jnp.where(blk < lax.div(t0, L_SEL), sel_bias, NEG)
        q_augs.append(jnp.concatenate([qbs[g], jnp.concatenate([sel_bias.astype(BF16)] * GRP, axis=0)], axis=1))
        q_earlys.append(jnp.concatenate([qbs[g], jnp.concatenate([early_bias.astype(BF16)] * GRP, axis=0)], axis=1))

    def keys_aug(k0, width, g):
        return jnp.concatenate([ks_ref[pl.ds(k0, width), hd(g)].astype(BF16), koh_ref[pl.ds(k0, width), :]], axis=1)

    kpos_d = t0 + lax.broadcasted_iota(jnp.int32, (1, tq), 1)
    s_diags = [jnp.where(kpos_d <= tpos, _dot_nt(q_augs[g], keys_aug(t0, tq, g)), NEG) for g in heads]
    n_early = lax.div(t0 + tk - 1, tk)
    for g in heads:
        mx_ref[g] = functools.reduce(jnp.maximum, [s_diags[g][:, c * LANES:(c + 1) * LANES]
                                                   for c in range(tq // LANES)])

    def pass1(kt, carry):
        k0 = pl.multiple_of(kt * tk, tk)
        ss = [_dot_nt(q_earlys[g], keys_aug(k0, tk, g)) for g in heads]
        old = [mx_ref[g] for g in heads]
        for g in heads:
            mx = ss[g][:, 0:LANES]
            for c in range(1, tk // LANES):
                mx = jnp.maximum(mx, ss[g][:, c * LANES:(c + 1) * LANES])
            sc_ref[g, kt] = ss[g]
            mx_ref[g] = jnp.maximum(old[g], mx)
        return carry

    lax.fori_loop(0, n_early, pass1, 0)
    m2s = [jnp.broadcast_to(jnp.max(mx_ref[g], axis=-1, keepdims=True) * c1, (R, LANES)) for g in heads]
    e_ds = [jnp.concatenate([jnp.exp2(s_diags[g][:, c * LANES:(c + 1) * LANES] * c1 - m2s[g])
                             for c in range(tq // LANES)], axis=1) for g in heads]
    pv_ds = [_dot(e_ds[g].astype(BF16), vs_ref[pl.ds(t0, tq), hd(g)].astype(BF16)) for g in heads]
    for g in heads:
        mx_ref[g] = m2s[g]
        l_ref[g] = sum(e_ds[g][:, c * LANES:(c + 1) * LANES] for c in range(tq // LANES))
        acc_ref[g] = pv_ds[g]

    def pass2(kt, carry):
        k0 = pl.multiple_of(kt * tk, tk)
        ss = [sc_ref[g, kt] for g in heads]
        m2 = [mx_ref[g] for g in heads]
        l_old = [l_ref[g] for g in heads]
        acc_old = [acc_ref[g] for g in heads]
        ebs, lsums = [], []
        for g in heads:
            es = [jnp.exp2(ss[g][:, c * LANES:(c + 1) * LANES] * c1 - m2[g]) for c in range(tk // LANES)]
            lsum = es[0]
            for ec in es[1:]:
                lsum = lsum + ec
            lsums.append(lsum)
            ebs.append(jnp.concatenate(es, axis=1).astype(BF16))
        pvs = [_dot(ebs[g], vs_ref[pl.ds(k0, tk), hd(g)].astype(BF16)) for g in heads]
        for g in heads:
            l_ref[g] = l_old[g] + lsums[g]
            acc_ref[g] = acc_old[g] + pvs[g]
        return carry

    lax.fori_loop(0, n_early, pass2, 0)

    outs = []
    for g in heads:
        o_s = acc_ref[g] / jnp.sum(l_ref[g], axis=-1, keepdims=True)
        gm = _sigmoid(misc_ref[:, g * LANES:g * LANES + GRP * 3])

        def gate(j):
            return jnp.concatenate([gm[:, r * 3 + j:r * 3 + j + 1] for r in range(GRP)], axis=0)

        outs.append(gate(0) * comp[g][1] + gate(1) * o_s + gate(2) * o_ws[g])
    for g in heads:
        for r in range(GRP):
            c = (g * GRP + r) * DH
            o_ref[:, c:c + DH] = (outs[g][r * tq:(r + 1) * tq] * _silu(z_ref[:, c:c + DH])).astype(BF16)


def _nsa_prompt(P, kcv, c2s, koh, B, T, tq, tk):
    assert T % tq == 0 and tq & (tq - 1) == 0 and tk % tq == 0 and T % tk == 0 and WINDOW % tq == 0
    nq = T // tq
    qw = H_A * DH
    kw = HKV * DH
    R = GRP * tq

    def kvspec(slot):
        return pl.BlockSpec((T, kw), lambda b, i: (b, A_KV // kw + slot))

    return pl.pallas_call(
        functools.partial(_nsa_prompt_body, tq=tq, tk=tk, T=T),
        grid=(B, nq),
        in_specs=[pl.BlockSpec((tq, qw), lambda b, i: (b * nq + i, A_Q // qw)),
                  pl.BlockSpec((None, None, HKV, NCP, DH), lambda b, i: (b, 0, 0, 0, 0)),
                  pl.BlockSpec((None, None, HKV, NCP, DH), lambda b, i: (b, 1, 0, 0, 0)),
                  kvspec(2), kvspec(3), kvspec(4), kvspec(5),
                  pl.BlockSpec((tq, HKV * LANES), lambda b, i: (b * nq + i, A_MISC // (HKV * LANES))),
                  pl.BlockSpec((tq, qw), lambda b, i: (b * nq + i, A_ZA // qw)),
                  pl.BlockSpec((NCP, NSP), lambda b, i: (0, 0)),
                  pl.BlockSpec((T, NSP), lambda b, i: (0, 0))],
        out_specs=pl.BlockSpec((tq, qw), lambda b, i: (b * nq + i, 0)),
        out_shape=jax.ShapeDtypeStruct((B * T, qw), BF16),
        scratch_shapes=[pltpu.VMEM((HKV, T // tk, R, tk), F32), pltpu.VMEM((HKV, R, LANES), F32),
                        pltpu.VMEM((HKV, R, LANES), F32), pltpu.VMEM((HKV, R, DH), F32)],
        compiler_params=_params(("arbitrary", "arbitrary")),
    )(P, kcv, kcv, P, P, P, P, P, P, c2s, koh)


def _pad_rows(x, rows):
    return jnp.concatenate([x, jnp.zeros((rows - x.shape[0], x.shape[1]), x.dtype)], axis=0)


def _nsa_sample_body(pt_ref, cache_ref, win_ref, q_ref, kvr_ref, kvw_ref, misc_ref, z_ref, w1_ref, cb_ref, w2_ref,
                     c2s_ref, koh_ref, o_ref, wout_ref, pbuf, sem, *, n_pages, page, past_len, dec_seq, wb):
    b = pl.program_id(0)
    scale = DH ** -0.5
    c1 = scale * LOG2E
    L = past_len + dec_seq
    ns = -(-L // L_SEL)
    nc = L // D_CMP - CMP_R + 1
    R = GRP * TP
    per_page = page // D_CMP
    nch = N_KV_SLOTS * HKV
    grp_rows = D_CMP * nch
    pitch = grp_rows + 1

    def page_copies(seq, slot):
        cps = []
        for j in range(n_pages):
            pid = pt_ref[seq * n_pages + j]
            for n in range(per_page):
                cps.append(pltpu.make_async_copy(
                    cache_ref.at[pid, pl.ds(n * grp_rows, grp_rows), :],
                    pbuf.at[slot, pl.ds((j * per_page + n) * pitch, grp_rows), :], sem.at[slot]))
        return cps

    slot = lax.rem(b, 2)

    @pl.when(b == 0)
    def _():
        for cp in page_copies(0, 0):
            cp.start()

    for cp in page_copies(b, slot):
        cp.wait()

    last = pl.num_programs(0) - 1
    next_copies = page_copies(jnp.minimum(b + 1, last), 1 - slot)
    copies_per_group = len(next_copies) // (2 * (D_CMP // 2))

    def gather(p, ch):
        return pbuf[slot, pl.ds(p * nch + ch, NCP, stride=pitch), :]

    trow = lax.broadcasted_iota(jnp.int32, (R, 1), 0) & (TP - 1)
    tpos = past_len + trow
    tpos_tok = past_len + lax.broadcasted_iota(jnp.int32, (TP, 1), 0)
    lane = lax.broadcasted_iota(jnp.int32, (1, LANES), 1)
    new_pos = past_len + lane
    new_ok = (new_pos <= tpos) & (new_pos < L)
    new_blk = (past_len + lax.broadcasted_iota(jnp.int32, (LANES, 1), 0)) // L_SEL
    new_oh = jnp.where(lax.broadcasted_iota(jnp.int32, (LANES, NSP), 1) == new_blk, 1.0, 0.0).astype(BF16)
    wpos = (past_len - wb) + lax.broadcasted_iota(jnp.int32, (1, wb), 1)
    wch = 2 * HKV

    heads = range(HKV)
    n_iter = D_CMP // 2
    qbs = [jnp.concatenate([q_ref[:, (g * GRP + r) * DH:(g * GRP + r + 1) * DH] for r in range(GRP)],
                           axis=0).astype(BF16) for g in heads]
    raw = [[] for _ in heads]
    win = {}

    def stage_window_scores():
        es = []
        for g in heads:
            s_w = _dot_nt(qbs[g], win_ref[pl.ds(g, wb, stride=wch), :].astype(BF16))
            s_w = jnp.where((wpos <= tpos) & (wpos > tpos - WINDOW), s_w, NEG)
            s_n = _dot_nt(qbs[g], _pad_rows(kvw_ref[:, g * DH:(g + 1) * DH], LANES).astype(BF16))
            s_n = jnp.where(new_ok & (new_pos > tpos - WINDOW), s_n, NEG)
            s = jnp.concatenate([s_w, s_n], axis=1)
            es.append(jnp.exp2((s - jnp.max(s, axis=-1, keepdims=True)) * c1))
        win["e"] = es

    def stage_window_out():
        outs_w = []
        for g in heads:
            e = win["e"][g]
            o_w = (_dot(e[:, :wb].astype(BF16), win_ref[pl.ds(HKV + g, wb, stride=wch), :].astype(BF16))
                   + _dot(e[:, wb:].astype(BF16),
                          _pad_rows(kvw_ref[:, (HKV + g) * DH:(HKV + g + 1) * DH], LANES).astype(BF16)))
            outs_w.append(o_w / jnp.sum(e, axis=-1, keepdims=True))
        win["o"] = outs_w

    def stage_raw_scores(p_lo, p_hi):
        for p in range(p_lo, p_hi):
            for g in heads:
                raw[g].append(_dot_nt(qbs[g], gather(p, 2 * HKV + g).astype(BF16)))

    def stage_raw_new():
        for g in heads:
            ksel_col = (2 * HKV + g) * DH
            raw[g].append(_dot_nt(qbs[g], _pad_rows(kvr_ref[:, ksel_col:ksel_col + DH], LANES).astype(BF16)))


    def compress_step(s, pp, u0, u1):
        xs = [jnp.concatenate([gather(2 * pp, s * HKV + g), gather(2 * pp + 1, s * HKV + g)], axis=1) for g in heads]
        xp = jnp.concatenate(xs, axis=0).astype(BF16)
        u0 = u0 + _dot(xp, w1_ref[s, pl.ds(pp * 2 * DH, 2 * DH), :])
        u1 = u1 + _dot(xp, w1_ref[s, pl.ds(D_CMP * DH + pp * 2 * DH, 2 * DH), :])
        grp = s * n_iter + pp
        for cp in next_copies[grp * copies_per_group:(grp + 1) * copies_per_group]:
            cp.start()
        return u0, u1

    def compress_out(s, u0, u1):
        return [_dot(_gelu_tanh(_compress_hidden(u0[g * NCP:(g + 1) * NCP], u1[g * NCP:(g + 1) * NCP],
                                                 cb_ref[s])).astype(BF16), w2_ref[s]) for g in heads]

    quarter = D_CMP // 4
    early = {0: stage_window_scores, 1: lambda: stage_raw_scores(0, quarter),
             2: lambda: stage_raw_scores(quarter, 2 * quarter), 3: stage_window_out,
             4: lambda: stage_raw_scores(2 * quarter, 3 * quarter),
             5: lambda: stage_raw_scores(3 * quarter, D_CMP), 6: stage_raw_new}
    u0 = u1 = jnp.zeros((HKV * NCP, CMP_HID), F32)
    for pp in range(n_iter):
        u0, u1 = compress_step(0, pp, u0, u1)
        if pp in early:
            early[pp]()
    kc = compress_out(0, u0, u1)

    st = {}

    def stage_probs():
        st["p_c"] = [_compressed_probs(qbs[g], kc[g], tpos, nc, scale) for g in heads]

    def stage_select():
        imps = []
        for g in heads:
            p_c = st["p_c"][g]
            psum = p_c[0:TP]
            for r in range(1, GRP):
                psum = psum + p_c[r * TP:(r + 1) * TP]
            imps.append(_dot_exact_rhs(psum, c2s_ref[...]))
        st["sel"] = [_select_blocks(imps[g], tpos_tok, ns) for g in heads]

    def stage_scores():
        biases = [jnp.concatenate([((st["sel"][g] - 1.0) * (-NEG)).astype(BF16)] * GRP, axis=0) for g in heads]
        bias_past = [_dot_nt(biases[g], koh_ref[...]) for g in heads]
        bias_new = [jnp.where(new_ok, _dot_nt(biases[g], new_oh), NEG) for g in heads]
        es = []
        for g in heads:
            s = jnp.concatenate([pc + bias_past[g] for pc in raw[g][:D_CMP]] + [raw[g][D_CMP] + bias_new[g]], axis=1)
            es.append(jnp.exp2((s - jnp.max(s, axis=-1, keepdims=True)) * c1))
        st["e"] = es
        st["o_s"] = [_dot(es[g][:, D_CMP * NCP:].astype(BF16),
                          _pad_rows(kvr_ref[:, (3 * HKV + g) * DH:(3 * HKV + g + 1) * DH], LANES).astype(BF16))
                     for g in heads]

    def stage_pv(p_lo, p_hi):
        for p in range(p_lo, p_hi):
            for g in heads:
                st["o_s"][g] = st["o_s"][g] + _dot(st["e"][g][:, p * NCP:(p + 1) * NCP].astype(BF16),
                                                  gather(p, 3 * HKV + g).astype(BF16))

    after = {0: stage_probs, 3: stage_select, 6: stage_scores, 7: lambda: stage_pv(0, D_CMP // 2)}
    u0 = u1 = jnp.zeros((HKV * NCP, CMP_HID), F32)
    for pp in range(n_iter):
        u0, u1 = compress_step(1, pp, u0, u1)
        if pp in after:
            after[pp]()
    vc = compress_out(1, u0, u1)
    stage_pv(D_CMP // 2, D_CMP)

    outs = []
    for g in heads:
        o_c = _dot(st["p_c"][g].astype(BF16), vc[g].astype(BF16))
        o_s = st["o_s"][g] / jnp.sum(st["e"][g], axis=-1, keepdims=True)
        gm = _sigmoid(misc_ref[:, g * LANES:g * LANES + GRP * 3])

        def gate(j):
            return jnp.concatenate([gm[:, r * 3 + j:r * 3 + j + 1] for r in range(GRP)], axis=0)

        outs.append(gate(0) * o_c + gate(1) * o_s + gate(2) * win["o"][g])

    for g in range(HKV):
        for r in range(GRP):
            c = (g * GRP + r) * DH
            o_ref[:, c:c + DH] = outs[g][r * TP:(r + 1) * TP] * _silu(z_ref[:, c:c + DH])

    wout_ref[pl.ds(0, (wb - dec_seq) * wch), :] = win_ref[pl.ds(dec_seq * wch, (wb - dec_seq) * wch), :]
    for t in range(dec_seq):
        for ch in range(wch):
            wout_ref[pl.ds((wb - dec_seq + t) * wch + ch, 1), :] = kvw_ref[t:t + 1, ch * DH:(ch + 1) * DH]

    @pl.when(b == last)
    def _():
        for cp in next_copies:
            cp.wait()


def _nsa_sample(Ps, cache, cache_win, page_table, w1, cb, w2, c2s, dec_seq):
    NB, n_pages = page_table.shape
    n_pool, prow, _ = cache.shape
    page = prow // (N_KV_SLOTS * HKV)
    wb = cache_win.shape[1] // (2 * HKV)
    past_len = n_pages * page
    L = past_len + dec_seq
    assert wb == WINDOW and (L // D_CMP) == NCP and NCP * D_CMP == past_len and dec_seq <= TP
    assert -(-L // L_SEL) < NSP and page % D_CMP == 0 and (dec_seq * 2 * HKV) % SUBLANES == 0
    qw = H_A * DH
    nch = N_KV_SLOTS * HKV
    koh = jnp.asarray(np.arange(NCP)[:, None] * D_CMP // L_SEL == np.arange(NSP)[None, :], dtype=BF16)

    grid_spec = pltpu.PrefetchScalarGridSpec(
        num_scalar_prefetch=1,
        grid=(NB,),
        in_specs=[
            pl.BlockSpec(memory_space=pl.ANY),
            pl.BlockSpec((None, wb * 2 * HKV, DH), lambda b, pt: (b, 0, 0)),
            pl.BlockSpec((TP, qw), lambda b, pt: (b, A_Q // qw)),
            pl.BlockSpec((TP, 4 * HKV * DH), lambda b, pt: (b, A_KV // (4 * HKV * DH))),
            pl.BlockSpec((TP, 2 * HKV * DH), lambda b, pt: (b, (A_KV + 4 * HKV * DH) // (2 * HKV * DH))),
            pl.BlockSpec((TP, HKV * LANES), lambda b, pt: (b, A_MISC // (HKV * LANES))),
            pl.BlockSpec((TP, qw), lambda b, pt: (b, A_ZA // qw)),
            pl.BlockSpec((2, L_CMP * DH, CMP_HID), lambda b, pt: (0, 0, 0)),
            pl.BlockSpec((2, 1, CMP_HID), lambda b, pt: (0, 0, 0)),
            pl.BlockSpec((2, CMP_HID, DH), lambda b, pt: (0, 0, 0)),
            pl.BlockSpec((NCP, NSP), lambda b, pt: (0, 0)),
            pl.BlockSpec((NCP, NSP), lambda b, pt: (0, 0))],
        out_specs=[pl.BlockSpec((TP, qw), lambda b, pt: (b, 0)),
                   pl.BlockSpec((None, wb * 2 * HKV, DH), lambda b, pt: (b, 0, 0))],
        scratch_shapes=[pltpu.VMEM((2, NCP * (D_CMP * nch + 1), DH), F32), pltpu.SemaphoreType.DMA((2,))],
    )
    return pl.pallas_call(
        functools.partial(_nsa_sample_body, n_pages=n_pages, page=page, past_len=past_len, dec_seq=dec_seq, wb=wb),
        grid_spec=grid_spec,
        out_shape=[jax.ShapeDtypeStruct((NB * TP, qw), F32),
                   jax.ShapeDtypeStruct((NB, wb * 2 * HKV, DH), F32)],
        compiler_params=_params(("arbitrary",)),
    )(page_table.reshape(-1), cache, cache_win, Ps, Ps, Ps, Ps, Ps, w1, cb, w2, c2s, koh)


def _rec_chunk(q, k, v, la, gate, S, C, last_row, nh, K, V):
    sb = min(16, C)
    ri = lax.broadcasted_iota(jnp.int32, (C, C), 0)
    cj = lax.broadcasted_iota(jnp.int32, (C, C), 1)
    causal = cj <= ri
    tri = jnp.where(causal, 1.0, 0.0).astype(BF16)
    b = _dot_exact_lhs(tri, la)
    qe = (q * jnp.exp(b)).astype(BF16)
    qis, kis = [], []
    k_ref_prev, c_prev = None, None
    for i in range(C // sb):
        r0 = i * sb
        ci = b[r0 + sb // 2:r0 + sb // 2 + 1, :]
        qis.append((q[r0:r0 + sb] * jnp.exp(b[r0:r0 + sb] - ci)).astype(BF16))
        k_new = k[r0:r0 + sb] * jnp.exp(jnp.minimum(ci - b[r0:r0 + sb], 80.0))
        k_scaled = k_new if i == 0 else jnp.concatenate([k_ref_prev * jnp.exp(ci - c_prev), k_new], axis=0)
        k_ref_prev, c_prev = k_scaled, ci
        rest = C - r0 - sb
        kis.append((k_scaled if rest == 0 else
                    jnp.concatenate([k_scaled, jnp.zeros((rest, k.shape[1]), F32)], axis=0)).astype(BF16))
    b_last = b[last_row:last_row + 1, :]
    rowi = lax.broadcasted_iota(jnp.int32, (C, 1), 0)
    kd = jnp.where(rowi <= last_row, k * jnp.exp(jnp.minimum(b_last - b, 0.0)), 0.0)
    stack = jnp.concatenate([kd, jnp.broadcast_to(b_last, (SUBLANES, nh * K)),
                             jnp.zeros((LANES - C - SUBLANES, nh * K), F32)], axis=0)
    vb = v.astype(BF16)
    v_pad = jnp.concatenate([vb, jnp.zeros((LANES - C, nh * V), BF16)], axis=0)
    outs, s_new = [], []
    for h in range(nh):
        ks = slice(h * K, (h + 1) * K)
        vs = slice(h * V, (h + 1) * V)
        o = _dot(qe[:, ks], S[h].astype(BF16))
        rows = [_dot_nt(qi[:, ks], ki[:, ks]) for qi, ki in zip(qis, kis)]
        att = rows[0] if len(rows) == 1 else jnp.concatenate(rows, axis=0)
        att = jnp.where(causal, att, 0.0)
        o = o + _dot(att.astype(BF16), vb[:, vs])
        stack_t = stack[:, ks].T
        a_col = jnp.exp(stack_t[:, C:C + 1])
        s_new.append(a_col * S[h] + _dot(stack_t.astype(BF16), v_pad[:, vs]))
        ms = jnp.mean(o * o, axis=-1, keepdims=True)
        outs.append(o * lax.rsqrt(ms + EPS) * gate[:, vs])
    return outs, s_new


def _rec_body(*refs, variant, nh, K, V, C, TB, SB, last_row, has_s0, layer_idx):
    refs = list(refs)
    q_ref, k_ref, v_ref, z_ref = refs[:4]
    pos = 4
    if variant == "gla":
        misc_ref, w2_ref, gb_ref = refs[pos:pos + 3]
        pos += 3
    else:
        lb_ref = refs[pos]
        pos += 1
    gn_ref = refs[pos]
    pos += 1
    if has_s0:
        s0_ref = refs[pos]
        pos += 1
    o_ref, s_ref = refs[pos:pos + 2]
    n_chunks = TB // C
    single = n_chunks == 1 and has_s0

    if not single:
        @pl.when(pl.program_id(2) == 0)
        def _():
            if has_s0:
                s_ref[...] = s0_ref[...]
            else:
                s_ref[...] = jnp.zeros(s_ref.shape, F32)

    if variant == "hgrn":
        lg = lb_ref[...]
        e = jnp.exp(lg - jnp.max(lg, axis=0, keepdims=True))
        prob = e / jnp.sum(e, axis=0, keepdims=True)
        lb = prob[1:2]
        for i in range(2, layer_idx + 1):
            lb = lb + prob[i:i + 1]
        if SB > 1:
            lb = jnp.concatenate([lb] * SB, axis=1)
    gn_all = jnp.concatenate([gn_ref[...]] * (nh * SB), axis=1)

    def chunk(c0):
        def rows(s):
            return pl.ds(s * TB + c0, C)

        def cat(f):
            return f(0) if SB == 1 else jnp.concatenate([f(s) for s in range(SB)], axis=1)

        qr = cat(lambda s: q_ref[rows(s), :])
        kr = cat(lambda s: k_ref[rows(s), :])
        v = cat(lambda s: v_ref[rows(s), :])
        gate = _silu(cat(lambda s: z_ref[rows(s), :])) * gn_all
        if variant == "gla":
            w2 = w2_ref[...].astype(BF16)
            zg = cat(lambda s: _dot(misc_ref[rows(s), :].astype(BF16), w2) + gb_ref[...])
            q, k, la = qr * (K ** -0.5), kr, _log_sigmoid(zg) / GLA_GATE_NORM
        else:
            t = jnp.exp(-jnp.abs(kr))
            r = 1.0 / (1.0 + t)
            tr = t * r
            nonneg = kr >= 0.0
            sig = jnp.where(nonneg, r, tr)
            nsig = jnp.where(nonneg, tr, r)
            q, k, la = _silu(qr), (1.0 - lb) * nsig, jnp.log(lb + (1.0 - lb) * sig)
        src = s0_ref if single else s_ref
        states = [src[s, h] for s in range(SB) for h in range(nh)]
        outs, s_new = _rec_chunk(q, k, v, la, gate, states, C, last_row, SB * nh, K, V)
        for s in range(SB):
            for h in range(nh):
                o_ref[rows(s), h * V:(h + 1) * V] = outs[s * nh + h].astype(o_ref.dtype)
                s_ref[s, h] = s_new[s * nh + h]

    if n_chunks == 1:
        chunk(0)
    else:
        def body(ci, carry):
            chunk(pl.multiple_of(ci * C, C))
            return carry

        lax.fori_loop(0, n_chunks, body, 0, unroll=8)


def _recurrence(P, cols, extra, gn, s0, *, variant, B, T, H, K, V, C, TB, nh, last_row, out_dtype, SB=1,
                layer_idx=1):
    cq, ck, cv, cz = cols
    assert H % nh == 0 and T % TB == 0 and TB % C == 0 and B % SB == 0
    nt = T // TB
    assert SB == 1 or nt == 1

    def colspec(c0, w):
        assert c0 % (nh * w) == 0
        return pl.BlockSpec((SB * TB, nh * w), lambda b, j, t: (b * nt + t, c0 // (nh * w) + j))

    in_specs = [colspec(cq, K), colspec(ck, K), colspec(cv, V), colspec(cz, V)]
    args = [P, P, P, P]
    if variant == "gla":
        w2p, gb = extra
        in_specs += [pl.BlockSpec((SB * TB, LANES), lambda b, j, t: (b * nt + t, A_MISC // LANES)),
                     pl.BlockSpec((LANES, nh * K), lambda b, j, t: (0, j)),
                     pl.BlockSpec((1, nh * K), lambda b, j, t: (0, j))]
        args += [P, w2p, gb]
    else:
        (lb_logits,) = extra
        in_specs += [pl.BlockSpec((lb_logits.shape[0], nh * K), lambda b, j, t: (0, j))]
        args += [lb_logits]
    in_specs += [pl.BlockSpec((1, V), lambda b, j, t: (0, 0))]
    args += [gn]
    st_spec = pl.BlockSpec((None, SB, nh, K, V), lambda b, j, t: (0, b, j, 0, 0))
    if s0 is not None:
        in_specs += [st_spec]
        args += [s0]
    return pl.pallas_call(
        functools.partial(_rec_body, variant=variant, nh=nh, K=K, V=V, C=C, TB=TB, SB=SB, last_row=last_row,
                          has_s0=s0 is not None, layer_idx=layer_idx),
        grid=(B // SB, H // nh, nt),
        in_specs=in_specs,
        out_specs=[pl.BlockSpec((SB * TB, nh * V), lambda b, j, t: (b * nt + t, j)), st_spec],
        out_shape=[jax.ShapeDtypeStruct((B * T, H * V), out_dtype),
                   jax.ShapeDtypeStruct((1, B, H, K, V), F32)],
        compiler_params=_params(("arbitrary", "arbitrary", "arbitrary")),
    )(*args)


def _layout_a_w_in_body(w_ref, o_ref):
    off = [int(v) for v in np.concatenate([[0], np.cumsum(A_SIZES)])]
    q, kv, gbr, za, qb, kb, vb, lr, zb = [w_ref[off[i]:off[i + 1], :] for i in range(len(A_SIZES))]
    cols = w_ref.shape[1]
    z = lambda n: jnp.zeros((n, cols), F32)
    misc0 = jnp.concatenate([gbr[:GRP * 3], lr, z(LANES - GRP * 3 - GLA_LR)], axis=0)
    misc1 = jnp.concatenate([gbr[GRP * 3:], z(LANES - GRP * 3)], axis=0)
    out = jnp.concatenate([q, vb, zb, za, qb, kb, kv, misc0, misc1, z(NA - A_MISC - HKV * LANES)], axis=0)
    o_ref[...] = out.astype(BF16)


def _layout_a_w_in(wt, tc=256):
    n, D = wt.shape
    assert D % tc == 0 and n == sum(A_SIZES)
    return pl.pallas_call(
        _layout_a_w_in_body,
        grid=(D // tc,),
        in_specs=[pl.BlockSpec((n, tc), lambda i: (0, i))],
        out_specs=pl.BlockSpec((NA, tc), lambda i: (0, i)),
        out_shape=jax.ShapeDtypeStruct((NA, D), BF16),
        compiler_params=_params(("arbitrary",)),
    )(wt)


def _cmp_to_sel():
    c_start = np.arange(NCP)[:, None] * D_CMP
    s_start = np.arange(NSP)[None, :] * L_SEL
    overlap = np.clip(np.minimum(c_start + L_CMP, s_start + L_SEL) - np.maximum(c_start, s_start), 0, None)
    return jnp.asarray(overlap / D_CMP, dtype=BF16)


class _Tiles(NamedTuple):
    proj_rows: int
    proj_cols: int
    out_rows: int
    nsa_q: int
    nsa_k: int
    rec_chunk: int
    rec_block: int
    gla_heads: int
    hgrn_heads: int
    gla_seqs: int
    hgrn_seqs: int


def _tiles(n_decode):
    return _Tiles(proj_rows=1024, proj_cols=N_KV_SLOTS * HKV * DH, out_rows=512, nsa_q=256, nsa_k=512,
                  rec_chunk=64, rec_block=512, gla_heads=H_B, hgrn_heads=8,
                  gla_seqs=math.gcd(n_decode, 16), hgrn_seqs=math.gcd(n_decode, 8))


def kernel(x_prompt, x_sample, cache_kv, cache_win, state_gla, state_hgrn, page_table, a_norm, a_w_in, a_gla_w2,
           a_gla_b, a_gla_gn, a_cmp_pe, a_cmp_w1, a_cmp_b1, a_cmp_w2, a_w_out, c_norm, c_w_in, c_lb_logits, c_gn,
           c_w_out, final_norm):
    B, T, D = x_prompt.shape
    NB, TS, _ = x_sample.shape
    n_pool, page = cache_kv.shape[1], cache_kv.shape[2]
    wb = cache_win.shape[2]
    assert a_norm.shape[0] == 1 and c_norm.shape[0] == 1 and c_lb_logits.shape[0] == 2

    wa = _layout_a_w_in(a_w_in[0].T)
    wc = c_w_in[0].astype(BF16)
    wao = a_w_out[0].astype(BF16)
    wco = c_w_out[0].astype(BF16)
    w1 = a_cmp_w1[0].astype(BF16)
    w2 = a_cmp_w2[0].astype(BF16)
    pe = a_cmp_pe[0].reshape(2, 1, L_CMP * DH)
    b1 = a_cmp_b1[0].reshape(2, 1, CMP_HID)
    w2p = jnp.zeros((LANES, H_B * DK_B), F32).at[MISC_LR:MISC_LR + GLA_LR, :].set(a_gla_w2[0])
    gb = a_gla_b[0].reshape(1, H_B * DK_B)
    c2s = _cmp_to_sel()
    a_nw = a_norm[0].reshape(1, D)
    c_nw = c_norm[0].reshape(1, D)
    f_nw = final_norm.reshape(1, D)
    gla_gn = a_gla_gn[0].reshape(1, DV_B)
    hg_gn = c_gn[0].reshape(1, DV_C)

    xp = x_prompt.reshape(B * T, D)
    xs = jnp.pad(x_sample, ((0, 0), (0, TP - TS), (0, 0))).reshape(NB * TP, D)

    cb = _cmp_bias(pe, w1, b1)

    tl = _tiles(NB)

    Pp, kv_rows_p = _norm_matmul(xp, a_nw, wa, tm=tl.proj_rows, tn=tl.proj_cols, copy_tile=A_KV // tl.proj_cols,
                                 w_rows=True)
    kcv = _compress_prompt(Pp, w1, cb, w2, B, T)
    koh = jnp.asarray(np.arange(T)[:, None] // L_SEL == np.arange(NSP)[None, :], dtype=BF16)
    oa_p = _nsa_prompt(Pp, kcv, c2s, koh, B, T, tq=tl.nsa_q, tk=tl.nsa_k)
    ob_p, gla_p = _recurrence(Pp, (A_QB, A_KB, A_VB, A_ZB), (w2p, gb), gla_gn, None, variant="gla", B=B, T=T,
                              H=H_B, K=DK_B, V=DV_B, C=tl.rec_chunk, TB=tl.rec_block, nh=tl.gla_heads,
                              last_row=tl.rec_chunk - 1, out_dtype=BF16)
    y1p = _out_proj(oa_p, 0, ob_p, 0, wao, xp, a_nw, tm=tl.out_rows, final_norm=False)

    def pad_tokens(p):
        return jnp.pad(p.reshape(NB, TS, -1), ((0, 0), (0, TP - TS), (0, 0))).reshape(NB * TP, -1)

    Ps = pad_tokens(_norm_matmul(x_sample.reshape(NB * TS, D), a_nw, wa, tm=NB * TS, tn=tl.proj_cols, w_rows=True))
    cache2 = cache_kv[0].reshape(n_pool, page * N_KV_SLOTS * HKV, DH)
    win2 = cache_win[0].reshape(NB, wb * 2 * HKV, DH)
    oa_s, win_out = _nsa_sample(Ps, cache2, win2, page_table, w1, cb, w2, c2s, TS)
    ob_s, gla_s = _recurrence(Ps, (A_QB, A_KB, A_VB, A_ZB), (w2p, gb), gla_gn, state_gla, variant="gla", B=NB, T=TP,
                              H=H_B, K=DK_B, V=DV_B, C=TP, TB=TP, nh=H_B, last_row=TS - 1, out_dtype=F32,
                              SB=tl.gla_seqs)
    y1s = _out_proj(oa_s, 0, ob_s, 0, wao, xs, a_nw, tm=min(tl.out_rows, NB * TP), final_norm=False)

    Pc = _norm_matmul(y1p, c_nw, wc, tm=tl.proj_rows, tn=tl.proj_cols)
    hc = H_C * DK_C
    oc_p, hg_p = _recurrence(Pc, (0, hc, 2 * hc, 3 * hc), (c_lb_logits,), hg_gn, None, variant="hgrn", B=B, T=T,
                             H=H_C, K=DK_C, V=DV_C, C=tl.rec_chunk, TB=tl.rec_block, nh=tl.hgrn_heads,
                             last_row=tl.rec_chunk - 1, out_dtype=BF16)
    y_prompt = _out_proj(oc_p, 0, oc_p, 1, wco, y1p, f_nw, tm=tl.out_rows, final_norm=True)

    y1s_real = y1s.reshape(NB, TP, D)[:, :TS].reshape(NB * TS, D)
    Pcs = pad_tokens(_norm_matmul(y1s_real, c_nw, wc, tm=NB * TS, tn=tl.proj_cols))
    oc_s, hg_s = _recurrence(Pcs, (0, hc, 2 * hc, 3 * hc), (c_lb_logits,), hg_gn, state_hgrn, variant="hgrn", B=NB,
                             T=TP, H=H_C, K=DK_C, V=DV_C, C=TP, TB=TP, nh=H_C, last_row=TS - 1, out_dtype=F32,
                             SB=tl.hgrn_seqs)
    y_sample = _out_proj(oc_s, 0, oc_s, 1, wco, y1s, f_nw, tm=min(tl.out_rows, NB * TP), final_norm=True)

    kvw = 6 * HKV * DH
    nrw = N_KV_SLOTS * HKV * DH
    Pp3 = Pp.reshape(B, T, NA)
    Ps3 = Ps.reshape(NB, TP, NA)
    kv_p = kv_rows_p.reshape(1, B, T, N_KV_SLOTS, HKV, DH)
    kv_s = Ps3[:, :TS, A_KV:A_KV + nrw].reshape(1, NB, TS, N_KV_SLOTS, HKV, DH)
    keep = min(WINDOW, T)
    win_p = Pp3[:, T - keep:, A_KV + nrw:A_KV + kvw].reshape(1, B, keep, 2, HKV, DH)
    win_s = win_out.reshape(1, NB, wb, 2, HKV, DH)
    return (y_prompt.reshape(B, T, D), y_sample.reshape(NB, TP, D)[:, :TS], kv_p, kv_s, win_p, win_s,
            gla_p, gla_s, hg_p, hg_s)
```
